```python
import functools
import jax, jax.numpy as jnp
from jax import lax
import numpy as np

D_MODEL = 1024
BATCH = 4
SEQ = 4096
DEPTH = 4
DEC_BATCH = 32
DEC_SEQ = 4
PAST_LEN = 8192
PAGE_SIZE = 128

GLA_HEADS = 4
GLA_DK = 64
GLA_DV = 128
GLA_GATE_RANK = 16
GLA_TAU = 16.0
GLA_CHUNK = 16
GLA_WIDTH = GLA_HEADS * GLA_DV
NSA_HEADS = 8
NSA_KV_GROUPS = 2
NSA_REP = NSA_HEADS // NSA_KV_GROUPS
NSA_DH = 64
NSA_WIDTH = NSA_HEADS * NSA_DH
KV_WIDTH = NSA_KV_GROUPS * NSA_DH
CMP_STRIDE = 16
CMP_LEN = 2 * CMP_STRIDE
SEL_BLOCK = 64
SEL_PER = SEL_BLOCK // CMP_STRIDE
SEL_TOP_N = 16
WINDOW = 512
Q_BLOCK = 64
D_FF = 2816
CONV_W = 3
RMS_EPS = 1e-6
BIG = 1e30

IN_SIZES = (GLA_HEADS * GLA_DK, GLA_HEADS * GLA_DK, GLA_WIDTH, GLA_GATE_RANK, GLA_WIDTH,
            NSA_WIDTH, KV_WIDTH, KV_WIDTH, KV_WIDTH, KV_WIDTH, KV_WIDTH, KV_WIDTH, 3 * NSA_HEADS)
D_IN = sum(IN_SIZES)

kernel_name = "hymba_gla_nsa_convffn_step"


def _rms(x, g):
    xf = x.astype(jnp.float32)
    y = xf * lax.rsqrt(jnp.mean(xf * xf, axis=-1, keepdims=True) + RMS_EPS)
    return (y * g.astype(jnp.float32)).astype(x.dtype)


def _alibi_slopes():
    h = jnp.arange(1, NSA_HEADS + 1, dtype=jnp.float32)
    return (2.0 ** (-8.0 * h / NSA_HEADS)).reshape(NSA_KV_GROUPS, NSA_REP)


def _masked_softmax(s, mask):
    s = jnp.where(mask, s, -BIG)
    p = jax.nn.softmax(s, axis=-1)
    return jnp.where(mask, p, 0.0)


def _gla(q, k, v, log_a, s0):
    B, L = q.shape[:2]
    n = -(-L // GLA_CHUNK)
    pad = n * GLA_CHUNK - L

    def blk(a):
        a = jnp.pad(a.astype(jnp.float32), ((0, 0), (0, pad), (0, 0), (0, 0)))
        return a.reshape(B, n, GLA_CHUNK, GLA_HEADS, a.shape[-1]).transpose(1, 0, 3, 2, 4)

    q, k, v, la = blk(q), blk(k), blk(v), blk(log_a)
    b = jnp.cumsum(la, axis=3)
    b_last = b[:, :, :, -1:, :]
    qe = q * jnp.exp(b)
    ke = k * jnp.exp(-b)
    kd = k * jnp.exp(b_last - b)
    causal = jnp.tril(jnp.ones((GLA_CHUNK, GLA_CHUNK), bool))
    att = jnp.where(causal, jnp.einsum('nbhid,nbhjd->nbhij', qe, ke), 0.0)
    o_intra = jnp.einsum('nbhij,nbhje->nbhie', att, v)
    decay = jnp.exp(b_last[:, :, :, 0, :])

    def step(s, inp):
        qe_c, kd_c, v_c, dec_c = inp
        o = jnp.einsum('bhid,bhde->bhie', qe_c, s)
        s = dec_c[..., None] * s + jnp.einsum('bhjd,bhje->bhde', kd_c, v_c)
        return s, o

    s_fin, o_inter = lax.scan(step, s0.astype(jnp.float32), (qe, kd, v, decay))
    o = (o_intra + o_inter).transpose(1, 0, 3, 2, 4).reshape(B, n * GLA_CHUNK, GLA_HEADS, GLA_DV)[:, :L]
    return o, s_fin


def _nsa_prepare(kc_rows, vc_rows, ks_rows, vs_rows, w_ck, w_cv, g_kc):
    B, L = kc_rows.shape[:2]
    lp = -(-L // SEL_BLOCK) * SEL_BLOCK
    pad = ((0, 0), (0, lp - L), (0, 0), (0, 0))
    n16 = lp // CMP_STRIDE

    def compress(rows, w):
        r = jnp.pad(rows, pad).reshape(B, n16, CMP_STRIDE, NSA_KV_GROUPS, NSA_DH)
        blocks = jnp.concatenate([r[:, :-1], r[:, 1:]], axis=2)
        return jnp.einsum('bnjgd,jde->bnge', blocks, w)

    kc = _rms(compress(kc_rows, w_ck), g_kc)
    vc = compress(vc_rows, w_cv)
    ec = CMP_STRIDE * jnp.arange(n16 - 1) + (CMP_LEN - 1)
    ns = lp // SEL_BLOCK

    def sel_blocks(rows):
        return jnp.pad(rows, pad).reshape(B, ns, SEL_BLOCK, NSA_KV_GROUPS, NSA_DH).transpose(0, 3, 1, 2, 4)

    return kc, vc, ec, sel_blocks(ks_rows), sel_blocks(vs_rows)


def _nsa_global(q, t, prep, slopes):
    kc, vc, ec, ksb, vsb = prep
    B, T = q.shape[:2]
    ns = ksb.shape[2]
    scale = NSA_DH ** -0.5
    qf = q.astype(jnp.float32)
    dist = t[:, None] - ec[None, :]
    s = (jnp.einsum('btgrd,bngd->btgrn', qf, kc.astype(jnp.float32)) * scale
         - slopes[:, :, None] * dist[:, None, None, :].astype(jnp.float32))
    p = _masked_softmax(s, (dist >= 0)[:, None, None, :])
    o_c = jnp.einsum('btgrn,bngd->btgrd', p, vc.astype(jnp.float32))
    imp = jnp.pad(p.sum(axis=3), ((0, 0), (0, 0), (0, 0), (1, 1)))
    quad = imp[..., :SEL_PER * ns].reshape(B, T, NSA_KV_GROUPS, ns, SEL_PER)
    p_slc = quad @ jnp.array([1.0, 2.0, 2.0, 2.0], jnp.float32) + imp[..., SEL_PER::SEL_PER]
    cur = t // SEL_BLOCK
    j = jnp.arange(ns)
    allowed = j[None, :] <= cur[:, None]
    forced = (j[None, :] == 0) | (j[None, :] == cur[:, None]) | (j[None, :] == cur[:, None] - 1)
    score = jnp.where(forced[None, :, None, :], BIG, jnp.where(allowed[None, :, None, :], p_slc, -BIG))
    n_sel = min(SEL_TOP_N, ns)
    _, idx = lax.top_k(score, n_sel)
    idx_g = idx.transpose(0, 2, 1, 3).reshape(B, NSA_KV_GROUPS, T * n_sel)
    bi = jnp.arange(B)[:, None, None]
    gi = jnp.arange(NSA_KV_GROUPS)[None, :, None]
    kg = ksb[bi, gi, idx_g].reshape(B, NSA_KV_GROUPS, T, n_sel * SEL_BLOCK, NSA_DH)
    vg = vsb[bi, gi, idx_g].reshape(B, NSA_KV_GROUPS, T, n_sel * SEL_BLOCK, NSA_DH)
    kpos = (idx[..., None] * SEL_BLOCK + jnp.arange(SEL_BLOCK)).reshape(B, T, NSA_KV_GROUPS, n_sel * SEL_BLOCK)
    d2 = t[None, :, None, None] - kpos
    s2 = (jnp.einsum('btgrd,bgtkd->btgrk', qf, kg.astype(jnp.float32)) * scale
          - slopes[None, None, :, :, None] * d2[:, :, :, None, :].astype(jnp.float32))
    p2 = _masked_softmax(s2, (d2 >= 0)[:, :, :, None, :])
    o_s = jnp.einsum('btgrk,bgtkd->btgrd', p2, vg.astype(jnp.float32))
    return o_c, o_s


def _nsa_window(q, t, kw, vw, kpos, slopes):
    d = t[:, None] - kpos[None, :]
    ok = (d >= 0) & (d <= WINDOW) & (kpos >= 0)[None, :]
    s = (jnp.einsum('btgrd,bkgd->btgrk', q.astype(jnp.float32), kw.astype(jnp.float32)) * NSA_DH ** -0.5
         - slopes[:, :, None] * d[:, None, None, :].astype(jnp.float32))
    p = _masked_softmax(s, ok[:, None, None, :])
    return jnp.einsum('btgrk,bkgd->btgrd', p, vw.astype(jnp.float32))


def _nsa_prompt(q, kc, vc, ks, vs, kw, vw, p, slopes):
    B, L = q.shape[:2]
    prep = _nsa_prepare(kc, vc, ks, vs, p['w_ck'], p['w_cv'], p['kc_norm'])
    nb = L // Q_BLOCK
    wpad = ((0, 0), (WINDOW, 0), (0, 0), (0, 0))
    kw_pad, vw_pad = jnp.pad(kw, wpad), jnp.pad(vw, wpad)
    qb = q.reshape(B, nb, Q_BLOCK, NSA_KV_GROUPS, NSA_REP, NSA_DH).transpose(1, 0, 2, 3, 4, 5)
    tb = jnp.arange(L).reshape(nb, Q_BLOCK)

    def one_block(args):
        qi, ti, i = args
        o_c, o_s = _nsa_global(qi, ti, prep, slopes)
        start = i * Q_BLOCK
        kwi = lax.dynamic_slice_in_dim(kw_pad, start, Q_BLOCK + WINDOW, axis=1)
        vwi = lax.dynamic_slice_in_dim(vw_pad, start, Q_BLOCK + WINDOW, axis=1)
        kpos = start - WINDOW + jnp.arange(Q_BLOCK + WINDOW)
        o_w = _nsa_window(qi, ti, kwi, vwi, kpos, slopes)
        return o_c, o_s, o_w

    o_c, o_s, o_w = lax.map(one_block, (qb, tb, jnp.arange(nb)))

    def unblock(o):
        return o.transpose(1, 0, 2, 3, 4, 5).reshape(B, L, NSA_KV_GROUPS, NSA_REP, NSA_DH)

    kv_new = jnp.stack([kc, vc, ks, vs], axis=2)
    wl = min(WINDOW, L)
    win_new = jnp.stack([kw[:, L - wl:], vw[:, L - wl:]], axis=2)
    return unblock(o_c), unblock(o_s), unblock(o_w), kv_new, win_new


def _nsa_sample(q, kc, vc, ks, vs, kw, vw, p, slopes, cache_kv, layer, cache_win_l, page_table):
    B, T = q.shape[:2]
    n_pages = page_table.shape[1]
    past_len = n_pages * PAGE_SIZE
    past = cache_kv[layer, page_table].reshape(B, past_len, 4, NSA_KV_GROUPS, NSA_DH)
    kv_new = jnp.stack([kc, vc, ks, vs], axis=2)
    rows = jnp.concatenate([past.astype(kv_new.dtype), kv_new], axis=1)
    prep = _nsa_prepare(rows[:, :, 0], rows[:, :, 1], rows[:, :, 2], rows[:, :, 3],
                        p['w_ck'], p['w_cv'], p['kc_norm'])
    t = past_len + jnp.arange(T)
    o_c, o_s = _nsa_global(q, t, prep, slopes)
    wb = cache_win_l.shape[1]
    win_all = jnp.concatenate([cache_win_l.astype(kw.dtype), jnp.stack([kw, vw], axis=2)], axis=1)
    kpos = past_len - wb + jnp.arange(wb + T)
    o_w = _nsa_window(q, t, win_all[:, :, 0], win_all[:, :, 1], kpos, slopes)
    return o_c, o_s, o_w, kv_new, win_all[:, -wb:]


def _layer(x, c, p, gla_s0, conv_prev, nsa_fn):
    B, T, _ = x.shape
    mod = jnp.einsum('bd,de->be', jax.nn.silu(c), p['w_ada']) + p['b_ada']
    sh1, sc1, g1, sh2, sc2, g2 = jnp.split(mod[:, None, :], 6, axis=-1)
    h = _rms(x, p['norm1']) * (1.0 + sc1) + sh1
    z = h @ p['w_in']
    offs = [int(o) for o in np.cumsum(IN_SIZES)[:-1]]
    gq, gk, gv, ga, gr, nq, kc, vc, ks, vs, kw, vw, ng = jnp.split(z, offs, axis=-1)
    log_a = jax.nn.log_sigmoid((ga @ p['w_gla_a'] + p['b_gla_a']).astype(jnp.float32)) / GLA_TAU
    o_g, s_new = _gla(gq.reshape(B, T, GLA_HEADS, GLA_DK) * GLA_DK ** -0.5,
                      gk.reshape(B, T, GLA_HEADS, GLA_DK),
                      gv.reshape(B, T, GLA_HEADS, GLA_DV),
                      log_a.reshape(B, T, GLA_HEADS, GLA_DK), gla_s0)
    o_g = _rms(o_g, p['gla_norm'].reshape(GLA_HEADS, GLA_DV)).reshape(B, T, GLA_WIDTH)
    o_g = o_g * jax.nn.silu(gr.astype(jnp.float32))
    def kvh(a):
        return a.reshape(B, T, NSA_KV_GROUPS, NSA_DH)
    qn = _rms(nq.reshape(B, T, NSA_KV_GROUPS, NSA_REP, NSA_DH), p['q_norm'])
    ksn = _rms(kvh(ks), p['ks_norm'])
    kwn = _rms(kvh(kw), p['kw_norm'])
    o_c, o_s, o_w, kv_new, win_new = nsa_fn(qn, kvh(kc), kvh(vc), ksn, kvh(vs), kwn, kvh(vw), p)
    gates = jax.nn.sigmoid(ng.astype(jnp.float32)).reshape(B, T, 3, NSA_KV_GROUPS, NSA_REP, 1)
    o_n = gates[:, :, 0] * o_c + gates[:, :, 1] * o_s + gates[:, :, 2] * o_w
    o_n = _rms(o_n.reshape(B, T, NSA_WIDTH), p['nsa_norm'])
    mix = jnp.concatenate([o_g.astype(x.dtype), o_n.astype(x.dtype)], axis=-1) @ p['w_out']
    x = x + g1 * mix
    h2 = _rms(x, p['norm2']) * (1.0 + sc2) + sh2
    u = h2 @ p['w_up']
    u_ext = jnp.concatenate([conv_prev.astype(u.dtype), u], axis=1)
    conv = sum(u_ext[:, k:k + T] * p['conv_w'][k] for k in range(CONV_W)) + p['conv_b']
    y = (jax.nn.gelu(conv) * (h2 @ p['w_gate'])) @ p['w_down']
    x = x + g2 * y
    return x, s_new, kv_new, win_new, u_ext[:, -(CONV_W - 1):]


def setup_inputs(seed: int = 0) -> dict:
    key = jax.random.key(seed)
    k = jax.random.split(key, 32)
    D = D_MODEL
    n_pages = PAST_LEN // PAGE_SIZE
    n_used = DEC_BATCH * n_pages
    n_pool = n_used + max(1, n_used // 4)
    wb = min(WINDOW, PAST_LEN)

    def nrm(i, shape, s=1.0):
        return jax.random.normal(k[i], shape, jnp.float32) * s

    def gain(i, shape):
        return 1.0 + nrm(i, shape, 0.02)

    page_table = jax.random.permutation(k[0], n_pool)[:n_used].reshape(DEC_BATCH, n_pages).astype(jnp.int32)
    return {
        'x_prompt': nrm(1, (BATCH, SEQ, D)),
        'x_sample': nrm(2, (DEC_BATCH, DEC_SEQ, D)),
        'cache_kv': nrm(3, (DEPTH, n_pool, PAGE_SIZE, 4, NSA_KV_GROUPS, NSA_DH)),
        'cache_win': nrm(4, (DEPTH, DEC_BATCH, wb, 2, NSA_KV_GROUPS, NSA_DH)),
        'state_gla': nrm(5, (DEPTH, DEC_BATCH, GLA_HEADS, GLA_DK, GLA_DV)),
        'state_conv': nrm(6, (DEPTH, DEC_BATCH, CONV_W - 1, D_FF)),
        'page_table': page_table,
        'c_prompt': nrm(7, (BATCH, D)),
        'c_sample': nrm(8, (DEC_BATCH, D)),
        'norm1': gain(9, (DEPTH, D)),
        'norm2': gain(10, (DEPTH, D)),
        'w_ada': nrm(11, (DEPTH, D, 6 * D), D ** -0.5),
        'b_ada': nrm(12, (DEPTH, 6 * D), 0.01),
        'w_in': nrm(13, (DEPTH, D, D_IN), D ** -0.5),
        'w_gla_a': nrm(14, (DEPTH, GLA_GATE_RANK, GLA_HEADS * GLA_DK), GLA_GATE_RANK ** -0.5),
        'b_gla_a': nrm(15, (DEPTH, GLA_HEADS * GLA_DK), 0.1),
        'gla_norm': gain(16, (DEPTH, GLA_WIDTH)),
        'q_norm': gain(17, (DEPTH, NSA_DH)),
        'kc_norm': gain(18, (DEPTH, NSA_DH)),
        'ks_norm': gain(19, (DEPTH, NSA_DH)),
        'kw_norm': gain(20, (DEPTH, NSA_DH)),
        'w_ck': nrm(21, (DEPTH, CMP_LEN, NSA_DH, NSA_DH), (CMP_LEN * NSA_DH) ** -0.5),
        'w_cv': nrm(22, (DEPTH, CMP_LEN, NSA_DH, NSA_DH), (CMP_LEN * NSA_DH) ** -0.5),
        'nsa_norm': gain(23, (DEPTH, NSA_WIDTH)),
        'w_out': nrm(24, (DEPTH, D, D), D ** -0.5),
        'w_up': nrm(25, (DEPTH, D, D_FF), D ** -0.5),
        'w_gate': nrm(26, (DEPTH, D, D_FF), D ** -0.5),
        'conv_w': nrm(27, (DEPTH, CONV_W, D_FF), CONV_W ** -0.5),
        'conv_b': nrm(28, (DEPTH, D_FF), 0.01),
        'w_down': nrm(29, (DEPTH, D_FF, D), D_FF ** -0.5),
    }


def reference(x_prompt, x_sample, cache_kv, cache_win, state_gla, state_conv, page_table, c_prompt, c_sample,
              norm1, norm2, w_ada, b_ada, w_in, w_gla_a, b_gla_a, gla_norm, q_norm, kc_norm, ks_norm, kw_norm,
              w_ck, w_cv, nsa_norm, w_out, w_up, w_gate, conv_w, conv_b, w_down):
    slopes = _alibi_slopes()
    xp, xs = x_prompt, x_sample
    bp = x_prompt.shape[0]
    kv_p, kv_s, win_p, win_s, gla_p, gla_s, conv_p, conv_s = [], [], [], [], [], [], [], []
    prompt_fn = functools.partial(_nsa_prompt, slopes=slopes)
    for l in range(DEPTH):
        p = {'norm1': norm1[l], 'norm2': norm2[l], 'w_ada': w_ada[l], 'b_ada': b_ada[l], 'w_in': w_in[l],
             'w_gla_a': w_gla_a[l], 'b_gla_a': b_gla_a[l], 'gla_norm': gla_norm[l], 'q_norm': q_norm[l],
             'kc_norm': kc_norm[l], 'ks_norm': ks_norm[l], 'kw_norm': kw_norm[l], 'w_ck': w_ck[l],
             'w_cv': w_cv[l], 'nsa_norm': nsa_norm[l], 'w_out': w_out[l], 'w_up': w_up[l],
             'w_gate': w_gate[l], 'conv_w': conv_w[l], 'conv_b': conv_b[l], 'w_down': w_down[l]}
        s0 = jnp.zeros((bp, GLA_HEADS, GLA_DK, GLA_DV), jnp.float32)
        cz = jnp.zeros((bp, CONV_W - 1, D_FF), xp.dtype)
        xp, sg, kvn, wn, cv = _layer(xp, c_prompt, p, s0, cz, prompt_fn)
        kv_p.append(kvn); win_p.append(wn); gla_p.append(sg.astype(state_gla.dtype)); conv_p.append(cv)
        sample_fn = functools.partial(_nsa_sample, slopes=slopes, cache_kv=cache_kv, layer=l,
                                      cache_win_l=cache_win[l], page_table=page_table)
        xs, sg, kvn, wn, cv = _layer(xs, c_sample, p, state_gla[l], state_conv[l], sample_fn)
        kv_s.append(kvn); win_s.append(wn); gla_s.append(sg.astype(state_gla.dtype)); conv_s.append(cv)
    return (xp, xs, jnp.stack(kv_p), jnp.stack(kv_s), jnp.stack(win_p), jnp.stack(win_s),
            jnp.stack(gla_p), jnp.stack(gla_s), jnp.stack(conv_p), jnp.stack(conv_s))
```

```python
import functools

import numpy as np
import jax
import jax.numpy as jnp
from jax import lax
from jax.experimental import pallas as pl
from jax.experimental.pallas import tpu as pltpu

F32 = jnp.float32
BF16 = jnp.bfloat16

D_MODEL = 1024
GLA_HEADS = 4
GLA_DK = 64
GLA_DV = 128
GLA_GATE_RANK = 16
GLA_TAU = 16.0
GLA_CHUNK = 16
GLA_WIDTH = GLA_HEADS * GLA_DV
NSA_HEADS = 8
NSA_G = 2
NSA_REP = NSA_HEADS // NSA_G
NSA_DH = 64
NSA_WIDTH = NSA_HEADS * NSA_DH
KV_WIDTH = NSA_G * NSA_DH
CMP_STRIDE = 16
CMP_LEN = 2 * CMP_STRIDE
SEL_BLOCK = 64
SEL_PER = SEL_BLOCK // CMP_STRIDE
SEL_TOP_N = 16
WINDOW = 512
D_FF = 2816
CONV_W = 3
RMS_EPS = 1e-6
BIG = 1e30
PAGE_SIZE = 128

IN_SIZES = (GLA_HEADS * GLA_DK, GLA_HEADS * GLA_DK, GLA_WIDTH, GLA_GATE_RANK, GLA_WIDTH,
            NSA_WIDTH, KV_WIDTH, KV_WIDTH, KV_WIDTH, KV_WIDTH, KV_WIDTH, KV_WIDTH, 3 * NSA_HEADS)

C_GQ, C_GK, C_GV, C_GR, C_NQ, C_KV, C_WIN, C_MISC = 0, 256, 512, 1024, 1536, 2048, 2560, 2816
MISC_GA, MISC_NG = 0, GLA_GATE_RANK
N_Z = 3072

VMEM_LIMIT_BYTES = 56 * 1024 * 1024
SLOPES = tuple(2.0 ** (-8.0 * h / NSA_HEADS) for h in range(1, NSA_HEADS + 1))


def _cparams(*sem):
    return pltpu.CompilerParams(dimension_semantics=sem, vmem_limit_bytes=VMEM_LIMIT_BYTES)


def _dot(a, b):
    return jnp.dot(a, b, preferred_element_type=F32)


def _dot_nt(a, b):
    return lax.dot_general(a, b, (((1,), (1,)), ((), ())), preferred_element_type=F32)


def _dot_tn(a, b):
    return lax.dot_general(a, b, (((0,), (0,)), ((), ())), preferred_element_type=F32)


def _rms_rows(x, gain):
    return x * lax.rsqrt(jnp.mean(x * x, axis=-1, keepdims=True) + RMS_EPS) * gain


def _seg_rms64(x, gain):
    rows, width = x.shape
    lo_lane = (lax.broadcasted_iota(jnp.int32, (1, 128), 1) < 64)
    outs = []
    for c in range(width // 128):
        xb = x[:, c * 128:(c + 1) * 128]
        sq = xb * xb
        lo = jnp.sum(jnp.where(lo_lane, sq, 0.0), axis=-1, keepdims=True)
        hi = jnp.sum(jnp.where(lo_lane, 0.0, sq), axis=-1, keepdims=True)
        ms = jnp.where(lo_lane, lo, hi) * (1.0 / 64.0)
        outs.append(xb * lax.rsqrt(ms + RMS_EPS))
    y = outs[0] if len(outs) == 1 else jnp.concatenate(outs, axis=-1)
    return y * gain


def _mod_body(c_ref, w_ref, b_ref, o_ref):
    c = c_ref[...]
    a = c * jax.nn.sigmoid(c)
    o_ref[...] = _dot(a.astype(BF16), w_ref[...].astype(BF16)) + b_ref[...]


def _mod_call(c_all, w_ada, b_ada):
    depth, d, n = w_ada.shape
    bc = c_all.shape[0]
    tn = 1536
    return pl.pallas_call(
        _mod_body,
        grid=(depth, n // tn),
        in_specs=[pl.BlockSpec((bc, d), lambda l, j: (0, 0)),
                  pl.BlockSpec((None, d, tn), lambda l, j: (l, 0, j)),
                  pl.BlockSpec((None, 1, tn), lambda l, j: (l, 0, j))],
        out_specs=pl.BlockSpec((None, bc, tn), lambda l, j: (l, 0, j)),
        out_shape=jax.ShapeDtypeStruct((depth, bc, n), F32),
        compiler_params=_cparams("parallel", "parallel"),
        name="adaln_mod",
    )(c_all, w_ada, b_ada.reshape(depth, 1, n))


def _inproj_body(x_ref, g_ref, sc_ref, sh_ref, w_ref, z_ref, h_scr):
    @pl.when(pl.program_id(1) == 0)
    def _():
        y = _rms_rows(x_ref[...], g_ref[...])
        h_scr[...] = (y * (1.0 + sc_ref[...]) + sh_ref[...]).astype(BF16)

    z_ref[...] = _dot(h_scr[...], w_ref[...])


def _mod_spec(mod, tm, rows_per_batch):
    if mod.ndim == 3:
        return pl.BlockSpec((None, 1, D_MODEL), lambda i, *_: ((i * tm) // rows_per_batch, 0, 0))
    return pl.BlockSpec((tm, D_MODEL), lambda i, *_: (i, 0))


def _inproj_call(x2d, gain, sc, sh, w_bf, tm, rows_per_batch):
    m = x2d.shape[0]
    tn = 512
    return pl.pallas_call(
        _inproj_body,
        grid=(m // tm, N_Z // tn),
        in_specs=[pl.BlockSpec((tm, D_MODEL), lambda i, j: (i, 0)),
                  pl.BlockSpec((1, D_MODEL), lambda i, j: (0, 0)),
                  _mod_spec(sc, tm, rows_per_batch),
                  _mod_spec(sh, tm, rows_per_batch),
                  pl.BlockSpec((D_MODEL, tn), lambda i, j: (0, j))],
        out_specs=pl.BlockSpec((tm, tn), lambda i, j: (i, j)),
        out_shape=jax.ShapeDtypeStruct((m, N_Z), F32),
        scratch_shapes=[pltpu.VMEM((tm, D_MODEL), BF16)],
        compiler_params=_cparams("parallel", "arbitrary"),
        name="inproj",
    )(x2d, gain, sc, sh, w_bf)


def _outproj_body(og_ref, on_ref, x_ref, g1_ref, w_ref, n2_ref, sc_ref, sh_ref, xo_ref, h2_ref):
    mix = (_dot(og_ref[...].astype(BF16), w_ref[:GLA_WIDTH, :])
           + _dot(on_ref[...].astype(BF16), w_ref[GLA_WIDTH:, :]))
    xn = x_ref[...] + g1_ref[...] * mix
    xo_ref[...] = xn
    y = _rms_rows(xn, n2_ref[...])
    h2_ref[...] = (y * (1.0 + sc_ref[...]) + sh_ref[...]).astype(BF16)


def _outproj_call(o_g, o_n, x2d, g1, w_bf, gain2, sc2, sh2, tm, rows_per_batch):
    m = x2d.shape[0]
    return pl.pallas_call(
        _outproj_body,
        grid=(m // tm,),
        in_specs=[pl.BlockSpec((tm, GLA_WIDTH), lambda i: (i, 0)),
                  pl.BlockSpec((tm, NSA_WIDTH), lambda i: (i, 0)),
                  pl.BlockSpec((tm, D_MODEL), lambda i: (i, 0)),
                  _mod_spec(g1, tm, rows_per_batch),
                  pl.BlockSpec((D_MODEL, D_MODEL), lambda i: (0, 0)),
                  pl.BlockSpec((1, D_MODEL), lambda i: (0, 0)),
                  _mod_spec(sc2, tm, rows_per_batch),
                  _mod_spec(sh2, tm, rows_per_batch)],
        out_specs=[pl.BlockSpec((tm, D_MODEL), lambda i: (i, 0)),
                   pl.BlockSpec((tm, D_MODEL), lambda i: (i, 0))],
        out_shape=[jax.ShapeDtypeStruct((m, D_MODEL), F32),
                   jax.ShapeDtypeStruct((m, D_MODEL), BF16)],
        compiler_params=_cparams("parallel"),
        name="outproj",
    )(o_g, o_n, x2d, g1, w_bf, gain2, sc2, sh2)


def _gelu_tanh(x):
    return x * (0.5 * (1.0 + jnp.tanh(0.7978845608028654 * (x + 0.044715 * (x * x * x)))))


def _ffn_body(*refs, tm, period, per_row_prev, keep):
    if per_row_prev:
        (h2_ref, x_ref, g2_ref, wu_ref, wg_ref, wd_ref, cw_ref, cb_ref, p1_ref, p2_ref,
         xo_ref, uk_ref, acc_scr) = refs
    else:
        (h2_ref, x_ref, g2_ref, wu_ref, wg_ref, wd_ref, cw_ref, cb_ref,
         xo_ref, uk_ref, acc_scr, carry_scr) = refs
    i, f = pl.program_id(0), pl.program_id(1)
    h2 = h2_ref[...]
    u = _dot(h2, wu_ref[...])
    gt = _dot(h2, wg_ref[...])
    row = lax.broadcasted_iota(jnp.int32, (tm, 1), 0)
    if per_row_prev:
        t = row % period
        m1, m2 = t >= 1, t >= 2
        prev1, prev2 = p1_ref[...], p2_ref[...]
    else:
        fresh = ((i * tm) % period) == 0
        c = jnp.where(fresh, 0.0, carry_scr[f])
        m1, m2 = row >= 1, row >= 2
        prev1 = c[7:8, :]
        prev2 = jnp.where(row == 0, c[6:7, :], c[7:8, :])
        carry_scr[f] = u[tm - 8:, :]
    u_m1 = jnp.where(m1, pltpu.roll(u, 1, 0), prev1)
    u_m2 = jnp.where(m2, pltpu.roll(u, 2, 0), prev2)
    conv = u_m2 * cw_ref[0:1, :] + u_m1 * cw_ref[1:2, :] + u * cw_ref[2:3, :] + cb_ref[...]
    act = _gelu_tanh(conv) * gt
    part = _dot(act.astype(BF16), wd_ref[...])
    uk_ref[...] = u[tm - keep:, :]

    @pl.when(f == 0)
    def _():
        acc_scr[...] = part

    @pl.when(f > 0)
    def _():
        acc_scr[...] += part

    @pl.when(f == pl.num_programs(1) - 1)
    def _():
        xo_ref[...] = x_ref[...] + g2_ref[...] * acc_scr[...]


def _ffn_call(h2, x2d, g2, wu, wg, wd, cw, cb, tm, period, prev=None):
    m = x2d.shape[0]
    tf = 256
    nf = D_FF // tf
    per_row_prev = prev is not None
    keep = tm if per_row_prev else 8
    body = functools.partial(_ffn_body, tm=tm, period=period, per_row_prev=per_row_prev, keep=keep)
    in_specs = [pl.BlockSpec((tm, D_MODEL), lambda i, f: (i, 0)),
                pl.BlockSpec((tm, D_MODEL), lambda i, f: (i, 0)),
                _mod_spec(g2, tm, period),
                pl.BlockSpec((D_MODEL, tf), lambda i, f: (0, f)),
                pl.BlockSpec((D_MODEL, tf), lambda i, f: (0, f)),
                pl.BlockSpec((tf, D_MODEL), lambda i, f: (f, 0)),
                pl.BlockSpec((CONV_W, tf), lambda i, f: (0, f)),
                pl.BlockSpec((1, tf), lambda i, f: (0, f))]
    args = [h2, x2d, g2, wu, wg, wd, cw, cb]
    scratch = [pltpu.VMEM((tm, D_MODEL), F32)]
    if per_row_prev:
        in_specs += [pl.BlockSpec((tm, tf), lambda i, f: (i, f))] * 2
        args += list(prev)
    else:
        scratch.append(pltpu.VMEM((nf, 8, tf), F32))
    return pl.pallas_call(
        body,
        grid=(m // tm, nf),
        in_specs=in_specs,
        out_specs=[pl.BlockSpec((tm, D_MODEL), lambda i, f: (i, 0)),
                   pl.BlockSpec((keep, tf), lambda i, f: (i, f))],
        out_shape=[jax.ShapeDtypeStruct((m, D_MODEL), F32),
                   jax.ShapeDtypeStruct((m // tm * keep, D_FF), F32)],
        scratch_shapes=scratch,
        compiler_params=_cparams("arbitrary", "arbitrary"),
        name="convffn",
    )(*args)


GLA_ROWS = 128


def _log_sigmoid(x):
    return jnp.minimum(x, 0.0) - jnp.log1p(jnp.exp(-jnp.abs(x)))


def _gla_body(gq_ref, gk_ref, gv_ref, gr_ref, misc_ref, wa_ref, ba_ref, gn_ref, s0_ref,
              o_ref, sT_out_ref, sT_scr, *, rows_in, t_valid):
    R, C = GLA_ROWS, GLA_CHUNK
    nchunk = R // C
    step = pl.program_id(1)

    @pl.when(step == 0)
    def _():
        for h in range(GLA_HEADS):
            sT_scr[h] = s0_ref[h].T

    def padded(ref):
        x = ref[...]
        if rows_in < R:
            x = jnp.concatenate([x, jnp.zeros((R - rows_in, x.shape[1]), x.dtype)], axis=0)
        return x

    row = lax.broadcasted_iota(jnp.int32, (R, 1), 0)
    live = row < t_valid
    q = padded(gq_ref) * (GLA_DK ** -0.5)
    k = jnp.where(live, padded(gk_ref), 0.0)
    v = jnp.where(live, padded(gv_ref), 0.0)
    xa = _dot(padded(misc_ref).astype(BF16), wa_ref[...]) + ba_ref[...]
    la = jnp.where(live, _log_sigmoid(xa) * (1.0 / GLA_TAU), 0.0)

    rc = row % C
    b = la
    shift = 1
    while shift < C:
        b = b + jnp.where(rc >= shift, pltpu.roll(b, shift, 0), 0.0)
        shift *= 2
    b3 = b.reshape(nchunk, C, GLA_HEADS * GLA_DK)
    b_last = jnp.broadcast_to(b3[:, C - 1:C, :], b3.shape).reshape(R, GLA_HEADS * GLA_DK)
    qe = (q * jnp.exp(b)).astype(BF16)
    ke = (k * jnp.exp(-b)).astype(BF16)
    kd = (k * jnp.exp(b_last - b)).astype(BF16)
    decay = jnp.exp(b_last)

    ri = lax.broadcasted_iota(jnp.int32, (R, R), 0)
    ci = lax.broadcasted_iota(jnp.int32, (R, R), 1)
    causal = (ri // C == ci // C) & (ci <= ri)
    col_chunk = lax.broadcasted_iota(jnp.int32, (GLA_DV, R), 1) // C

    outs = []
    for h in range(GLA_HEADS):
        ks = slice(h * GLA_DK, (h + 1) * GLA_DK)
        vs = slice(h * GLA_DV, (h + 1) * GLA_DV)
        v_h = v[:, vs]
        att = jnp.where(causal, _dot_nt(qe[:, ks], ke[:, ks]), 0.0)
        o_h = _dot(att.astype(BF16), v_h.astype(BF16))
        vT = v_h.T
        stack = jnp.concatenate([jnp.where(col_chunk == n, vT, 0.0) for n in range(nchunk)], axis=0)
        incT = _dot(stack.astype(BF16), kd[:, ks])
        sT = sT_scr[h]
        inter = []
        for n in range(nchunk):
            inter.append(_dot_nt(qe[n * C:(n + 1) * C, ks], sT.astype(BF16)))
            sT = decay[n * C + C - 1:n * C + C, ks] * sT + incT[n * GLA_DV:(n + 1) * GLA_DV, :]
        sT_scr[h] = sT
        o_h = o_h + jnp.concatenate(inter, axis=0)
        outs.append(_rms_rows(o_h, gn_ref[:, vs]))
    o = jnp.concatenate(outs, axis=-1)
    gr = padded(gr_ref)
    o = o * (gr * jax.nn.sigmoid(gr))
    o_ref[...] = o[:rows_in, :]

    @pl.when(step == pl.num_programs(1) - 1)
    def _():
        sT_out_ref[...] = sT_scr[...]


def _gla_call(z, wa_pad_bf, ba, gnorm, s0, n_batch, rows_per_batch, rows_in, t_valid):
    steps = rows_per_batch // rows_in
    body = functools.partial(_gla_body, rows_in=rows_in, t_valid=t_valid)

    def zspec(col, width):
        return pl.BlockSpec((rows_in, width), lambda b, s, c=col // width: (b * steps + s, c))

    return pl.pallas_call(
        body,
        grid=(n_batch, steps),
        in_specs=[zspec(C_GQ, 256), zspec(C_GK, 256), zspec(C_GV, 512), zspec(C_GR, 512),
                  zspec(C_MISC, 128),
                  pl.BlockSpec((128, GLA_HEADS * GLA_DK), lambda b, s: (0, 0)),
                  pl.BlockSpec((1, GLA_HEADS * GLA_DK), lambda b, s: (0, 0)),
                  pl.BlockSpec((1, GLA_WIDTH), lambda b, s: (0, 0)),
                  pl.BlockSpec((None, GLA_HEADS, GLA_DK, GLA_DV), lambda b, s: (b, 0, 0, 0))],
        out_specs=[pl.BlockSpec((rows_in, GLA_WIDTH), lambda b, s: (b * steps + s, 0)),
                   pl.BlockSpec((None, GLA_HEADS, GLA_DV, GLA_DK), lambda b, s: (b, 0, 0, 0))],
        out_shape=[jax.ShapeDtypeStruct((n_batch * rows_per_batch, GLA_WIDTH), F32),
                   jax.ShapeDtypeStruct((n_batch, GLA_HEADS, GLA_DV, GLA_DK), F32)],
        scratch_shapes=[pltpu.VMEM((GLA_HEADS, GLA_DV, GLA_DK), F32)],
        compiler_params=_cparams("parallel", "arbitrary"),
        name="gla",
    )(z, z, z, z, z, wa_pad_bf, ba, gnorm, s0)


def _nsa_prep_body(nq_ref, kv_ref, win_ref, qg_ref, ksg_ref, kwg_ref,
                   q_out, kv_out, win_out, kvs_bf, win_bf):
    q = _seg_rms64(nq_ref[...], qg_ref[...]) * (NSA_DH ** -0.5)
    q_out[...] = q.astype(BF16)
    kv = kv_ref[...]
    ksn = _seg_rms64(kv[:, 2 * KV_WIDTH:3 * KV_WIDTH], ksg_ref[...])
    vs = kv[:, 3 * KV_WIDTH:]
    kv_out[:, :2 * KV_WIDTH] = kv[:, :2 * KV_WIDTH]
    kv_out[:, 2 * KV_WIDTH:3 * KV_WIDTH] = ksn
    kv_out[:, 3 * KV_WIDTH:] = vs
    kvs_bf[:, :KV_WIDTH] = ksn.astype(BF16)
    kvs_bf[:, KV_WIDTH:] = vs.astype(BF16)
    w = win_ref[...]
    kwn = _seg_rms64(w[:, :KV_WIDTH], kwg_ref[...])
    win_out[:, :KV_WIDTH] = kwn
    win_out[:, KV_WIDTH:] = w[:, KV_WIDTH:]
    win_bf[:, :KV_WIDTH] = kwn.astype(BF16)
    win_bf[:, KV_WIDTH:] = w[:, KV_WIDTH:].astype(BF16)


def _nsa_prep_call(z, qg, ksg, kwg, tm):
    m = z.shape[0]

    def zspec(col, width):
        return pl.BlockSpec((tm, width), lambda i, c=col // width: (i, c))

    def ospec(width):
        return pl.BlockSpec((tm, width), lambda i: (i, 0))

    return pl.pallas_call(
        _nsa_prep_body,
        grid=(m // tm,),
        in_specs=[zspec(C_NQ, 512), zspec(C_KV, 512), zspec(C_WIN, 256),
                  pl.BlockSpec((1, 512), lambda i: (0, 0)),
                  pl.BlockSpec((1, 128), lambda i: (0, 0)),
                  pl.BlockSpec((1, 128), lambda i: (0, 0))],
        out_specs=[ospec(512), ospec(512), ospec(256), ospec(256), ospec(256)],
        out_shape=[jax.ShapeDtypeStruct((m, 512), BF16),
                   jax.ShapeDtypeStruct((m, 512), F32),
                   jax.ShapeDtypeStruct((m, 256), F32),
                   jax.ShapeDtypeStruct((m, 256), BF16),
                   jax.ShapeDtypeStruct((m, 256), BF16)],
        compiler_params=_cparams("parallel"),
        name="nsa_prep",
    )(z, z, z, qg, ksg, kwg)


def _compress_rows(load_rows, w_ref, kind, nblk):
    acc = jnp.zeros((nblk, 2 * KV_WIDTH), F32)
    for j in range(CMP_STRIDE):
        acc = acc + _dot(load_rows(j).astype(BF16), w_ref[kind, j])
    row = lax.broadcasted_iota(jnp.int32, (nblk, 1), 0)
    bot_next = pltpu.roll(acc[:, KV_WIDTH:], nblk - 1, 0)
    return jnp.where(row < nblk - 1, acc[:, :KV_WIDTH] + bot_next, 0.0)


def _compress_body(krows_ref, vrows_ref, w_ref, g_ref, kc_out, vc_out, *, nblk):
    def loader(ref):
        return lambda j: ref[pl.ds(j, nblk, stride=CMP_STRIDE), :]

    kc = _compress_rows(loader(krows_ref), w_ref, 0, nblk)
    vc = _compress_rows(loader(vrows_ref), w_ref, 1, nblk)
    kc_out[...] = _seg_rms64(kc, g_ref[...]).astype(BF16)
    vc_out[...] = vc.astype(BF16)


def _compress_call(kv_new3, w_cmp, kcg):
    nb, length, _ = kv_new3.shape
    nblk = length // CMP_STRIDE
    return pl.pallas_call(
        functools.partial(_compress_body, nblk=nblk),
        grid=(nb,),
        in_specs=[pl.BlockSpec((None, length, KV_WIDTH), lambda b: (b, 0, 0)),
                  pl.BlockSpec((None, length, KV_WIDTH), lambda b: (b, 0, 1)),
                  pl.BlockSpec((2, CMP_STRIDE, KV_WIDTH, 2 * KV_WIDTH), lambda b: (0, 0, 0, 0)),
                  pl.BlockSpec((1, KV_WIDTH), lambda b: (0, 0))],
        out_specs=[pl.BlockSpec((None, nblk, KV_WIDTH), lambda b: (b, 0, 0))] * 2,
        out_shape=[jax.ShapeDtypeStruct((nb, nblk, KV_WIDTH), BF16)] * 2,
        compiler_params=_cparams("parallel"),
        name="nsa_compress",
    )(kv_new3, kv_new3, w_cmp, kcg)


def _round_up(a, b):
    return -(-a // b) * b


def _tile_rows(a, n):
    return jnp.concatenate([a] * n, axis=0)


def _softmax_rows(s, valid):
    s = jnp.where(valid, s, -BIG)
    m = jnp.max(s, axis=-1, keepdims=True)
    e = jnp.where(valid, jnp.exp(s - m), 0.0)
    l = jnp.sum(e, axis=-1, keepdims=True)
    return e / jnp.where(l > 0.0, l, 1.0)


def _sel_rows(ns):
    return _round_up(ns, 8)


def _imp_scratch_rows(ns):
    return 8 + SEL_PER * _sel_rows(ns) + 8


def _select_blocks(imp, imp_scr, sc_scr, t_row, ns, n_sel):
    tq, nb = imp.shape
    nsr = _sel_rows(ns)
    rows = _imp_scratch_rows(ns)
    imp_scr[0:8, :] = jnp.zeros((8, tq), F32)
    imp_scr[8 + nb:rows, :] = jnp.zeros((rows - 8 - nb, tq), F32)
    imp_scr[8:8 + nb, :] = imp.T

    def ld(off):
        return imp_scr[pl.ds(8 + off, nsr, stride=SEL_PER), :]

    p_slc = ld(-1) + 2.0 * (ld(0) + ld(1) + ld(2)) + ld(3)
    j = lax.broadcasted_iota(jnp.int32, (nsr, 1), 0)
    cur = t_row // SEL_BLOCK
    forced = (j == 0) | (j == cur) | (j == cur - 1)
    score = jnp.where(forced, BIG, jnp.where(j <= cur, p_slc, -BIG))
    sc_scr[...] = score

    def rank_body(k, cnt):
        sk = sc_scr[pl.ds(k, 1), :]
        beats = (sk > score) | ((sk == score) & (k < j))
        return cnt + jnp.where(beats, 1.0, 0.0)

    cnt = lax.fori_loop(0, ns, rank_body, jnp.zeros((nsr, tq), F32))
    sel_t = jnp.where((cnt < float(n_sel)) & (j < ns), 1.0, 0.0)
    nsp = _round_up(ns, 128)
    if nsp > nsr:
        sel_t = jnp.concatenate([sel_t, jnp.zeros((nsp - nsr, tq), F32)], axis=0)
    return sel_t.T.astype(BF16)


def _expand_sel(sel, k0, tk):
    nsp = sel.shape[1]
    kblk = (k0 + lax.broadcasted_iota(jnp.int32, (nsp, tk), 1)) // SEL_BLOCK
    expand = jnp.where(lax.broadcasted_iota(jnp.int32, (nsp, tk), 0) == kblk, 1.0, 0.0).astype(BF16)
    return _dot(sel, expand)


def _online_update(carry, s, valid, v_t):
    m, l, acc = carry
    s = jnp.where(valid, s, -BIG)
    m_new = jnp.maximum(m, jnp.max(s, axis=-1, keepdims=True))
    alpha = jnp.exp(m - m_new)
    e = jnp.where(valid, jnp.exp(s - m_new), 0.0)
    l = alpha * l + jnp.sum(e, axis=-1, keepdims=True)
    acc = alpha * acc + _dot(e.astype(BF16), v_t)
    return m_new, l, acc


def _online_init(rows):
    return (jnp.full((rows, 1), -BIG, F32), jnp.zeros((rows, 1), F32), jnp.zeros((rows, NSA_DH), F32))


def _online_finish(carry):
    _, l, acc = carry
    return acc / jnp.where(l > 0.0, l, 1.0)


def _nsa_attn_body(q_ref, misc_ref, kc_ref, vc_ref, kvs_ref, win_ref, nn_ref, o_ref,
                   imp_scr, sc_scr, *, length, tq, tk):
    nb = length // CMP_STRIDE
    ns = length // SEL_BLOCK
    nsp = _round_up(ns, 128)
    n_sel = min(SEL_TOP_N, ns)
    wk = min(WINDOW + tq, length)
    t0 = pl.program_id(1) * tq
    t_col = t0 + lax.broadcasted_iota(jnp.int32, (tq, 1), 0)
    t_row = t0 + lax.broadcasted_iota(jnp.int32, (1, tq), 1)
    gates = jax.nn.sigmoid(misc_ref[:, MISC_NG:MISC_NG + 3 * NSA_HEADS])
    heads = []
    for g in range(NSA_G):
        ks = slice(g * NSA_DH, (g + 1) * NSA_DH)
        vs = slice(KV_WIDTH + g * NSA_DH, KV_WIDTH + (g + 1) * NSA_DH)
        q_stack = jnp.concatenate(
            [q_ref[:, (g * NSA_REP + r) * NSA_DH:(g * NSA_REP + r + 1) * NSA_DH] for r in range(NSA_REP)], axis=0)
        slope_rows = jnp.concatenate(
            [jnp.full((tq, 1), SLOPES[g * NSA_REP + r], F32) for r in range(NSA_REP)], axis=0)

        ec = CMP_STRIDE * lax.broadcasted_iota(jnp.int32, (1, nb), 1) + (CMP_LEN - 1)
        dist = t_col - ec
        valid4 = _tile_rows(dist >= 0, NSA_REP)
        s = _dot_nt(q_stack, kc_ref[:, ks]) - slope_rows * _tile_rows(dist.astype(F32), NSA_REP)
        p = _softmax_rows(s, valid4)
        o_c = _dot(p.astype(BF16), vc_ref[:, ks])
        imp = p[0:tq]
        for r in range(1, NSA_REP):
            imp = imp + p[r * tq:(r + 1) * tq]

        sel = _select_blocks(imp, imp_scr, sc_scr, t_row, ns, n_sel)

        def kt_body(kt, carry):
            k0 = pl.multiple_of(kt * tk, tk)
            selx = _expand_sel(sel, k0, tk)
            d = t_col - (k0 + lax.broadcasted_iota(jnp.int32, (1, tk), 1))
            valid = _tile_rows((selx > 0.5) & (d >= 0), NSA_REP)
            s = _dot_nt(q_stack, kvs_ref[pl.ds(k0, tk), ks]) - slope_rows * _tile_rows(d.astype(F32), NSA_REP)
            return _online_update(carry, s, valid, kvs_ref[pl.ds(k0, tk), vs])

        n_kt = (t0 + tq + tk - 1) // tk
        o_s = _online_finish(lax.fori_loop(0, n_kt, kt_body, _online_init(NSA_REP * tq)))

        w0 = pl.multiple_of(jnp.maximum(t0 + tq - wk, 0), 128)
        d = t_col - (w0 + lax.broadcasted_iota(jnp.int32, (1, wk), 1))
        valid = _tile_rows((d >= 0) & (d <= WINDOW), NSA_REP)
        s = _dot_nt(q_stack, win_ref[pl.ds(w0, wk), ks]) - slope_rows * _tile_rows(d.astype(F32), NSA_REP)
        p = _softmax_rows(s, valid)
        o_w = _dot(p.astype(BF16), win_ref[pl.ds(w0, wk), vs])

        for r in range(NSA_REP):
            h = g * NSA_REP + r
            rs = slice(r * tq, (r + 1) * tq)
            heads.append(gates[:, h:h + 1] * o_c[rs]
                         + gates[:, NSA_HEADS + h:NSA_HEADS + h + 1] * o_s[rs]
                         + gates[:, 2 * NSA_HEADS + h:2 * NSA_HEADS + h + 1] * o_w[rs])
    o = jnp.concatenate(heads, axis=-1)
    o_ref[...] = _rms_rows(o, nn_ref[...])


def _nsa_attn_call(q_bf, z, kc_c, vc_c, kvs_bf, win_bf, nn, n_batch, length):
    tq = 128
    tk = 256
    nq = length // tq
    nb = length // CMP_STRIDE
    ns = length // SEL_BLOCK
    body = functools.partial(_nsa_attn_body, length=length, tq=tq, tk=tk)
    return pl.pallas_call(
        body,
        grid=(n_batch, nq),
        in_specs=[pl.BlockSpec((tq, NSA_WIDTH), lambda b, i: (b * nq + i, 0)),
                  pl.BlockSpec((tq, 128), lambda b, i: (b * nq + i, C_MISC // 128)),
                  pl.BlockSpec((None, nb, KV_WIDTH), lambda b, i: (b, 0, 0)),
                  pl.BlockSpec((None, nb, KV_WIDTH), lambda b, i: (b, 0, 0)),
                  pl.BlockSpec((None, length, 2 * KV_WIDTH), lambda b, i: (b, 0, 0)),
                  pl.BlockSpec((None, length, 2 * KV_WIDTH), lambda b, i: (b, 0, 0)),
                  pl.BlockSpec((1, NSA_WIDTH), lambda b, i: (0, 0))],
        out_specs=pl.BlockSpec((tq, NSA_WIDTH), lambda b, i: (b * nq + i, 0)),
        out_shape=jax.ShapeDtypeStruct((n_batch * length, NSA_WIDTH), F32),
        scratch_shapes=[pltpu.VMEM((_imp_scratch_rows(ns), tq), F32), pltpu.VMEM((_sel_rows(ns), tq), F32)],
        compiler_params=_cparams("parallel", "parallel"),
        name="nsa_attn",
    )(q_bf, z, kc_c, vc_c, kvs_bf.reshape(n_batch, length, 2 * KV_WIDTH),
      win_bf.reshape(n_batch, length, 2 * KV_WIDTH), nn)


def _prep_weights(p):
    depth = p['w_in'].shape[0]
    offs = np.concatenate([[0], np.cumsum(IN_SIZES)])
    piece = lambda i: p['w_in'][:, :, int(offs[i]):int(offs[i + 1])]
    order = [0, 1, 2, 4, 5, 6, 7, 8, 9, 10, 11, 3, 12]
    cols = [piece(i) for i in order]
    used = sum(IN_SIZES)
    cols.append(jnp.zeros((depth, D_MODEL, N_Z - used), F32))
    w_in = jnp.concatenate(cols, axis=-1).astype(BF16)
    wa = jnp.zeros((depth, 128, GLA_HEADS * GLA_DK), F32).at[:, :GLA_GATE_RANK, :].set(p['w_gla_a']).astype(BF16)

    def cmp_weights(w):
        eye = jnp.eye(NSA_G, dtype=F32)
        bd = jnp.einsum('gh,ljde->ljgdhe', eye, w).reshape(depth, CMP_LEN, KV_WIDTH, KV_WIDTH)
        return jnp.concatenate([bd[:, :CMP_STRIDE], bd[:, CMP_STRIDE:]], axis=-1)

    w_cmp = jnp.stack([cmp_weights(p['w_ck']), cmp_weights(p['w_cv'])], axis=1).astype(BF16)
    tile = lambda a, n: jnp.tile(a, (1, n))[:, None, :]
    return {
        'w_in': w_in, 'wa': wa, 'ba': p['b_gla_a'][:, None, :], 'w_cmp': w_cmp,
        'norm1': p['norm1'][:, None, :], 'norm2': p['norm2'][:, None, :],
        'gla_norm': p['gla_norm'][:, None, :], 'nsa_norm': p['nsa_norm'][:, None, :],
        'q_norm': tile(p['q_norm'], NSA_HEADS), 'kc_norm': tile(p['kc_norm'], NSA_G),
        'ks_norm': tile(p['ks_norm'], NSA_G), 'kw_norm': tile(p['kw_norm'], NSA_G),
        'w_out': p['w_out'].astype(BF16), 'w_up': p['w_up'].astype(BF16),
        'w_gate': p['w_gate'].astype(BF16), 'w_down': p['w_down'].astype(BF16),
        'conv_w': p['conv_w'], 'conv_b': p['conv_b'][:, None, :],
    }


def _split_mod(mod, shape):
    return [mod[:, k * D_MODEL:(k + 1) * D_MODEL].reshape(shape) for k in range(6)]


def _layer_prompt(x, mod, pw, l):
    nb, length, _ = x.shape
    m = nb * length
    sh1, sc1, g1, sh2, sc2, g2 = _split_mod(mod, (nb, 1, D_MODEL))
    x2 = x.reshape(m, D_MODEL)
    tm = min(512, length)
    z = _inproj_call(x2, pw['norm1'][l], sc1, sh1, pw['w_in'][l], tm, length)
    s0 = jnp.zeros((nb, GLA_HEADS, GLA_DK, GLA_DV), F32)
    o_g, s_t = _gla_call(z, pw['wa'][l], pw['ba'][l], pw['gla_norm'][l], s0, nb, length, GLA_ROWS, length)
    q_bf, kv_new, win_new, kvs_bf, win_bf = _nsa_prep_call(
        z, pw['q_norm'][l], pw['ks_norm'][l], pw['kw_norm'][l], tm)
    kc_c, vc_c = _compress_call(kv_new.reshape(nb, length, 4 * KV_WIDTH), pw['w_cmp'][l], pw['kc_norm'][l])
    o_n = _nsa_attn_call(q_bf, z, kc_c, vc_c, kvs_bf, win_bf, pw['nsa_norm'][l], nb, length)
    x_mid, h2 = _outproj_call(o_g, o_n, x2, g1, pw['w_out'][l], pw['norm2'][l], sc2, sh2, tm, length)
    tm_f = min(1024, length)
    x_out, u_keep = _ffn_call(h2, x_mid, g2, pw['w_up'][l], pw['w_gate'][l], pw['w_down'][l],
                              pw['conv_w'][l], pw['conv_b'][l], tm_f, length)
    conv = u_keep.reshape(nb, length // tm_f, 8, D_FF)[:, -1, 8 - (CONV_W - 1):, :]
    wl = min(WINDOW, length)
    return (x_out.reshape(nb, length, D_MODEL),
            jnp.swapaxes(s_t, -1, -2),
            kv_new.reshape(nb, length, 4, NSA_G, NSA_DH),
            win_new.reshape(nb, length, 2, NSA_G, NSA_DH)[:, length - wl:],
            conv)


SAMPLE_ROWS = 16


def _page_dma(pt_ref, ckv_hbm, buf, sem, layer, batch, slot, col0, n_pages, start):
    splits, width = buf.shape[1], buf.shape[3]

    def body(p, carry):
        for s in range(splits):
            cp = pltpu.make_async_copy(
                ckv_hbm.at[layer, pt_ref[batch, p], :, pl.ds(col0 + s * width, width)],
                buf.at[slot, s, pl.ds(pl.multiple_of(p * PAGE_SIZE, PAGE_SIZE), PAGE_SIZE), :],
                sem.at[slot])
            if start:
                cp.start()
            else:
                cp.wait()
        return carry

    lax.fori_loop(0, n_pages, body, 0)


def _gather_pages(pt_ref, ckv_hbm, buf, sem, layer, col0, n_pages):
    b = pl.program_id(0)
    dma = functools.partial(_page_dma, pt_ref, ckv_hbm, buf, sem, layer)

    @pl.when(b == 0)
    def _():
        dma(0, 0, col0, n_pages, True)

    @pl.when(b + 1 < pl.num_programs(0))
    def _():
        dma(b + 1, (b + 1) % 2, col0, n_pages, True)

    slot = b % 2
    dma(b, slot, col0, n_pages, False)
    return slot


def _q_stack(q_ref, g):
    return jnp.concatenate(
        [q_ref[:, (g * NSA_REP + r) * NSA_DH:(g * NSA_REP + r + 1) * NSA_DH] for r in range(NSA_REP)], axis=0)


def _slope_rows(g, rows):
    return jnp.concatenate([jnp.full((rows, 1), SLOPES[g * NSA_REP + r], F32) for r in range(NSA_REP)], axis=0)


def _s1_body(pt_ref, ckv_hbm, q_ref, w_ref, g_ref, oc_ref, sel_ref, buf, sem, imp_scr, sc_scr,
             *, layer, n_pages, past_len):
    slot = _gather_pages(pt_ref, ckv_hbm, buf, sem, layer, 0, n_pages)
    ts = SAMPLE_ROWS
    nblk = past_len // CMP_STRIDE
    ns = past_len // SEL_BLOCK + 1
    nsp = _round_up(ns, 128)
    n_sel = min(SEL_TOP_N, ns)

    def loader(kind):
        return lambda j: buf[slot, kind, pl.ds(j, nblk, stride=CMP_STRIDE), :]

    kc = _seg_rms64(_compress_rows(loader(0), w_ref, 0, nblk), g_ref[...]).astype(BF16)
    vc = _compress_rows(loader(1), w_ref, 1, nblk).astype(BF16)

    t_col = past_len + lax.broadcasted_iota(jnp.int32, (ts, 1), 0)
    t_row = past_len + lax.broadcasted_iota(jnp.int32, (1, 128), 1)
    ec = CMP_STRIDE * lax.broadcasted_iota(jnp.int32, (1, nblk), 1) + (CMP_LEN - 1)
    dist = t_col - ec
    valid4 = _tile_rows(dist >= 0, NSA_REP)
    distf4 = _tile_rows(dist.astype(F32), NSA_REP)
    heads = []
    for g in range(NSA_G):
        ks = slice(g * NSA_DH, (g + 1) * NSA_DH)
        s = _dot_nt(_q_stack(q_ref, g), kc[:, ks]) - _slope_rows(g, ts) * distf4
        p = _softmax_rows(s, valid4)
        o_c = _dot(p.astype(BF16), vc[:, ks])
        imp = p[0:ts]
        for r in range(1, NSA_REP):
            imp = imp + p[r * ts:(r + 1) * ts]
        imp = jnp.concatenate([imp, jnp.zeros((128 - ts, nblk), F32)], axis=0)
        sel = _select_blocks(imp, imp_scr, sc_scr, t_row, ns, n_sel)
        sel_ref[:, g * nsp:(g + 1) * nsp] = sel[0:ts, :].astype(F32)
        heads += [o_c[r * ts:(r + 1) * ts] for r in range(NSA_REP)]
    oc_ref[...] = jnp.concatenate(heads, axis=-1)


def _s1_call(page_table, ckv, q_bf, w_cmp, kcg, layer, past_len):
    nbatch, n_pages = page_table.shape
    ts = SAMPLE_ROWS
    ns = past_len // SEL_BLOCK + 1
    nsp = _round_up(ns, 128)
    body = functools.partial(_s1_body, layer=layer, n_pages=n_pages, past_len=past_len)
    grid_spec = pltpu.PrefetchScalarGridSpec(
        num_scalar_prefetch=1,
        grid=(nbatch,),
        in_specs=[pl.BlockSpec(memory_space=pl.ANY),
                  pl.BlockSpec((ts, NSA_WIDTH), lambda b, pt: (b, 0)),
                  pl.BlockSpec((2, CMP_STRIDE, KV_WIDTH, 2 * KV_WIDTH), lambda b, pt: (0, 0, 0, 0)),
                  pl.BlockSpec((1, KV_WIDTH), lambda b, pt: (0, 0))],
        out_specs=[pl.BlockSpec((ts, NSA_WIDTH), lambda b, pt: (b, 0)),
                   pl.BlockSpec((ts, NSA_G * nsp), lambda b, pt: (b, 0))],
        scratch_shapes=[pltpu.VMEM((2, 2, past_len, KV_WIDTH), F32),
                        pltpu.SemaphoreType.DMA((2,)),
                        pltpu.VMEM((_imp_scratch_rows(ns), 128), F32),
                        pltpu.VMEM((_sel_rows(ns), 128), F32)])
    return pl.pallas_call(
        body,
        grid_spec=grid_spec,
        out_shape=[jax.ShapeDtypeStruct((nbatch * ts, NSA_WIDTH), F32),
                   jax.ShapeDtypeStruct((nbatch * ts, NSA_G * nsp), F32)],
        compiler_params=_cparams("arbitrary"),
        name="nsa_sample_select",
    )(page_table, ckv, q_bf, w_cmp, kcg)


def _s2_body(pt_ref, ckv_hbm, q_ref, misc_ref, oc_ref, sel_ref, kvn_ref, cwin_ref, wn_ref, nn_ref,
             o_ref, buf, sem, *, layer, n_pages, past_len, wb, tk):
    slot = _gather_pages(pt_ref, ckv_hbm, buf, sem, layer, 2 * KV_WIDTH, n_pages)
    ts = SAMPLE_ROWS
    nsp = sel_ref.shape[1] // NSA_G
    t_col = past_len + lax.broadcasted_iota(jnp.int32, (ts, 1), 0)
    gates = jax.nn.sigmoid(misc_ref[:, MISC_NG:MISC_NG + 3 * NSA_HEADS])
    zpad = jnp.zeros((128 - ts, NSA_DH), BF16)
    d_new = t_col - (past_len + lax.broadcasted_iota(jnp.int32, (1, 128), 1))
    d_win = t_col - (past_len - wb + lax.broadcasted_iota(jnp.int32, (1, wb), 1))
    heads = []
    for g in range(NSA_G):
        ks = slice(g * NSA_DH, (g + 1) * NSA_DH)
        vs = slice(KV_WIDTH + g * NSA_DH, KV_WIDTH + (g + 1) * NSA_DH)
        q_stack = _q_stack(q_ref, g)
        slope_rows = _slope_rows(g, ts)
        sel = sel_ref[:, g * nsp:(g + 1) * nsp].astype(BF16)

        def bias(d):
            return slope_rows * _tile_rows(d.astype(F32), NSA_REP)

        def kt_body(kt, carry):
            k0 = pl.multiple_of(kt * tk, tk)
            selx = _expand_sel(sel, k0, tk)
            d = t_col - (k0 + lax.broadcasted_iota(jnp.int32, (1, tk), 1))
            valid = _tile_rows((selx > 0.5) & (d >= 0), NSA_REP)
            s = _dot_nt(q_stack, buf[slot, 0, pl.ds(k0, tk), ks].astype(BF16)) - bias(d)
            return _online_update(carry, s, valid, buf[slot, 0, pl.ds(k0, tk), vs].astype(BF16))

        carry = lax.fori_loop(0, past_len // tk, kt_body, _online_init(NSA_REP * ts))
        k_n = jnp.concatenate([kvn_ref[:, ks], zpad], axis=0)
        v_n = jnp.concatenate([kvn_ref[:, vs], zpad], axis=0)
        valid = _tile_rows((_expand_sel(sel, past_len, 128) > 0.5) & (d_new >= 0), NSA_REP)
        o_s = _online_finish(_online_update(carry, _dot_nt(q_stack, k_n) - bias(d_new), valid, v_n))

        valid = _tile_rows((d_win >= 0) & (d_win <= WINDOW), NSA_REP)
        carry = _online_update(_online_init(NSA_REP * ts),
                               _dot_nt(q_stack, cwin_ref[:, ks].astype(BF16)) - bias(d_win), valid,
                               cwin_ref[:, vs].astype(BF16))
        k_n = jnp.concatenate([wn_ref[:, ks], zpad], axis=0)
        v_n = jnp.concatenate([wn_ref[:, vs], zpad], axis=0)
        valid = _tile_rows((d_new >= 0) & (d_new <= WINDOW), NSA_REP)
        o_w = _online_finish(_online_update(carry, _dot_nt(q_stack, k_n) - bias(d_new), valid, v_n))

        for r in range(NSA_REP):
            h = g * NSA_REP + r
            rs = slice(r * ts, (r + 1) * ts)
            heads.append(gates[:, h:h + 1] * oc_ref[:, h * NSA_DH:(h + 1) * NSA_DH]
                         + gates[:, NSA_HEADS + h:NSA_HEADS + h + 1] * o_s[rs]
                         + gates[:, 2 * NSA_HEADS + h:2 * NSA_HEADS + h + 1] * o_w[rs])
    o_ref[...] = _rms_rows(jnp.concatenate(heads, axis=-1), nn_ref[...])


def _s2_call(page_table, ckv, q_bf, z, o_c, sel, kvs_bf, cwin, win_bf, nn, layer, past_len):
    nbatch, n_pages = page_table.shape
    ts = SAMPLE_ROWS
    wb = cwin.shape[2]
    tk = min(512, past_len)
    body = functools.partial(_s2_body, layer=layer, n_pages=n_pages, past_len=past_len, wb=wb, tk=tk)

    def rows(width, col_block=0):
        return pl.BlockSpec((ts, width), lambda b, pt, c=col_block: (b, c))

    grid_spec = pltpu.PrefetchScalarGridSpec(
        num_scalar_prefetch=1,
        grid=(nbatch,),
        in_specs=[pl.BlockSpec(memory_space=pl.ANY),
                  rows(NSA_WIDTH), rows(128, C_MISC // 128), rows(NSA_WIDTH), rows(sel.shape[1]),
                  rows(2 * KV_WIDTH),
                  pl.BlockSpec((None, None, wb, 2 * KV_WIDTH), lambda b, pt: (layer, b, 0, 0)),
                  rows(2 * KV_WIDTH),
                  pl.BlockSpec((1, NSA_WIDTH), lambda b, pt: (0, 0))],
        out_specs=rows(NSA_WIDTH),
        scratch_shapes=[pltpu.VMEM((2, 1, past_len, 2 * KV_WIDTH), F32),
                        pltpu.SemaphoreType.DMA((2,))])
    return pl.pallas_call(
        body,
        grid_spec=grid_spec,
        out_shape=jax.ShapeDtypeStruct((nbatch * ts, NSA_WIDTH), F32),
        compiler_params=_cparams("arbitrary"),
        name="nsa_sample_attn",
    )(page_table, ckv, q_bf, z, o_c, sel, kvs_bf, cwin, win_bf, nn)


def _layer_sample(x, mod, pw, l, t_real, ckv, cwin, page_table, s0, conv_prev):
    nb, ts, _ = x.shape
    m = nb * ts
    past_len = page_table.shape[1] * PAGE_SIZE
    sh1, sc1, g1, sh2, sc2, g2 = [jnp.repeat(a, ts, axis=0) for a in _split_mod(mod, (nb, D_MODEL))]
    x2 = x.reshape(m, D_MODEL)
    z = _inproj_call(x2, pw['norm1'][l], sc1, sh1, pw['w_in'][l], m, ts)
    o_g, s_t = _gla_call(z, pw['wa'][l], pw['ba'][l], pw['gla_norm'][l], s0, nb, ts, ts, t_real)
    q_bf, kv_new, win_new, kvs_bf, win_bf = _nsa_prep_call(
        z, pw['q_norm'][l], pw['ks_norm'][l], pw['kw_norm'][l], m)
    o_c, sel = _s1_call(page_table, ckv, q_bf, pw['w_cmp'][l], pw['kc_norm'][l], l, past_len)
    o_n = _s2_call(page_table, ckv, q_bf, z, o_c, sel, kvs_bf, cwin, win_bf, pw['nsa_norm'][l], l, past_len)
    x_mid, h2 = _outproj_call(o_g, o_n, x2, g1, pw['w_out'][l], pw['norm2'][l], sc2, sh2, m, ts)
    zrow = jnp.zeros((nb, ts, D_FF), F32)
    p1 = zrow.at[:, 0].set(conv_prev[:, 1]).reshape(m, D_FF)
    p2 = zrow.at[:, 0].set(conv_prev[:, 0]).at[:, 1].set(conv_prev[:, 1]).reshape(m, D_FF)
    x_out, u = _ffn_call(h2, x_mid, g2, pw['w_up'][l], pw['w_gate'][l], pw['w_down'][l],
                         pw['conv_w'][l], pw['conv_b'][l], m, ts, prev=(p1, p2))
    conv = u.reshape(nb, ts, D_FF)[:, t_real - (CONV_W - 1):t_real]
    return (x_out.reshape(nb, ts, D_MODEL),
            jnp.swapaxes(s_t, -1, -2),
            kv_new.reshape(nb, ts, 4, NSA_G, NSA_DH)[:, :t_real],
            win_new.reshape(nb, ts, 2, NSA_G, NSA_DH)[:, :t_real],
            conv)


def kernel(x_prompt, x_sample, cache_kv, cache_win, state_gla, state_conv, page_table, c_prompt, c_sample,
           norm1, norm2, w_ada, b_ada, w_in, w_gla_a, b_gla_a, gla_norm, q_norm, kc_norm, ks_norm, kw_norm,
           w_ck, w_cv, nsa_norm, w_out, w_up, w_gate, conv_w, conv_b, w_down):
    depth = w_in.shape[0]
    bp, bs, t_s = x_prompt.shape[0], x_sample.shape[0], x_sample.shape[1]
    assert CONV_W - 1 <= t_s <= SAMPLE_ROWS
    pw = _prep_weights({'w_in': w_in, 'w_gla_a': w_gla_a, 'b_gla_a': b_gla_a, 'w_ck': w_ck, 'w_cv': w_cv,
                        'norm1': norm1, 'norm2': norm2, 'gla_norm': gla_norm, 'nsa_norm': nsa_norm,
                        'q_norm': q_norm, 'kc_norm': kc_norm, 'ks_norm': ks_norm, 'kw_norm': kw_norm,
                        'w_out': w_out, 'w_up': w_up, 'w_gate': w_gate, 'w_down': w_down,
                        'conv_w': conv_w, 'conv_b': conv_b})
    bc = _round_up(bp + bs, 8)
    c_all = jnp.concatenate([c_prompt, c_sample, jnp.zeros((bc - bp - bs, D_MODEL), F32)], axis=0)
    mod_all = _mod_call(c_all, w_ada, b_ada)
    n_pool = cache_kv.shape[1]
    wb = cache_win.shape[2]
    ckv = cache_kv.reshape(depth, n_pool, PAGE_SIZE, 4 * KV_WIDTH)
    cwin = cache_win.reshape(depth, bs, wb, 2 * KV_WIDTH)
    xp = x_prompt
    xs = jnp.pad(x_sample, ((0, 0), (0, SAMPLE_ROWS - t_s), (0, 0)))
    outs_p, outs_s = [], []
    for l in range(depth):
        res = _layer_prompt(xp, mod_all[l, :bp], pw, l)
        xp = res[0]
        outs_p.append(res[1:])
        res = _layer_sample(xs, mod_all[l, bp:bp + bs], pw, l, t_s, ckv, cwin, page_table,
                            state_gla[l], state_conv[l])
        xs = res[0]
        outs_s.append(res[1:])
    stack = lambda outs, k: jnp.stack([o[k] for o in outs])
    win_s = jnp.stack([jnp.concatenate([cache_win[l][:, t_s:], outs_s[l][2]], axis=1) for l in range(depth)])
    return (xp, xs[:, :t_s], stack(outs_p, 1), stack(outs_s, 1), stack(outs_p, 2), win_s,
            stack(outs_p, 0), stack(outs_s, 0), stack(outs_p, 3), stack(outs_s, 3))
```

```python
import functools

import numpy as np
import jax
import jax.numpy as jnp
from jax import lax
from jax.experimental import pallas as pl
from jax.experimental.pallas import tpu as pltpu

F32 = jnp.float32
BF16 = jnp.bfloat16

D_MODEL = 1024
GLA_HEADS = 4
GLA_DK = 64
GLA_DV = 128
GLA_GATE_RANK = 16
GLA_TAU = 16.0
GLA_CHUNK = 16
GLA_WIDTH = GLA_HEADS * GLA_DV
NSA_HEADS = 8
NSA_G = 2
NSA_REP = NSA_HEADS // NSA_G
NSA_DH = 64
NSA_WIDTH = NSA_HEADS * NSA_DH
KV_WIDTH = NSA_G * NSA_DH
CMP_STRIDE = 16
CMP_LEN = 2 * CMP_STRIDE
SEL_BLOCK = 64
SEL_PER = SEL_BLOCK // CMP_STRIDE
SEL_TOP_N = 16
WINDOW = 512
D_FF = 2816
CONV_W = 3
RMS_EPS = 1e-6
BIG = 1e30
PAGE_SIZE = 128

IN_SIZES = (GLA_HEADS * GLA_DK, GLA_HEADS * GLA_DK, GLA_WIDTH, GLA_GATE_RANK, GLA_WIDTH,
            NSA_WIDTH, KV_WIDTH, KV_WIDTH, KV_WIDTH, KV_WIDTH, KV_WIDTH, KV_WIDTH, 3 * NSA_HEADS)

C_GQ, C_GK, C_GV, C_GR, C_NQ, C_KV, C_WIN, C_MISC = 0, 256, 512, 1024, 1536, 2048, 2560, 2816
MISC_GA, MISC_NG = 0, GLA_GATE_RANK
N_Z = 3072

VMEM_LIMIT_BYTES = 56 * 1024 * 1024
SLOPES = tuple(2.0 ** (-8.0 * h / NSA_HEADS) for h in range(1, NSA_HEADS + 1))


def _cparams(*sem):
    return pltpu.CompilerParams(dimension_semantics=sem, vmem_limit_bytes=VMEM_LIMIT_BYTES)


def _dot(a, b):
    return jnp.dot(a, b, preferred_element_type=F32)


def _dot_nt(a, b):
    return lax.dot_general(a, b, (((1,), (1,)), ((), ())), preferred_element_type=F32)


def _dot_tn(a, b):
    return lax.dot_general(a, b, (((0,), (0,)), ((), ())), preferred_element_type=F32)


def _rms_rows(x, gain):
    return x * lax.rsqrt(jnp.mean(x * x, axis=-1, keepdims=True) + RMS_EPS) * gain


def _seg_rms64(x, gain):
    rows, width = x.shape
    lo_lane = (lax.broadcasted_iota(jnp.int32, (1, 128), 1) < 64)
    outs = []
    for c in range(width // 128):
        xb = x[:, c * 128:(c + 1) * 128]
        sq = xb * xb
        lo = jnp.sum(jnp.where(lo_lane, sq, 0.0), axis=-1, keepdims=True)
        hi = jnp.sum(jnp.where(lo_lane, 0.0, sq), axis=-1, keepdims=True)
        ms = jnp.where(lo_lane, lo, hi) * (1.0 / 64.0)
        outs.append(xb * lax.rsqrt(ms + RMS_EPS))
    y = outs[0] if len(outs) == 1 else jnp.concatenate(outs, axis=-1)
    return y * gain


def _mod_body(c_ref, w_ref, b_ref, o_ref):
    c = c_ref[...]
    a = c * jax.nn.sigmoid(c)
    o_ref[...] = _dot(a.astype(BF16), w_ref[...].astype(BF16)) + b_ref[...]


def _mod_call(c_all, w_ada, b_ada):
    depth, d, n = w_ada.shape
    bc = c_all.shape[0]
    tn = 1536
    return pl.pallas_call(
        _mod_body,
        grid=(depth, n // tn),
        in_specs=[pl.BlockSpec((bc, d), lambda l, j: (0, 0)),
                  pl.BlockSpec((None, d, tn), lambda l, j: (l, 0, j)),
                  pl.BlockSpec((None, 1, tn), lambda l, j: (l, 0, j))],
        out_specs=pl.BlockSpec((None, bc, tn), lambda l, j: (l, 0, j)),
        out_shape=jax.ShapeDtypeStruct((depth, bc, n), F32),
        compiler_params=_cparams("parallel", "parallel"),
        name="adaln_mod",
    )(c_all, w_ada, b_ada.reshape(depth, 1, n))


def _inproj_body(x_ref, g_ref, sc_ref, sh_ref, w_ref, z_ref, h_scr):
    @pl.when(pl.program_id(1) == 0)
    def _():
        y = _rms_rows(x_ref[...], g_ref[...])
        h_scr[...] = (y * (1.0 + sc_ref[...]) + sh_ref[...]).astype(BF16)

    z_ref[...] = _dot(h_scr[...], w_ref[...])


def _mod_spec(mod, tm, rows_per_batch):
    if mod.ndim == 3:
        return pl.BlockSpec((None, 1, D_MODEL), lambda i, *_: ((i * tm) // rows_per_batch, 0, 0))
    return pl.BlockSpec((tm, D_MODEL), lambda i, *_: (i, 0))


def _inproj_call(x2d, gain, sc, sh, w_bf, tm, rows_per_batch):
    m = x2d.shape[0]
    tn = 1024
    return pl.pallas_call(
        _inproj_body,
        grid=(m // tm, N_Z // tn),
        in_specs=[pl.BlockSpec((tm, D_MODEL), lambda i, j: (i, 0)),
                  pl.BlockSpec((1, D_MODEL), lambda i, j: (0, 0)),
                  _mod_spec(sc, tm, rows_per_batch),
                  _mod_spec(sh, tm, rows_per_batch),
                  pl.BlockSpec((D_MODEL, tn), lambda i, j: (0, j))],
        out_specs=pl.BlockSpec((tm, tn), lambda i, j: (i, j)),
        out_shape=jax.ShapeDtypeStruct((m, N_Z), F32),
        scratch_shapes=[pltpu.VMEM((tm, D_MODEL), BF16)],
        compiler_params=_cparams("parallel", "arbitrary"),
        name="inproj",
    )(x2d, gain, sc, sh, w_bf)


def _outproj_body(og_ref, on_ref, x_ref, g1_ref, w_ref, n2_ref, sc_ref, sh_ref, xo_ref, h2_ref):
    mix = (_dot(og_ref[...].astype(BF16), w_ref[:GLA_WIDTH, :])
           + _dot(on_ref[...].astype(BF16), w_ref[GLA_WIDTH:, :]))
    xn = x_ref[...] + g1_ref[...] * mix
    xo_ref[...] = xn
    y = _rms_rows(xn, n2_ref[...])
    h2_ref[...] = (y * (1.0 + sc_ref[...]) + sh_ref[...]).astype(BF16)


def _outproj_call(o_g, o_n, x2d, g1, w_bf, gain2, sc2, sh2, tm, rows_per_batch):
    m = x2d.shape[0]
    return pl.pallas_call(
        _outproj_body,
        grid=(m // tm,),
        in_specs=[pl.BlockSpec((tm, GLA_WIDTH), lambda i: (i, 0)),
                  pl.BlockSpec((tm, NSA_WIDTH), lambda i: (i, 0)),
                  pl.BlockSpec((tm, D_MODEL), lambda i: (i, 0)),
                  _mod_spec(g1, tm, rows_per_batch),
                  pl.BlockSpec((D_MODEL, D_MODEL), lambda i: (0, 0)),
                  pl.BlockSpec((1, D_MODEL), lambda i: (0, 0)),
                  _mod_spec(sc2, tm, rows_per_batch),
                  _mod_spec(sh2, tm, rows_per_batch)],
        out_specs=[pl.BlockSpec((tm, D_MODEL), lambda i: (i, 0)),
                   pl.BlockSpec((tm, D_MODEL), lambda i: (i, 0))],
        out_shape=[jax.ShapeDtypeStruct((m, D_MODEL), F32),
                   jax.ShapeDtypeStruct((m, D_MODEL), BF16)],
        compiler_params=_cparams("parallel"),
        name="outproj",
    )(o_g, o_n, x2d, g1, w_bf, gain2, sc2, sh2)


def _gelu_tanh(x):
    return x * (0.5 * (1.0 + jnp.tanh(0.7978845608028654 * (x + 0.044715 * (x * x * x)))))


def _ffn_body(*refs, tm, period, per_row_prev, keep):
    if per_row_prev:
        (h2_ref, x_ref, g2_ref, wu_ref, wg_ref, wd_ref, cw_ref, cb_ref, p1_ref, p2_ref,
         xo_ref, uk_ref, acc_scr) = refs
    else:
        (h2_ref, x_ref, g2_ref, wu_ref, wg_ref, wd_ref, cw_ref, cb_ref,
         xo_ref, uk_ref, acc_scr, carry_scr) = refs
    i, f = pl.program_id(0), pl.program_id(1)
    h2 = h2_ref[...]
    u = _dot(h2, wu_ref[...])
    gt = _dot(h2, wg_ref[...])
    row = lax.broadcasted_iota(jnp.int32, (tm, 1), 0)
    if per_row_prev:
        t = row % period
        m1, m2 = t >= 1, t >= 2
        prev1, prev2 = p1_ref[...], p2_ref[...]
    else:
        fresh = ((i * tm) % period) == 0
        c = jnp.where(fresh, 0.0, carry_scr[f])
        m1, m2 = row >= 1, row >= 2
        prev1 = c[7:8, :]
        prev2 = jnp.where(row == 0, c[6:7, :], c[7:8, :])
        carry_scr[f] = u[tm - 8:, :]
    u_m1 = jnp.where(m1, pltpu.roll(u, 1, 0), prev1)
    u_m2 = jnp.where(m2, pltpu.roll(u, 2, 0), prev2)
    conv = u_m2 * cw_ref[0:1, :] + u_m1 * cw_ref[1:2, :] + u * cw_ref[2:3, :] + cb_ref[...]
    act = _gelu_tanh(conv) * gt
    part = _dot(act.astype(BF16), wd_ref[...])
    uk_ref[...] = u[tm - keep:, :]

    @pl.when(f == 0)
    def _():
        acc_scr[...] = part

    @pl.when(f > 0)
    def _():
        acc_scr[...] += part

    @pl.when(f == pl.num_programs(1) - 1)
    def _():
        xo_ref[...] = x_ref[...] + g2_ref[...] * acc_scr[...]


def _ffn_call(h2, x2d, g2, wu, wg, wd, cw, cb, tm, period, prev=None):
    m = x2d.shape[0]
    tf = 256
    nf = D_FF // tf
    per_row_prev = prev is not None
    keep = tm if per_row_prev else 8
    body = functools.partial(_ffn_body, tm=tm, period=period, per_row_prev=per_row_prev, keep=keep)
    in_specs = [pl.BlockSpec((tm, D_MODEL), lambda i, f: (i, 0)),
                pl.BlockSpec((tm, D_MODEL), lambda i, f: (i, 0)),
                _mod_spec(g2, tm, period),
                pl.BlockSpec((D_MODEL, tf), lambda i, f: (0, f)),
                pl.BlockSpec((D_MODEL, tf), lambda i, f: (0, f)),
                pl.BlockSpec((tf, D_MODEL), lambda i, f: (f, 0)),
                pl.BlockSpec((CONV_W, tf), lambda i, f: (0, f)),
                pl.BlockSpec((1, tf), lambda i, f: (0, f))]
    args = [h2, x2d, g2, wu, wg, wd, cw, cb]
    scratch = [pltpu.VMEM((tm, D_MODEL), F32)]
    if per_row_prev:
        in_specs += [pl.BlockSpec((tm, tf), lambda i, f: (i, f))] * 2
        args += list(prev)
    else:
        scratch.append(pltpu.VMEM((nf, 8, tf), F32))
    return pl.pallas_call(
        body,
        grid=(m // tm, nf),
        in_specs=in_specs,
        out_specs=[pl.BlockSpec((tm, D_MODEL), lambda i, f: (i, 0)),
                   pl.BlockSpec((keep, tf), lambda i, f: (i, f))],
        out_shape=[jax.ShapeDtypeStruct((m, D_MODEL), F32),
                   jax.ShapeDtypeStruct((m // tm * keep, D_FF), F32)],
        scratch_shapes=scratch,
        compiler_params=_cparams("arbitrary", "arbitrary"),
        name="convffn",
    )(*args)


GLA_ROWS = 128


def _log_sigmoid(x):
    return jnp.minimum(x, 0.0) - jnp.log1p(jnp.exp(-jnp.abs(x)))


def _gla_body(gq_ref, gk_ref, gv_ref, gr_ref, misc_ref, wa_ref, ba_ref, gn_ref, s0_ref,
              o_ref, sT_out_ref, sT_scr, *, rows_in, t_valid):
    R, C = GLA_ROWS, GLA_CHUNK
    nchunk = R // C
    step = pl.program_id(1)

    @pl.when(step == 0)
    def _():
        for h in range(GLA_HEADS):
            sT_scr[h] = s0_ref[h].T

    def padded(ref):
        x = ref[...]
        if rows_in < R:
            x = jnp.concatenate([x, jnp.zeros((R - rows_in, x.shape[1]), x.dtype)], axis=0)
        return x

    row = lax.broadcasted_iota(jnp.int32, (R, 1), 0)
    live = row < t_valid
    q = padded(gq_ref) * (GLA_DK ** -0.5)
    k = jnp.where(live, padded(gk_ref), 0.0)
    v = jnp.where(live, padded(gv_ref), 0.0)
    xa = _dot(padded(misc_ref).astype(BF16), wa_ref[...]) + ba_ref[...]
    la = jnp.where(live, _log_sigmoid(xa) * (1.0 / GLA_TAU), 0.0)

    rc = row % C
    b = la
    shift = 1
    while shift < C:
        b = b + jnp.where(rc >= shift, pltpu.roll(b, shift, 0), 0.0)
        shift *= 2
    b3 = b.reshape(nchunk, C, GLA_HEADS * GLA_DK)
    b_last = jnp.broadcast_to(b3[:, C - 1:C, :], b3.shape).reshape(R, GLA_HEADS * GLA_DK)
    qe = (q * jnp.exp(b)).astype(BF16)
    ke = (k * jnp.exp(-b)).astype(BF16)
    kd = (k * jnp.exp(b_last - b)).astype(BF16)
    decay = jnp.exp(b_last)

    ri = lax.broadcasted_iota(jnp.int32, (R, R), 0)
    ci = lax.broadcasted_iota(jnp.int32, (R, R), 1)
    causal = (ri // C == ci // C) & (ci <= ri)
    col_chunk = lax.broadcasted_iota(jnp.int32, (GLA_DV, R), 1) // C

    outs = []
    for h in range(GLA_HEADS):
        ks = slice(h * GLA_DK, (h + 1) * GLA_DK)
        vs = slice(h * GLA_DV, (h + 1) * GLA_DV)
        v_h = v[:, vs]
        att = jnp.where(causal, _dot_nt(qe[:, ks], ke[:, ks]), 0.0)
        o_h = _dot(att.astype(BF16), v_h.astype(BF16))
        vT = v_h.T
        stack = jnp.concatenate([jnp.where(col_chunk == n, vT, 0.0) for n in range(nchunk)], axis=0)
        incT = _dot(stack.astype(BF16), kd[:, ks])
        sT = sT_scr[h]
        inter = []
        for n in range(nchunk):
            inter.append(_dot_nt(qe[n * C:(n + 1) * C, ks], sT.astype(BF16)))
            sT = decay[n * C + C - 1:n * C + C, ks] * sT + incT[n * GLA_DV:(n + 1) * GLA_DV, :]
        sT_scr[h] = sT
        o_h = o_h + jnp.concatenate(inter, axis=0)
        outs.append(_rms_rows(o_h, gn_ref[:, vs]))
    o = jnp.concatenate(outs, axis=-1)
    gr = padded(gr_ref)
    o = o * (gr * jax.nn.sigmoid(gr))
    o_ref[...] = o[:rows_in, :]

    @pl.when(step == pl.num_programs(1) - 1)
    def _():
        sT_out_ref[...] = sT_scr[...]


def _gla_call(z, wa_pad_bf, ba, gnorm, s0, n_batch, rows_per_batch, rows_in, t_valid):
    steps = rows_per_batch // rows_in
    body = functools.partial(_gla_body, rows_in=rows_in, t_valid=t_valid)

    def zspec(col, width):
        return pl.BlockSpec((rows_in, width), lambda b, s, c=col // width: (b * steps + s, c))

    return pl.pallas_call(
        body,
        grid=(n_batch, steps),
        in_specs=[zspec(C_GQ, 256), zspec(C_GK, 256), zspec(C_GV, 512), zspec(C_GR, 512),
                  zspec(C_MISC, 128),
                  pl.BlockSpec((128, GLA_HEADS * GLA_DK), lambda b, s: (0, 0)),
                  pl.BlockSpec((1, GLA_HEADS * GLA_DK), lambda b, s: (0, 0)),
                  pl.BlockSpec((1, GLA_WIDTH), lambda b, s: (0, 0)),
                  pl.BlockSpec((None, GLA_HEADS, GLA_DK, GLA_DV), lambda b, s: (b, 0, 0, 0))],
        out_specs=[pl.BlockSpec((rows_in, GLA_WIDTH), lambda b, s: (b * steps + s, 0)),
                   pl.BlockSpec((None, GLA_HEADS, GLA_DV, GLA_DK), lambda b, s: (b, 0, 0, 0))],
        out_shape=[jax.ShapeDtypeStruct((n_batch * rows_per_batch, GLA_WIDTH), F32),
                   jax.ShapeDtypeStruct((n_batch, GLA_HEADS, GLA_DV, GLA_DK), F32)],
        scratch_shapes=[pltpu.VMEM((GLA_HEADS, GLA_DV, GLA_DK), F32)],
        compiler_params=_cparams("parallel", "arbitrary"),
        name="gla",
    )(z, z, z, z, z, wa_pad_bf, ba, gnorm, s0)


def _nsa_prep_body(nq_ref, kv_ref, win_ref, qg_ref, ksg_ref, kwg_ref,
                   q_out, kv_out, win_out, kvs_bf, win_bf):
    q = _seg_rms64(nq_ref[...], qg_ref[...]) * (NSA_DH ** -0.5)
    q_out[...] = q.astype(BF16)
    kv = kv_ref[...]
    ksn = _seg_rms64(kv[:, 2 * KV_WIDTH:3 * KV_WIDTH], ksg_ref[...])
    vs = kv[:, 3 * KV_WIDTH:]
    kv_out[:, :2 * KV_WIDTH] = kv[:, :2 * KV_WIDTH]
    kv_out[:, 2 * KV_WIDTH:3 * KV_WIDTH] = ksn
    kv_out[:, 3 * KV_WIDTH:] = vs
    kvs_bf[:, :KV_WIDTH] = ksn.astype(BF16)
    kvs_bf[:, KV_WIDTH:] = vs.astype(BF16)
    w = win_ref[...]
    kwn = _seg_rms64(w[:, :KV_WIDTH], kwg_ref[...])
    win_out[:, :KV_WIDTH] = kwn
    win_out[:, KV_WIDTH:] = w[:, KV_WIDTH:]
    win_bf[:, :KV_WIDTH] = kwn.astype(BF16)
    win_bf[:, KV_WIDTH:] = w[:, KV_WIDTH:].astype(BF16)


def _nsa_prep_call(z, qg, ksg, kwg, tm):
    m = z.shape[0]

    def zspec(col, width):
        return pl.BlockSpec((tm, width), lambda i, c=col // width: (i, c))

    def ospec(width):
        return pl.BlockSpec((tm, width), lambda i: (i, 0))

    return pl.pallas_call(
        _nsa_prep_body,
        grid=(m // tm,),
        in_specs=[zspec(C_NQ, 512), zspec(C_KV, 512), zspec(C_WIN, 256),
                  pl.BlockSpec((1, 512), lambda i: (0, 0)),
                  pl.BlockSpec((1, 128), lambda i: (0, 0)),
                  pl.BlockSpec((1, 128), lambda i: (0, 0))],
        out_specs=[ospec(512), ospec(512), ospec(256), ospec(256), ospec(256)],
        out_shape=[jax.ShapeDtypeStruct((m, 512), BF16),
                   jax.ShapeDtypeStruct((m, 512), F32),
                   jax.ShapeDtypeStruct((m, 256), F32),
                   jax.ShapeDtypeStruct((m, 256), BF16),
                   jax.ShapeDtypeStruct((m, 256), BF16)],
        compiler_params=_cparams("parallel"),
        name="nsa_prep",
    )(z, z, z, qg, ksg, kwg)


def _pos_lanes(hi, lo, rows):
    lane = lax.broadcasted_iota(jnp.int32, (rows, 128), 1)
    return jnp.where(lane == 0, hi, jnp.where(lane == 1, lo, 0))


def _nsa_prep_t_body(nq_ref, kv_ref, win_ref, misc_ref, qg_ref, ksg_ref, kwg_ref,
                     qt_out, gt_out, kv_out, win_out, ksel_out, vselt_out, kwin_out, vwint_out,
                     *, tm, length):
    q = _seg_rms64(nq_ref[...], qg_ref[...]) * (NSA_DH ** -0.5)
    qt_out[...] = q.T.astype(BF16)
    gt_out[...] = jax.nn.sigmoid(misc_ref[...]).T[MISC_NG:MISC_NG + 32, :]
    kv = kv_ref[...]
    ksn = _seg_rms64(kv[:, 2 * KV_WIDTH:3 * KV_WIDTH], ksg_ref[...])
    vs = kv[:, 3 * KV_WIDTH:]
    kv_out[:, :2 * KV_WIDTH] = kv[:, :2 * KV_WIDTH]
    kv_out[:, 2 * KV_WIDTH:3 * KV_WIDTH] = ksn
    kv_out[:, 3 * KV_WIDTH:] = vs
    w = win_ref[...]
    kwn = _seg_rms64(w[:, :KV_WIDTH], kwg_ref[...])
    vw = w[:, KV_WIDTH:]
    win_out[:, :KV_WIDTH] = kwn
    win_out[:, KV_WIDTH:] = vw

    pos = (pl.program_id(0) * tm + lax.broadcasted_iota(jnp.int32, (tm, 1), 0)) % length
    seg = pos % ATT_TK
    pos_lanes = _pos_lanes(seg // SEL_BLOCK, seg % SEL_BLOCK, tm)
    lane = lax.broadcasted_iota(jnp.int32, (tm, 128), 1)
    block_lanes = jnp.where(lane - SEL_BLOCK == pos // SEL_BLOCK, 1, 0)
    ksel_out[:, :KV_WIDTH] = ksn.astype(BF16)
    ksel_out[:, KV_WIDTH:] = (pos_lanes + block_lanes).astype(BF16)
    kwin_out[:, :KV_WIDTH] = kwn.astype(BF16)
    kwin_out[:, KV_WIDTH:] = pos_lanes.astype(BF16)
    ones = jnp.where(lax.broadcasted_iota(jnp.int32, (NSA_DH, tm), 0) == 0, 1.0, 0.0)

    def values_t(v):
        vt = v.T
        return jnp.concatenate([vt[:NSA_DH], ones, vt[NSA_DH:], ones], axis=0).astype(BF16)

    vselt_out[...] = values_t(vs)
    vwint_out[...] = values_t(vw)


def _nsa_prep_t_call(z, qg, ksg, kwg, tm, length):
    m = z.shape[0]

    def zspec(col, width):
        return pl.BlockSpec((tm, width), lambda i, c=col // width: (i, c))

    def rows(width):
        return pl.BlockSpec((tm, width), lambda i: (i, 0))

    def cols(height):
        return pl.BlockSpec((height, tm), lambda i: (0, i))

    return pl.pallas_call(
        functools.partial(_nsa_prep_t_body, tm=tm, length=length),
        grid=(m // tm,),
        in_specs=[zspec(C_NQ, 512), zspec(C_KV, 512), zspec(C_WIN, 256), zspec(C_MISC, 128),
                  pl.BlockSpec((1, 512), lambda i: (0, 0)),
                  pl.BlockSpec((1, 128), lambda i: (0, 0)),
                  pl.BlockSpec((1, 128), lambda i: (0, 0))],
        out_specs=[cols(NSA_WIDTH), cols(32), rows(512), rows(256), rows(256), cols(256), rows(256), cols(256)],
        out_shape=[jax.ShapeDtypeStruct((NSA_WIDTH, m), BF16),
                   jax.ShapeDtypeStruct((32, m), F32),
                   jax.ShapeDtypeStruct((m, 512), F32),
                   jax.ShapeDtypeStruct((m, 256), F32),
                   jax.ShapeDtypeStruct((m, 256), BF16),
                   jax.ShapeDtypeStruct((256, m), BF16),
                   jax.ShapeDtypeStruct((m, 256), BF16),
                   jax.ShapeDtypeStruct((256, m), BF16)],
        compiler_params=_cparams("parallel"),
        name="nsa_prep_t",
    )(z, z, z, z, qg, ksg, kwg)


def _compress_rows(load_rows, w_ref, kind, nblk):
    acc = jnp.zeros((nblk, 2 * KV_WIDTH), F32)
    for j in range(CMP_STRIDE):
        acc = acc + _dot(load_rows(j).astype(BF16), w_ref[kind, j])
    row = lax.broadcasted_iota(jnp.int32, (nblk, 1), 0)
    bot_next = pltpu.roll(acc[:, KV_WIDTH:], nblk - 1, 0)
    return jnp.where(row < nblk - 1, acc[:, :KV_WIDTH] + bot_next, 0.0)


def _compress_body(krows_ref, vrows_ref, w_ref, g_ref, kc_out, vct_out, *, nblk):
    def loader(ref):
        return lambda j: ref[pl.ds(j, nblk, stride=CMP_STRIDE), :]

    kc = _compress_rows(loader(krows_ref), w_ref, 0, nblk)
    vc = _compress_rows(loader(vrows_ref), w_ref, 1, nblk)
    kc_out[:, :KV_WIDTH] = _seg_rms64(kc, g_ref[...]).astype(BF16)
    ec = CMP_STRIDE * lax.broadcasted_iota(jnp.int32, (nblk, 1), 0) + (CMP_LEN - 1)
    kc_out[:, KV_WIDTH:] = _pos_lanes(ec // SEL_BLOCK, ec % SEL_BLOCK, nblk).astype(BF16)
    vct_out[...] = vc.T.astype(BF16)


def _compress_call(kv_new3, w_cmp, kcg):
    nb, length, _ = kv_new3.shape
    nblk = length // CMP_STRIDE
    return pl.pallas_call(
        functools.partial(_compress_body, nblk=nblk),
        grid=(nb,),
        in_specs=[pl.BlockSpec((None, length, KV_WIDTH), lambda b: (b, 0, 0)),
                  pl.BlockSpec((None, length, KV_WIDTH), lambda b: (b, 0, 1)),
                  pl.BlockSpec((2, CMP_STRIDE, KV_WIDTH, 2 * KV_WIDTH), lambda b: (0, 0, 0, 0)),
                  pl.BlockSpec((1, KV_WIDTH), lambda b: (0, 0))],
        out_specs=[pl.BlockSpec((None, nblk, 2 * KV_WIDTH), lambda b: (b, 0, 0)),
                   pl.BlockSpec((None, KV_WIDTH, nblk), lambda b: (b, 0, 0))],
        out_shape=[jax.ShapeDtypeStruct((nb, nblk, 2 * KV_WIDTH), BF16),
                   jax.ShapeDtypeStruct((nb, KV_WIDTH, nblk), BF16)],
        compiler_params=_cparams("parallel"),
        name="nsa_compress",
    )(kv_new3, kv_new3, w_cmp, kcg)


def _round_up(a, b):
    return -(-a // b) * b


def _tile_rows(a, n):
    return jnp.concatenate([a] * n, axis=0)


def _softmax_rows(s, valid):
    s = jnp.where(valid, s, -BIG)
    m = jnp.max(s, axis=-1, keepdims=True)
    e = jnp.where(valid, jnp.exp(s - m), 0.0)
    l = jnp.sum(e, axis=-1, keepdims=True)
    return e / jnp.where(l > 0.0, l, 1.0)


def _sel_rows(ns):
    return _round_up(ns, 8)


def _imp_scratch_rows(ns):
    return 8 + SEL_PER * _sel_rows(ns) + 8


def _select_blocks_t(imp_t, imp_scr, sc_scr, t_row, ns, n_sel):
    nb, tq = imp_t.shape
    nsr = _sel_rows(ns)
    rows = _imp_scratch_rows(ns)
    imp_scr[0:8, :] = jnp.zeros((8, tq), F32)
    imp_scr[8 + nb:rows, :] = jnp.zeros((rows - 8 - nb, tq), F32)
    imp_scr[8:8 + nb, :] = imp_t

    def ld(off):
        return imp_scr[pl.ds(8 + off, nsr, stride=SEL_PER), :]

    p_slc = ld(-1) + 2.0 * (ld(0) + ld(1) + ld(2)) + ld(3)
    j = lax.broadcasted_iota(jnp.int32, (nsr, 1), 0)
    cur = t_row // SEL_BLOCK
    forced = (j == 0) | (j == cur) | (j == cur - 1)
    score = jnp.where(forced, BIG, jnp.where(j <= cur, p_slc, -BIG))
    sc_scr[...] = score

    def rank_body(k, cnt):
        sk = sc_scr[pl.ds(k, 1), :]
        beats = (sk > score) | ((sk == score) & (k < j))
        return cnt + jnp.where(beats, 1.0, 0.0)

    cnt = lax.fori_loop(0, ns, rank_body, jnp.zeros((nsr, tq), F32))
    return jnp.where((cnt < float(n_sel)) & (j < ns), 1.0, 0.0)


def _select_blocks(imp, imp_scr, sc_scr, t_row, ns, n_sel):
    sel_t = _select_blocks_t(imp.T, imp_scr, sc_scr, t_row, ns, n_sel)
    nsp = _round_up(ns, 128)
    if nsp > sel_t.shape[0]:
        sel_t = jnp.concatenate([sel_t, jnp.zeros((nsp - sel_t.shape[0], sel_t.shape[1]), F32)], axis=0)
    return sel_t.T.astype(BF16)


def _expand_sel(sel, k0, tk):
    nsp = sel.shape[1]
    kblk = (k0 + lax.broadcasted_iota(jnp.int32, (nsp, tk), 1)) // SEL_BLOCK
    expand = jnp.where(lax.broadcasted_iota(jnp.int32, (nsp, tk), 0) == kblk, 1.0, 0.0).astype(BF16)
    return _dot(sel, expand)


def _online_update(carry, s, valid, v_t):
    m, l, acc = carry
    s = jnp.where(valid, s, -BIG)
    m_new = jnp.maximum(m, jnp.max(s, axis=-1, keepdims=True))
    alpha = jnp.exp(m - m_new)
    e = jnp.where(valid, jnp.exp(s - m_new), 0.0)
    l = alpha * l + jnp.sum(e, axis=-1, keepdims=True)
    acc = alpha * acc + _dot(e.astype(BF16), v_t)
    return m_new, l, acc


def _online_init(rows):
    return (jnp.full((rows, 1), -BIG, F32), jnp.zeros((rows, 1), F32), jnp.zeros((rows, NSA_DH), F32))


def _online_finish(carry):
    _, l, acc = carry
    return acc / jnp.where(l > 0.0, l, 1.0)


NEG_MASK = -(2.0 ** 100)
CMP_TQ = 128
ATT_TQ = 256
ATT_TK = 512


def _slope_lanes(g, tq):
    r = lax.broadcasted_iota(jnp.int32, (1, NSA_REP * tq), 1) // tq
    out = jnp.full((1, NSA_REP * tq), SLOPES[g * NSA_REP], F32)
    for k in range(1, NSA_REP):
        out = jnp.where(r == k, SLOPES[g * NSA_REP + k], out)
    return out


def _q_aug_t(qt_ref, g, tq, qsel_t):
    lanes = NSA_REP * tq
    qrows = jnp.concatenate(
        [qt_ref[(g * NSA_REP + r) * NSA_DH:(g * NSA_REP + r + 1) * NSA_DH, :] for r in range(NSA_REP)], axis=1)
    zero = jnp.zeros((NSA_DH, lanes), BF16)
    row = lax.broadcasted_iota(jnp.int32, (NSA_DH, lanes), 0)
    slope = _slope_lanes(g, tq)
    alibi = jnp.where(row == 0, slope * float(SEL_BLOCK), jnp.where(row == 1, slope, 0.0)).astype(BF16)
    sel = zero if qsel_t is None else jnp.concatenate([qsel_t] * NSA_REP, axis=1)
    groups = [qrows, zero] if g == 0 else [zero, qrows]
    return jnp.concatenate(groups + [alibi, sel], axis=0)


def _tile_lanes(a, n):
    return jnp.concatenate([a] * n, axis=1)


def _cmp_select_body(qt_ref, kc_ref, vct_ref, oct_ref, qselt_ref, cnt_ref, imp_scr, sc_scr, *, length, tq):
    nb = length // CMP_STRIDE
    ns = length // SEL_BLOCK
    nsr = _sel_rows(ns)
    n_sel = min(SEL_TOP_N, ns)
    t0 = pl.program_id(1) * tq
    t_row = t0 + lax.broadcasted_iota(jnp.int32, (1, tq), 1)
    ec = CMP_STRIDE * lax.broadcasted_iota(jnp.int32, (nb, 1), 0) + (CMP_LEN - 1)
    hidden = _tile_lanes(jnp.where(t_row >= ec, 0.0, NEG_MASK), NSA_REP)
    counts = []
    for g in range(NSA_G):
        s_t = _dot(kc_ref[...], _q_aug_t(qt_ref, g, tq, None)) + hidden
        m = jnp.max(s_t, axis=0, keepdims=True)
        e = jnp.where(hidden < 0.0, 0.0, jnp.exp(s_t - m))
        l = jnp.sum(e, axis=0, keepdims=True)
        p_t = e * jnp.where(l > 0.0, 1.0 / l, 0.0)
        o_ct = _dot(vct_ref[g * NSA_DH:(g + 1) * NSA_DH, :], p_t.astype(BF16))
        imp_t = p_t[:, 0:tq]
        for r in range(1, NSA_REP):
            imp_t = imp_t + p_t[:, r * tq:(r + 1) * tq]
        sel_t = _select_blocks_t(imp_t, imp_scr, sc_scr, t_row, ns, n_sel)
        qsel = jnp.where(sel_t > 0.5, 0.0, NEG_MASK)
        if nsr < SEL_BLOCK:
            qsel = jnp.concatenate([qsel, jnp.full((SEL_BLOCK - nsr, tq), NEG_MASK, F32)], axis=0)
        qselt_ref[g * SEL_BLOCK:(g + 1) * SEL_BLOCK, :] = qsel.astype(BF16)
        cnt = _dot_nt(jnp.ones((8, tq), BF16), sel_t.astype(BF16))
        if nsr < SEL_BLOCK:
            cnt = jnp.concatenate([cnt, jnp.zeros((8, SEL_BLOCK - nsr), F32)], axis=1)
        counts.append(cnt)
        for r in range(NSA_REP):
            h = g * NSA_REP + r
            oct_ref[h * NSA_DH:(h + 1) * NSA_DH, :] = o_ct[:, r * tq:(r + 1) * tq]
    cnt_ref[...] = jnp.concatenate(counts, axis=1)


def _cmp_select_call(q_t, kc_aug, vc_t, n_batch, length):
    tq = CMP_TQ
    nq = length // tq
    nb = length // CMP_STRIDE
    ns = length // SEL_BLOCK
    assert ns <= SEL_BLOCK
    m = n_batch * length
    return pl.pallas_call(
        functools.partial(_cmp_select_body, length=length, tq=tq),
        grid=(n_batch, nq),
        in_specs=[pl.BlockSpec((NSA_WIDTH, tq), lambda b, i: (0, b * nq + i)),
                  pl.BlockSpec((None, nb, 2 * KV_WIDTH), lambda b, i: (b, 0, 0)),
                  pl.BlockSpec((None, KV_WIDTH, nb), lambda b, i: (b, 0, 0))],
        out_specs=[pl.BlockSpec((NSA_WIDTH, tq), lambda b, i: (0, b * nq + i)),
                   pl.BlockSpec((NSA_G * SEL_BLOCK, tq), lambda b, i: (0, b * nq + i)),
                   pl.BlockSpec((8, NSA_G * SEL_BLOCK), lambda b, i: (b * nq + i, 0))],
        out_shape=[jax.ShapeDtypeStruct((NSA_WIDTH, m), F32),
                   jax.ShapeDtypeStruct((NSA_G * SEL_BLOCK, m), BF16),
                   jax.ShapeDtypeStruct((n_batch * nq * 8, NSA_G * SEL_BLOCK), F32)],
        scratch_shapes=[pltpu.VMEM((_imp_scratch_rows(ns), tq), F32), pltpu.VMEM((_sel_rows(ns), tq), F32)],
        compiler_params=_cparams("parallel", "parallel"),
        name="nsa_cmp_select",
    )(q_t, kc_aug, vc_t)


def _nsa_attn_body(flags_ref, qt_ref, qselt_ref, gt_ref, oct_ref, ksel_ref, vselt_ref, kwin_ref, vwint_ref,
                   nn_ref, o_ref, m_scr, acc_scr, on_scr, *, length, tq, tk):
    b, i = pl.program_id(0), pl.program_id(1)
    nq = pl.num_programs(1)
    n_kt_all = length // tk
    n_win = WINDOW // tq
    t0 = i * tq
    t_row = t0 + lax.broadcasted_iota(jnp.int32, (1, tq), 1)
    for g in range(NSA_G):
        qa = _q_aug_t(qt_ref, g, tq, qselt_ref[g * SEL_BLOCK:(g + 1) * SEL_BLOCK, :])
        slope = _slope_lanes(g, tq)
        vrows = slice(g * 2 * NSA_DH, (g + 1) * 2 * NSA_DH)

        def reset():
            m_scr[...] = jnp.full(m_scr.shape, -BIG, F32)
            acc_scr[...] = jnp.zeros(acc_scr.shape, F32)

        def step(k_ref, vt_ref, k0, rows, mask):
            s_t = _dot(k_ref[pl.ds(k0, rows), :], qa)
            if mask is not None:
                kpos = k0 + lax.broadcasted_iota(jnp.int32, (rows, 1), 0)
                ok = (kpos <= t_row) if mask == 'causal' else (kpos >= t_row - WINDOW)
                s_t = s_t + _tile_lanes(jnp.where(ok, 0.0, NEG_MASK), NSA_REP)
            c = slope * ((k0 // ATT_TK) * ATT_TK - t0).astype(F32)
            m_old = m_scr[...]
            m_new = jnp.maximum(m_old, jnp.max(s_t, axis=0, keepdims=True) + c)
            e = jnp.exp(s_t - (m_new - c)).astype(BF16)
            acc_scr[...] = jnp.exp(m_old - m_new) * acc_scr[...] + _dot(vt_ref[vrows, pl.ds(k0, rows)], e)
            m_scr[...] = m_new

        def finish():
            acc = acc_scr[...]
            l = acc[NSA_DH:NSA_DH + 1, :]
            return acc[0:NSA_DH, :] / jnp.where(l > 0.0, l, 1.0)

        reset()
        last = (t0 + tq + tk - 1) // tk - 1
        fbase = ((b * nq + i) * NSA_G + g) * n_kt_all

        def sel_tile(kt, carry):
            @pl.when(flags_ref[fbase + kt] != 0)
            def _():
                step(ksel_ref, vselt_ref, pl.multiple_of(kt * tk, tk), tk, None)
            return carry

        lax.fori_loop(0, last, sel_tile, 0)
        step(ksel_ref, vselt_ref, pl.multiple_of(last * tk, tk), tk, 'causal')
        o_s = finish()

        reset()

        @pl.when(i >= n_win)
        def _():
            step(kwin_ref, vwint_ref, pl.multiple_of((i - n_win) * tq, tq), tq, 'window')

        def win_tile(w, carry):
            step(kwin_ref, vwint_ref, pl.multiple_of(w * tq, tq), tq, None)
            return carry

        lax.fori_loop(jnp.maximum(i - n_win + 1, 0), i, win_tile, 0)
        step(kwin_ref, vwint_ref, pl.multiple_of(t0, tq), tq, 'causal')
        o_w = finish()

        for r in range(NSA_REP):
            h = g * NSA_REP + r
            hs = slice(h * NSA_DH, (h + 1) * NSA_DH)
            ls = slice(r * tq, (r + 1) * tq)
            on_scr[hs, :] = (gt_ref[h:h + 1, :] * oct_ref[hs, :]
                             + gt_ref[NSA_HEADS + h:NSA_HEADS + h + 1, :] * o_s[:, ls]
                             + gt_ref[2 * NSA_HEADS + h:2 * NSA_HEADS + h + 1, :] * o_w[:, ls])
    on = on_scr[...]
    ms = jnp.sum(on * on, axis=0, keepdims=True) * (1.0 / NSA_WIDTH)
    o_ref[...] = (on * lax.rsqrt(ms + RMS_EPS) * nn_ref[...]).T


def _nsa_attn_call(flags, q_t, qsel_t, g_t, oc_t, ksel, vsel_t, kwin, vwin_t, nn_col, n_batch, length):
    tq, tk = ATT_TQ, ATT_TK
    nq = length // tq
    assert WINDOW % tq == 0 and length % tk == 0 and tk % tq == 0
    body = functools.partial(_nsa_attn_body, length=length, tq=tq, tk=tk)

    def cols(rows):
        return pl.BlockSpec((rows, tq), lambda b, i, fl: (0, b * nq + i))

    def keys():
        return pl.BlockSpec((length, 2 * KV_WIDTH), lambda b, i, fl: (b, 0))

    def values():
        return pl.BlockSpec((2 * KV_WIDTH, length), lambda b, i, fl: (0, b))

    grid_spec = pltpu.PrefetchScalarGridSpec(
        num_scalar_prefetch=1,
        grid=(n_batch, nq),
        in_specs=[cols(NSA_WIDTH), cols(NSA_G * SEL_BLOCK), cols(32), cols(NSA_WIDTH),
                  keys(), values(), keys(), values(),
                  pl.BlockSpec((NSA_WIDTH, 1), lambda b, i, fl: (0, 0))],
        out_specs=pl.BlockSpec((tq, NSA_WIDTH), lambda b, i, fl: (b * nq + i, 0)),
        scratch_shapes=[pltpu.VMEM((1, NSA_REP * tq), F32),
                        pltpu.VMEM((2 * NSA_DH, NSA_REP * tq), F32),
                        pltpu.VMEM((NSA_WIDTH, tq), F32)])
    return pl.pallas_call(
        body,
        grid_spec=grid_spec,
        out_shape=jax.ShapeDtypeStruct((n_batch * length, NSA_WIDTH), F32),
        compiler_params=_cparams("parallel", "parallel"),
        name="nsa_attn",
    )(flags, q_t, qsel_t, g_t, oc_t, ksel, vsel_t, kwin, vwin_t, nn_col)


def _prep_weights(p):
    depth = p['w_in'].shape[0]
    offs = np.concatenate([[0], np.cumsum(IN_SIZES)])
    piece = lambda i: p['w_in'][:, :, int(offs[i]):int(offs[i + 1])]
    order = [0, 1, 2, 4, 5, 6, 7, 8, 9, 10, 11, 3, 12]
    cols = [piece(i) for i in order]
    used = sum(IN_SIZES)
    cols.append(jnp.zeros((depth, D_MODEL, N_Z - used), F32))
    w_in = jnp.concatenate(cols, axis=-1).astype(BF16)
    wa = jnp.zeros((depth, 128, GLA_HEADS * GLA_DK), F32).at[:, :GLA_GATE_RANK, :].set(p['w_gla_a']).astype(BF16)

    def cmp_weights(w):
        eye = jnp.eye(NSA_G, dtype=F32)
        bd = jnp.einsum('gh,ljde->ljgdhe', eye, w).reshape(depth, CMP_LEN, KV_WIDTH, KV_WIDTH)
        return jnp.concatenate([bd[:, :CMP_STRIDE], bd[:, CMP_STRIDE:]], axis=-1)

    w_cmp = jnp.stack([cmp_weights(p['w_ck']), cmp_weights(p['w_cv'])], axis=1).astype(BF16)
    tile = lambda a, n: jnp.tile(a, (1, n))[:, None, :]
    return {
        'w_in': w_in, 'wa': wa, 'ba': p['b_gla_a'][:, None, :], 'w_cmp': w_cmp,
        'norm1': p['norm1'][:, None, :], 'norm2': p['norm2'][:, None, :],
        'gla_norm': p['gla_norm'][:, None, :], 'nsa_norm': p['nsa_norm'][:, None, :],
        'q_norm': tile(p['q_norm'], NSA_HEADS), 'kc_norm': tile(p['kc_norm'], NSA_G),
        'ks_norm': tile(p['ks_norm'], NSA_G), 'kw_norm': tile(p['kw_norm'], NSA_G),
        'w_out': p['w_out'].astype(BF16), 'w_up': p['w_up'].astype(BF16),
        'w_gate': p['w_gate'].astype(BF16), 'w_down': p['w_down'].astype(BF16),
        'conv_w': p['conv_w'], 'conv_b': p['conv_b'][:, None, :],
    }


def _split_mod(mod, shape):
    return [mod[:, k * D_MODEL:(k + 1) * D_MODEL].reshape(shape) for k in range(6)]


def _layer_prompt(x, mod, pw, l):
    nb, length, _ = x.shape
    m = nb * length
    sh1, sc1, g1, sh2, sc2, g2 = _split_mod(mod, (nb, 1, D_MODEL))
    x2 = x.reshape(m, D_MODEL)
    tm = min(512, length)
    z = _inproj_call(x2, pw['norm1'][l], sc1, sh1, pw['w_in'][l], min(1024, length), length)
    s0 = jnp.zeros((nb, GLA_HEADS, GLA_DK, GLA_DV), F32)
    o_g, s_t = _gla_call(z, pw['wa'][l], pw['ba'][l], pw['gla_norm'][l], s0, nb, length, GLA_ROWS, length)
    q_t, g_t, kv_new, win_new, ksel, vsel_t, kwin, vwin_t = _nsa_prep_t_call(
        z, pw['q_norm'][l], pw['ks_norm'][l], pw['kw_norm'][l], tm, length)
    kc_aug, vc_t = _compress_call(kv_new.reshape(nb, length, 4 * KV_WIDTH), pw['w_cmp'][l], pw['kc_norm'][l])
    oc_t, qsel_t, cnt = _cmp_select_call(q_t, kc_aug, vc_t, nb, length)
    per_tile = ATT_TK // SEL_BLOCK
    flags = cnt.reshape(-1, ATT_TQ // CMP_TQ, 8, NSA_G, SEL_BLOCK // per_tile, per_tile)
    flags = jnp.sum(flags[:, :, 0, :, :length // ATT_TK], axis=(1, -1))
    flags = (flags > 0.0).astype(jnp.int32).reshape(-1)
    o_n = _nsa_attn_call(flags, q_t, qsel_t, g_t, oc_t, ksel, vsel_t, kwin, vwin_t,
                         pw['nsa_norm'][l].reshape(NSA_WIDTH, 1), nb, length)
    x_mid, h2 = _outproj_call(o_g, o_n, x2, g1, pw['w_out'][l], pw['norm2'][l], sc2, sh2, tm, length)
    tm_f = min(1024, length)
    x_out, u_keep = _ffn_call(h2, x_mid, g2, pw['w_up'][l], pw['w_gate'][l], pw['w_down'][l],
                              pw['conv_w'][l], pw['conv_b'][l], tm_f, length)
    conv = u_keep.reshape(nb, length // tm_f, 8, D_FF)[:, -1, 8 - (CONV_W - 1):, :]
    wl = min(WINDOW, length)
    return (x_out.reshape(nb, length, D_MODEL),
            jnp.swapaxes(s_t, -1, -2),
            kv_new.reshape(nb, length, 4, NSA_G, NSA_DH),
            win_new.reshape(nb, length, 2, NSA_G, NSA_DH)[:, length - wl:],
            conv)


SAMPLE_ROWS = 16


def _page_dma(pt_ref, ckv_hbm, buf, sem, layer, batch, slot, col0, n_pages, start):
    splits, width = buf.shape[1], buf.shape[3]

    def body(p, carry):
        for s in range(splits):
            cp = pltpu.make_async_copy(
                ckv_hbm.at[layer, pt_ref[batch, p], :, pl.ds(col0 + s * width, width)],
                buf.at[slot, s, pl.ds(pl.multiple_of(p * PAGE_SIZE, PAGE_SIZE), PAGE_SIZE), :],
                sem.at[slot])
            if start:
                cp.start()
            else:
                cp.wait()
        return carry

    lax.fori_loop(0, n_pages, body, 0)


def _gather_pages(pt_ref, ckv_hbm, buf, sem, layer, col0, n_pages):
    b = pl.program_id(0)
    dma = functools.partial(_page_dma, pt_ref, ckv_hbm, buf, sem, layer)

    @pl.when(b == 0)
    def _():
        dma(0, 0, col0, n_pages, True)

    @pl.when(b + 1 < pl.num_programs(0))
    def _():
        dma(b + 1, (b + 1) % 2, col0, n_pages, True)

    slot = b % 2
    dma(b, slot, col0, n_pages, False)
    return slot


def _q_stack(q_ref, g):
    return jnp.concatenate(
        [q_ref[:, (g * NSA_REP + r) * NSA_DH:(g * NSA_REP + r + 1) * NSA_DH] for r in range(NSA_REP)], axis=0)


def _slope_rows(g, rows):
    return jnp.concatenate([jnp.full((rows, 1), SLOPES[g * NSA_REP + r], F32) for r in range(NSA_REP)], axis=0)


def _s1_body(pt_ref, ckv_hbm, q_ref, w_ref, g_ref, oc_ref, sel_ref, buf, sem, imp_scr, sc_scr,
             *, layer, n_pages, past_len):
    slot = _gather_pages(pt_ref, ckv_hbm, buf, sem, layer, 0, n_pages)
    ts = SAMPLE_ROWS
    nblk = past_len // CMP_STRIDE
    ns = past_len // SEL_BLOCK + 1
    nsp = _round_up(ns, 128)
    n_sel = min(SEL_TOP_N, ns)

    def loader(kind):
        return lambda j: buf[slot, kind, pl.ds(j, nblk, stride=CMP_STRIDE), :]

    kc = _seg_rms64(_compress_rows(loader(0), w_ref, 0, nblk), g_ref[...]).astype(BF16)
    vc = _compress_rows(loader(1), w_ref, 1, nblk).astype(BF16)

    t_col = past_len + lax.broadcasted_iota(jnp.int32, (ts, 1), 0)
    t_row = past_len + lax.broadcasted_iota(jnp.int32, (1, 128), 1)
    ec = CMP_STRIDE * lax.broadcasted_iota(jnp.int32, (1, nblk), 1) + (CMP_LEN - 1)
    dist = t_col - ec
    valid4 = _tile_rows(dist >= 0, NSA_REP)
    distf4 = _tile_rows(dist.astype(F32), NSA_REP)
    heads = []
    for g in range(NSA_G):
        ks = slice(g * NSA_DH, (g + 1) * NSA_DH)
        s = _dot_nt(_q_stack(q_ref, g), kc[:, ks]) - _slope_rows(g, ts) * distf4
        p = _softmax_rows(s, valid4)
        o_c = _dot(p.astype(BF16), vc[:, ks])
        imp = p[0:ts]
        for r in range(1, NSA_REP):
            imp = imp + p[r * ts:(r + 1) * ts]
        imp = jnp.concatenate([imp, jnp.zeros((128 - ts, nblk), F32)], axis=0)
        sel = _select_blocks(imp, imp_scr, sc_scr, t_row, ns, n_sel)
        sel_ref[:, g * nsp:(g + 1) * nsp] = sel[0:ts, :].astype(F32)
        heads += [o_c[r * ts:(r + 1) * ts] for r in range(NSA_REP)]
    oc_ref[...] = jnp.concatenate(heads, axis=-1)


def _s1_call(page_table, ckv, q_bf, w_cmp, kcg, layer, past_len):
    nbatch, n_pages = page_table.shape
    ts = SAMPLE_ROWS
    ns = past_len // SEL_BLOCK + 1
    nsp = _round_up(ns, 128)
    body = functools.partial(_s1_body, layer=layer, n_pages=n_pages, past_len=past_len)
    grid_spec = pltpu.PrefetchScalarGridSpec(
        num_scalar_prefetch=1,
        grid=(nbatch,),
        in_specs=[pl.BlockSpec(memory_space=pl.ANY),
                  pl.BlockSpec((ts, NSA_WIDTH), lambda b, pt: (b, 0)),
                  pl.BlockSpec((2, CMP_STRIDE, KV_WIDTH, 2 * KV_WIDTH), lambda b, pt: (0, 0, 0, 0)),
                  pl.BlockSpec((1, KV_WIDTH), lambda b, pt: (0, 0))],
        out_specs=[pl.BlockSpec((ts, NSA_WIDTH), lambda b, pt: (b, 0)),
                   pl.BlockSpec((ts, NSA_G * nsp), lambda b, pt: (b, 0))],
        scratch_shapes=[pltpu.VMEM((2, 2, past_len, KV_WIDTH), F32),
                        pltpu.SemaphoreType.DMA((2,)),
                        pltpu.VMEM((_imp_scratch_rows(ns), 128), F32),
                        pltpu.VMEM((_sel_rows(ns), 128), F32)])
    return pl.pallas_call(
        body,
        grid_spec=grid_spec,
        out_shape=[jax.ShapeDtypeStruct((nbatch * ts, NSA_WIDTH), F32),
                   jax.ShapeDtypeStruct((nbatch * ts, NSA_G * nsp), F32)],
        compiler_params=_cparams("arbitrary"),
        name="nsa_sample_select",
    )(page_table, ckv, q_bf, w_cmp, kcg)


def _s2_body(pt_ref, ckv_hbm, q_ref, misc_ref, oc_ref, sel_ref, kvn_ref, cwin_ref, wn_ref, nn_ref,
             o_ref, buf, sem, *, layer, n_pages, past_len, wb, tk):
    slot = _gather_pages(pt_ref, ckv_hbm, buf, sem, layer, 2 * KV_WIDTH, n_pages)
    ts = SAMPLE_ROWS
    nsp = sel_ref.shape[1] // NSA_G
    t_col = past_len + lax.broadcasted_iota(jnp.int32, (ts, 1), 0)
    gates = jax.nn.sigmoid(misc_ref[:, MISC_NG:MISC_NG + 3 * NSA_HEADS])
    zpad = jnp.zeros((128 - ts, NSA_DH), BF16)
    d_new = t_col - (past_len + lax.broadcasted_iota(jnp.int32, (1, 128), 1))
    d_win = t_col - (past_len - wb + lax.broadcasted_iota(jnp.int32, (1, wb), 1))
    heads = []
    for g in range(NSA_G):
        ks = slice(g * NSA_DH, (g + 1) * NSA_DH)
        vs = slice(KV_WIDTH + g * NSA_DH, KV_WIDTH + (g + 1) * NSA_DH)
        q_stack = _q_stack(q_ref, g)
        slope_rows = _slope_rows(g, ts)
        sel = sel_ref[:, g * nsp:(g + 1) * nsp].astype(BF16)

        def bias(d):
            return slope_rows * _tile_rows(d.astype(F32), NSA_REP)

        def kt_body(kt, carry):
            k0 = pl.multiple_of(kt * tk, tk)
            selx = _expand_sel(sel, k0, tk)
            d = t_col - (k0 + lax.broadcasted_iota(jnp.int32, (1, tk), 1))
            valid = _tile_rows((selx > 0.5) & (d >= 0), NSA_REP)
            s = _dot_nt(q_stack, buf[slot, 0, pl.ds(k0, tk), ks].astype(BF16)) - bias(d)
            return _online_update(carry, s, valid, buf[slot, 0, pl.ds(k0, tk), vs].astype(BF16))

        carry = lax.fori_loop(0, past_len // tk, kt_body, _online_init(NSA_REP * ts))
        k_n = jnp.concatenate([kvn_ref[:, ks], zpad], axis=0)
        v_n = jnp.concatenate([kvn_ref[:, vs], zpad], axis=0)
        valid = _tile_rows((_expand_sel(sel, past_len, 128) > 0.5) & (d_new >= 0), NSA_REP)
        o_s = _online_finish(_online_update(carry, _dot_nt(q_stack, k_n) - bias(d_new), valid, v_n))

        valid = _tile_rows((d_win >= 0) & (d_win <= WINDOW), NSA_REP)
        carry = _online_update(_online_init(NSA_REP * ts),
                               _dot_nt(q_stack, cwin_ref[:, ks].astype(BF16)) - bias(d_win), valid,
                               cwin_ref[:, vs].astype(BF16))
        k_n = jnp.concatenate([wn_ref[:, ks], zpad], axis=0)
        v_n = jnp.concatenate([wn_ref[:, vs], zpad], axis=0)
        valid = _tile_rows((d_new >= 0) & (d_new <= WINDOW), NSA_REP)
        o_w = _online_finish(_online_update(carry, _dot_nt(q_stack, k_n) - bias(d_new), valid, v_n))

        for r in range(NSA_REP):
            h = g * NSA_REP + r
            rs = slice(r * ts, (r + 1) * ts)
            heads.append(gates[:, h:h + 1] * oc_ref[:, h * NSA_DH:(h + 1) * NSA_DH]
                         + gates[:, NSA_HEADS + h:NSA_HEADS + h + 1] * o_s[rs]
                         + gates[:, 2 * NSA_HEADS + h:2 * NSA_HEADS + h + 1] * o_w[rs])
    o_ref[...] = _rms_rows(jnp.concatenate(heads, axis=-1), nn_ref[...])


def _s2_call(page_table, ckv, q_bf, z, o_c, sel, kvs_bf, cwin, win_bf, nn, layer, past_len):
    nbatch, n_pages = page_table.shape
    ts = SAMPLE_ROWS
    wb = cwin.shape[2]
    tk = min(512, past_len)
    body = functools.partial(_s2_body, layer=layer, n_pages=n_pages, past_len=past_len, wb=wb, tk=tk)

    def rows(width, col_block=0):
        return pl.BlockSpec((ts, width), lambda b, pt, c=col_block: (b, c))

    grid_spec = pltpu.PrefetchScalarGridSpec(
        num_scalar_prefetch=1,
        grid=(nbatch,),
        in_specs=[pl.BlockSpec(memory_space=pl.ANY),
                  rows(NSA_WIDTH), rows(128, C_MISC // 128), rows(NSA_WIDTH), rows(sel.shape[1]),
                  rows(2 * KV_WIDTH),
                  pl.BlockSpec((None, None, wb, 2 * KV_WIDTH), lambda b, pt: (layer, b, 0, 0)),
                  rows(2 * KV_WIDTH),
                  pl.BlockSpec((1, NSA_WIDTH), lambda b, pt: (0, 0))],
        out_specs=rows(NSA_WIDTH),
        scratch_shapes=[pltpu.VMEM((2, 1, past_len, 2 * KV_WIDTH), F32),
                        pltpu.SemaphoreType.DMA((2,))])
    return pl.pallas_call(
        body,
        grid_spec=grid_spec,
        out_shape=jax.ShapeDtypeStruct((nbatch * ts, NSA_WIDTH), F32),
        compiler_params=_cparams("arbitrary"),
        name="nsa_sample_attn",
    )(page_table, ckv, q_bf, z, o_c, sel, kvs_bf, cwin, win_bf, nn)


def _layer_sample(x, mod, pw, l, t_real, ckv, cwin, page_table, s0, conv_prev):
    nb, ts, _ = x.shape
    m = nb * ts
    past_len = page_table.shape[1] * PAGE_SIZE
    sh1, sc1, g1, sh2, sc2, g2 = [jnp.repeat(a, ts, axis=0) for a in _split_mod(mod, (nb, D_MODEL))]
    x2 = x.reshape(m, D_MODEL)
    z = _inproj_call(x2, pw['norm1'][l], sc1, sh1, pw['w_in'][l], m, ts)
    o_g, s_t = _gla_call(z, pw['wa'][l], pw['ba'][l], pw['gla_norm'][l], s0, nb, ts, ts, t_real)
    q_bf, kv_new, win_new, kvs_bf, win_bf = _nsa_prep_call(
        z, pw['q_norm'][l], pw['ks_norm'][l], pw['kw_norm'][l], m)
    o_c, sel = _s1_call(page_table, ckv, q_bf, pw['w_cmp'][l], pw['kc_norm'][l], l, past_len)
    o_n = _s2_call(page_table, ckv, q_bf, z, o_c, sel, kvs_bf, cwin, win_bf, pw['nsa_norm'][l], l, past_len)
    x_mid, h2 = _outproj_call(o_g, o_n, x2, g1, pw['w_out'][l], pw['norm2'][l], sc2, sh2, m, ts)
    zrow = jnp.zeros((nb, ts, D_FF), F32)
    p1 = zrow.at[:, 0].set(conv_prev[:, 1]).reshape(m, D_FF)
    p2 = zrow.at[:, 0].set(conv_prev[:, 0]).at[:, 1].set(conv_prev[:, 1]).reshape(m, D_FF)
    x_out, u = _ffn_call(h2, x_mid, g2, pw['w_up'][l], pw['w_gate'][l], pw['w_down'][l],
                         pw['conv_w'][l], pw['conv_b'][l], m, ts, prev=(p1, p2))
    conv = u.reshape(nb, ts, D_FF)[:, t_real - (CONV_W - 1):t_real]
    return (x_out.reshape(nb, ts, D_MODEL),
            jnp.swapaxes(s_t, -1, -2),
            kv_new.reshape(nb, ts, 4, NSA_G, NSA_DH)[:, :t_real],
            win_new.reshape(nb, ts, 2, NSA_G, NSA_DH)[:, :t_real],
            conv)


def kernel(x_prompt, x_sample, cache_kv, cache_win, state_gla, state_conv, page_table, c_prompt, c_sample,
           norm1, norm2, w_ada, b_ada, w_in, w_gla_a, b_gla_a, gla_norm, q_norm, kc_norm, ks_norm, kw_norm,
           w_ck, w_cv, nsa_norm, w_out, w_up, w_gate, conv_w, conv_b, w_down):
    depth = w_in.shape[0]
    bp, bs, t_s = x_prompt.shape[0], x_sample.shape[0], x_sample.shape[1]
    assert CONV_W - 1 <= t_s <= SAMPLE_ROWS
    pw = _prep_weights({'w_in': w_in, 'w_gla_a': w_gla_a, 'b_gla_a': b_gla_a, 'w_ck': w_ck, 'w_cv': w_cv,
                        'norm1': norm1, 'norm2': norm2, 'gla_norm': gla_norm, 'nsa_norm': nsa_norm,
                        'q_norm': q_norm, 'kc_norm': kc_norm, 'ks_norm': ks_norm, 'kw_norm': kw_norm,
                        'w_out': w_out, 'w_up': w_up, 'w_gate': w_gate, 'w_down': w_down,
                        'conv_w': conv_w, 'conv_b': conv_b})
    bc = _round_up(bp + bs, 8)
    c_all = jnp.concatenate([c_prompt, c_sample, jnp.zeros((bc - bp - bs, D_MODEL), F32)], axis=0)
    mod_all = _mod_call(c_all, w_ada, b_ada)
    n_pool = cache_kv.shape[1]
    wb = cache_win.shape[2]
    ckv = cache_kv.reshape(depth, n_pool, PAGE_SIZE, 4 * KV_WIDTH)
    cwin = cache_win.reshape(depth, bs, wb, 2 * KV_WIDTH)
    xp = x_prompt
    xs = jnp.pad(x_sample, ((0, 0), (0, SAMPLE_ROWS - t_s), (0, 0)))
    outs_p, outs_s = [], []
    for l in range(depth):
        res = _layer_prompt(xp, mod_all[l, :bp], pw, l)
        xp = res[0]
        outs_p.append(res[1:])
        res = _layer_sample(xs, mod_all[l, bp:bp + bs], pw, l, t_s, ckv, cwin, page_table,
                            state_gla[l], state_conv[l])
        xs = res[0]
        outs_s.append(res[1:])
    stack = lambda outs, k: jnp.stack([o[k] for o in outs])
    win_s = jnp.stack([jnp.concatenate([cache_win[l][:, t_s:], outs_s[l][2]], axis=1) for l in range(depth)])
    return (xp, xs[:, :t_s], stack(outs_p, 1), stack(outs_s, 1), stack(outs_p, 2), win_s,
            stack(outs_p, 0), stack(outs_s, 0), stack(outs_p, 3), stack(outs_s, 3))
```

```python
import functools

import numpy as np
import jax
import jax.numpy as jnp
from jax import lax
from jax.experimental import pallas as pl
from jax.experimental.pallas import tpu as pltpu

F32 = jnp.float32
BF16 = jnp.bfloat16

D_MODEL = 1024
GLA_HEADS = 4
GLA_DK = 64
GLA_DV = 128
GLA_GATE_RANK = 16
GLA_TAU = 16.0
GLA_CHUNK = 16
GLA_WIDTH = GLA_HEADS * GLA_DV
NSA_HEADS = 8
NSA_G = 2
NSA_REP = NSA_HEADS // NSA_G
NSA_DH = 64
NSA_WIDTH = NSA_HEADS * NSA_DH
KV_WIDTH = NSA_G * NSA_DH
CMP_STRIDE = 16
CMP_LEN = 2 * CMP_STRIDE
SEL_BLOCK = 64
SEL_PER = SEL_BLOCK // CMP_STRIDE
SEL_TOP_N = 16
WINDOW = 512
D_FF = 2816
CONV_W = 3
RMS_EPS = 1e-6
BIG = 1e30
PAGE_SIZE = 128

IN_SIZES = (GLA_HEADS * GLA_DK, GLA_HEADS * GLA_DK, GLA_WIDTH, GLA_GATE_RANK, GLA_WIDTH,
            NSA_WIDTH, KV_WIDTH, KV_WIDTH, KV_WIDTH, KV_WIDTH, KV_WIDTH, KV_WIDTH, 3 * NSA_HEADS)

C_GQ, C_GK, C_GV, C_GR, C_NQ, C_KV, C_WIN, C_MISC = 0, 256, 512, 1024, 1536, 2048, 2560, 2816
MISC_GA, MISC_NG = 0, GLA_GATE_RANK
N_Z = 3072

VMEM_LIMIT_BYTES = 56 * 1024 * 1024
SLOPES = tuple(2.0 ** (-8.0 * h / NSA_HEADS) for h in range(1, NSA_HEADS + 1))


def _cparams(*sem):
    return pltpu.CompilerParams(dimension_semantics=sem, vmem_limit_bytes=VMEM_LIMIT_BYTES)


def _dot(a, b):
    return jnp.dot(a, b, preferred_element_type=F32)


def _dot_nt(a, b):
    return lax.dot_general(a, b, (((1,), (1,)), ((), ())), preferred_element_type=F32)


def _rms_rows(x, gain):
    return x * lax.rsqrt(jnp.mean(x * x, axis=-1, keepdims=True) + RMS_EPS) * gain


def _seg_rms64(x, gain):
    rows, width = x.shape
    lo_lane = (lax.broadcasted_iota(jnp.int32, (1, 128), 1) < 64)
    outs = []
    for c in range(width // 128):
        xb = x[:, c * 128:(c + 1) * 128]
        sq = xb * xb
        lo = jnp.sum(jnp.where(lo_lane, sq, 0.0), axis=-1, keepdims=True)
        hi = jnp.sum(jnp.where(lo_lane, 0.0, sq), axis=-1, keepdims=True)
        ms = jnp.where(lo_lane, lo, hi) * (1.0 / 64.0)
        outs.append(xb * lax.rsqrt(ms + RMS_EPS))
    y = outs[0] if len(outs) == 1 else jnp.concatenate(outs, axis=-1)
    return y * gain


def _mod_body(c_ref, w_ref, b_ref, o_ref):
    c = c_ref[...]
    a = c * jax.nn.sigmoid(c)
    o_ref[...] = _dot(a.astype(BF16), w_ref[...].astype(BF16)) + b_ref[...]


def _mod_call(c_all, w_ada, b_ada):
    depth, d, n = w_ada.shape
    bc = c_all.shape[0]
    tn = 1536
    return pl.pallas_call(
        _mod_body,
        grid=(depth, n // tn),
        in_specs=[pl.BlockSpec((bc, d), lambda l, j: (0, 0)),
                  pl.BlockSpec((None, d, tn), lambda l, j: (l, 0, j)),
                  pl.BlockSpec((None, 1, tn), lambda l, j: (l, 0, j))],
        out_specs=pl.BlockSpec((None, bc, tn), lambda l, j: (l, 0, j)),
        out_shape=jax.ShapeDtypeStruct((depth, bc, n), F32),
        compiler_params=_cparams("parallel", "parallel"),
        name="adaln_mod",
    )(c_all, w_ada, b_ada.reshape(depth, 1, n))


def _inproj_body(x_ref, g_ref, sc_ref, sh_ref, w_ref, z_ref, h_scr):
    @pl.when(pl.program_id(1) == 0)
    def _():
        y = _rms_rows(x_ref[...], g_ref[...])
        h_scr[...] = (y * (1.0 + sc_ref[...]) + sh_ref[...]).astype(BF16)

    z_ref[...] = _dot(h_scr[...], w_ref[...])


def _mod_spec(mod, tm, rows_per_batch):
    if mod.ndim == 3:
        return pl.BlockSpec((None, 1, D_MODEL), lambda i, *_: ((i * tm) // rows_per_batch, 0, 0))
    return pl.BlockSpec((tm, D_MODEL), lambda i, *_: (i, 0))


def _inproj_call(x2d, gain, sc, sh, w_bf, tm, rows_per_batch):
    m = x2d.shape[0]
    tn = 1024
    return pl.pallas_call(
        _inproj_body,
        grid=(m // tm, N_Z // tn),
        in_specs=[pl.BlockSpec((tm, D_MODEL), lambda i, j: (i, 0)),
                  pl.BlockSpec((1, D_MODEL), lambda i, j: (0, 0)),
                  _mod_spec(sc, tm, rows_per_batch),
                  _mod_spec(sh, tm, rows_per_batch),
                  pl.BlockSpec((D_MODEL, tn), lambda i, j: (0, j))],
        out_specs=pl.BlockSpec((tm, tn), lambda i, j: (i, j)),
        out_shape=jax.ShapeDtypeStruct((m, N_Z), F32),
        scratch_shapes=[pltpu.VMEM((tm, D_MODEL), BF16)],
        compiler_params=_cparams("parallel", "arbitrary"),
        name="inproj",
    )(x2d, gain, sc, sh, w_bf)


def _outproj_body(og_ref, on_ref, x_ref, g1_ref, w_ref, n2_ref, sc_ref, sh_ref, xo_ref, h2_ref):
    mix = (_dot(og_ref[...].astype(BF16), w_ref[:GLA_WIDTH, :])
           + _dot(on_ref[...].astype(BF16), w_ref[GLA_WIDTH:, :]))
    xn = x_ref[...] + g1_ref[...] * mix
    xo_ref[...] = xn
    y = _rms_rows(xn, n2_ref[...])
    h2_ref[...] = (y * (1.0 + sc_ref[...]) + sh_ref[...]).astype(BF16)


def _outproj_call(o_g, o_n, x2d, g1, w_bf, gain2, sc2, sh2, tm, rows_per_batch):
    m = x2d.shape[0]
    return pl.pallas_call(
        _outproj_body,
        grid=(m // tm,),
        in_specs=[pl.BlockSpec((tm, GLA_WIDTH), lambda i: (i, 0)),
                  pl.BlockSpec((tm, NSA_WIDTH), lambda i: (i, 0)),
                  pl.BlockSpec((tm, D_MODEL), lambda i: (i, 0)),
                  _mod_spec(g1, tm, rows_per_batch),
                  pl.BlockSpec((D_MODEL, D_MODEL), lambda i: (0, 0)),
                  pl.BlockSpec((1, D_MODEL), lambda i: (0, 0)),
                  _mod_spec(sc2, tm, rows_per_batch),
                  _mod_spec(sh2, tm, rows_per_batch)],
        out_specs=[pl.BlockSpec((tm, D_MODEL), lambda i: (i, 0)),
                   pl.BlockSpec((tm, D_MODEL), lambda i: (i, 0))],
        out_shape=[jax.ShapeDtypeStruct((m, D_MODEL), F32),
                   jax.ShapeDtypeStruct((m, D_MODEL), BF16)],
        compiler_params=_cparams("parallel"),
        name="outproj",
    )(o_g, o_n, x2d, g1, w_bf, gain2, sc2, sh2)


def _gelu_tanh(x):
    return x * (0.5 * (1.0 + jnp.tanh(0.7978845608028654 * (x + 0.044715 * (x * x * x)))))


def _ffn_body(*refs, tm, period, per_row_prev, keep):
    if per_row_prev:
        (h2_ref, x_ref, g2_ref, wu_ref, wg_ref, wd_ref, cw_ref, cb_ref, p1_ref, p2_ref,
         xo_ref, uk_ref, acc_scr) = refs
    else:
        (h2_ref, x_ref, g2_ref, wu_ref, wg_ref, wd_ref, cw_ref, cb_ref,
         xo_ref, uk_ref, acc_scr, carry_scr) = refs
    i, f = pl.program_id(0), pl.program_id(1)
    h2 = h2_ref[...]
    u = _dot(h2, wu_ref[...])
    gt = _dot(h2, wg_ref[...])
    row = lax.broadcasted_iota(jnp.int32, (tm, 1), 0)
    if per_row_prev:
        t = row % period
        m1, m2 = t >= 1, t >= 2
        prev1, prev2 = p1_ref[...], p2_ref[...]
    else:
        fresh = ((i * tm) % period) == 0
        c = jnp.where(fresh, 0.0, carry_scr[f])
        m1, m2 = row >= 1, row >= 2
        prev1 = c[7:8, :]
        prev2 = jnp.where(row == 0, c[6:7, :], c[7:8, :])
        carry_scr[f] = u[tm - 8:, :]
    u_m1 = jnp.where(m1, pltpu.roll(u, 1, 0), prev1)
    u_m2 = jnp.where(m2, pltpu.roll(u, 2, 0), prev2)
    conv = u_m2 * cw_ref[0:1, :] + u_m1 * cw_ref[1:2, :] + u * cw_ref[2:3, :] + cb_ref[...]
    act = _gelu_tanh(conv) * gt
    part = _dot(act.astype(BF16), wd_ref[...])
    uk_ref[...] = u[tm - keep:, :]

    @pl.when(f == 0)
    def _():
        acc_scr[...] = part

    @pl.when(f > 0)
    def _():
        acc_scr[...] += part

    @pl.when(f == pl.num_programs(1) - 1)
    def _():
        xo_ref[...] = x_ref[...] + g2_ref[...] * acc_scr[...]


def _ffn_call(h2, x2d, g2, wu, wg, wd, cw, cb, tm, period, prev=None):
    m = x2d.shape[0]
    tf = 256
    nf = D_FF // tf
    per_row_prev = prev is not None
    keep = tm if per_row_prev else 8
    body = functools.partial(_ffn_body, tm=tm, period=period, per_row_prev=per_row_prev, keep=keep)
    in_specs = [pl.BlockSpec((tm, D_MODEL), lambda i, f: (i, 0)),
                pl.BlockSpec((tm, D_MODEL), lambda i, f: (i, 0)),
                _mod_spec(g2, tm, period),
                pl.BlockSpec((D_MODEL, tf), lambda i, f: (0, f)),
                pl.BlockSpec((D_MODEL, tf), lambda i, f: (0, f)),
                pl.BlockSpec((tf, D_MODEL), lambda i, f: (f, 0)),
                pl.BlockSpec((CONV_W, tf), lambda i, f: (0, f)),
                pl.BlockSpec((1, tf), lambda i, f: (0, f))]
    args = [h2, x2d, g2, wu, wg, wd, cw, cb]
    scratch = [pltpu.VMEM((tm, D_MODEL), F32)]
    if per_row_prev:
        in_specs += [pl.BlockSpec((tm, tf), lambda i, f: (i, f))] * 2
        args += list(prev)
    else:
        scratch.append(pltpu.VMEM((nf, 8, tf), F32))
    return pl.pallas_call(
        body,
        grid=(m // tm, nf),
        in_specs=in_specs,
        out_specs=[pl.BlockSpec((tm, D_MODEL), lambda i, f: (i, 0)),
                   pl.BlockSpec((keep, tf), lambda i, f: (i, f))],
        out_shape=[jax.ShapeDtypeStruct((m, D_MODEL), F32),
                   jax.ShapeDtypeStruct((m // tm * keep, D_FF), F32)],
        scratch_shapes=scratch,
        compiler_params=_cparams("arbitrary", "arbitrary"),
        name="convffn",
    )(*args)


GLA_ROWS = 128


def _log_sigmoid(x):
    return jnp.minimum(x, 0.0) - jnp.log1p(jnp.exp(-jnp.abs(x)))


def _gla_body(gq_ref, gk_ref, gv_ref, gr_ref, misc_ref, wa_ref, ba_ref, gn_ref, s0_ref,
              o_ref, sT_out_ref, sT_scr, *, rows_in, t_valid):
    R, C = GLA_ROWS, GLA_CHUNK
    nchunk = R // C
    step = pl.program_id(1)

    @pl.when(step == 0)
    def _():
        for h in range(GLA_HEADS):
            sT_scr[h] = s0_ref[h].T

    def padded(ref):
        x = ref[...]
        if rows_in < R:
            x = jnp.concatenate([x, jnp.zeros((R - rows_in, x.shape[1]), x.dtype)], axis=0)
        return x

    row = lax.broadcasted_iota(jnp.int32, (R, 1), 0)
    live = row < t_valid
    q = padded(gq_ref) * (GLA_DK ** -0.5)
    k = jnp.where(live, padded(gk_ref), 0.0)
    v = jnp.where(live, padded(gv_ref), 0.0)
    xa = _dot(padded(misc_ref).astype(BF16), wa_ref[...]) + ba_ref[...]
    la = jnp.where(live, _log_sigmoid(xa) * (1.0 / GLA_TAU), 0.0)

    rc = row % C
    b = la
    shift = 1
    while shift < C:
        b = b + jnp.where(rc >= shift, pltpu.roll(b, shift, 0), 0.0)
        shift *= 2
    b3 = b.reshape(nchunk, C, GLA_HEADS * GLA_DK)
    b_last = jnp.broadcast_to(b3[:, C - 1:C, :], b3.shape).reshape(R, GLA_HEADS * GLA_DK)
    qe = (q * jnp.exp(b)).astype(BF16)
    ke = (k * jnp.exp(-b)).astype(BF16)
    kd = (k * jnp.exp(b_last - b)).astype(BF16)
    decay = jnp.exp(b_last)

    ri = lax.broadcasted_iota(jnp.int32, (R, R), 0)
    ci = lax.broadcasted_iota(jnp.int32, (R, R), 1)
    causal = (ri // C == ci // C) & (ci <= ri)
    col_chunk = lax.broadcasted_iota(jnp.int32, (GLA_DV, R), 1) // C

    outs = []
    for h in range(GLA_HEADS):
        ks = slice(h * GLA_DK, (h + 1) * GLA_DK)
        vs = slice(h * GLA_DV, (h + 1) * GLA_DV)
        v_h = v[:, vs]
        att = jnp.where(causal, _dot_nt(qe[:, ks], ke[:, ks]), 0.0)
        o_h = _dot(att.astype(BF16), v_h.astype(BF16))
        vT = v_h.T
        stack = jnp.concatenate([jnp.where(col_chunk == n, vT, 0.0) for n in range(nchunk)], axis=0)
        incT = _dot(stack.astype(BF16), kd[:, ks])
        sT = sT_scr[h]
        inter = []
        for n in range(nchunk):
            inter.append(_dot_nt(qe[n * C:(n + 1) * C, ks], sT.astype(BF16)))
            sT = decay[n * C + C - 1:n * C + C, ks] * sT + incT[n * GLA_DV:(n + 1) * GLA_DV, :]
        sT_scr[h] = sT
        o_h = o_h + jnp.concatenate(inter, axis=0)
        outs.append(_rms_rows(o_h, gn_ref[:, vs]))
    o = jnp.concatenate(outs, axis=-1)
    gr = padded(gr_ref)
    o = o * (gr * jax.nn.sigmoid(gr))
    o_ref[...] = o[:rows_in, :]

    @pl.when(step == pl.num_programs(1) - 1)
    def _():
        sT_out_ref[...] = sT_scr[...]


def _gla_call(z, wa_pad_bf, ba, gnorm, s0, n_batch, rows_per_batch, rows_in, t_valid):
    steps = rows_per_batch // rows_in
    body = functools.partial(_gla_body, rows_in=rows_in, t_valid=t_valid)

    def zspec(col, width):
        return pl.BlockSpec((rows_in, width), lambda b, s, c=col // width: (b * steps + s, c))

    return pl.pallas_call(
        body,
        grid=(n_batch, steps),
        in_specs=[zspec(C_GQ, 256), zspec(C_GK, 256), zspec(C_GV, 512), zspec(C_GR, 512),
                  zspec(C_MISC, 128),
                  pl.BlockSpec((128, GLA_HEADS * GLA_DK), lambda b, s: (0, 0)),
                  pl.BlockSpec((1, GLA_HEADS * GLA_DK), lambda b, s: (0, 0)),
                  pl.BlockSpec((1, GLA_WIDTH), lambda b, s: (0, 0)),
                  pl.BlockSpec((None, GLA_HEADS, GLA_DK, GLA_DV), lambda b, s: (b, 0, 0, 0))],
        out_specs=[pl.BlockSpec((rows_in, GLA_WIDTH), lambda b, s: (b * steps + s, 0)),
                   pl.BlockSpec((None, GLA_HEADS, GLA_DV, GLA_DK), lambda b, s: (b, 0, 0, 0))],
        out_shape=[jax.ShapeDtypeStruct((n_batch * rows_per_batch, GLA_WIDTH), F32),
                   jax.ShapeDtypeStruct((n_batch, GLA_HEADS, GLA_DV, GLA_DK), F32)],
        scratch_shapes=[pltpu.VMEM((GLA_HEADS, GLA_DV, GLA_DK), F32)],
        compiler_params=_cparams("parallel", "arbitrary"),
        name="gla",
    )(z, z, z, z, z, wa_pad_bf, ba, gnorm, s0)


def _nsa_prep_body(nq_ref, kv_ref, win_ref, qg_ref, ksg_ref, kwg_ref,
                   q_out, kv_out, win_out, kvs_bf, win_bf):
    q = _seg_rms64(nq_ref[...], qg_ref[...]) * (NSA_DH ** -0.5)
    q_out[...] = q.astype(BF16)
    kv = kv_ref[...]
    ksn = _seg_rms64(kv[:, 2 * KV_WIDTH:3 * KV_WIDTH], ksg_ref[...])
    vs = kv[:, 3 * KV_WIDTH:]
    kv_out[:, :2 * KV_WIDTH] = kv[:, :2 * KV_WIDTH]
    kv_out[:, 2 * KV_WIDTH:3 * KV_WIDTH] = ksn
    kv_out[:, 3 * KV_WIDTH:] = vs
    kvs_bf[:, :KV_WIDTH] = ksn.astype(BF16)
    kvs_bf[:, KV_WIDTH:] = vs.astype(BF16)
    w = win_ref[...]
    kwn = _seg_rms64(w[:, :KV_WIDTH], kwg_ref[...])
    win_out[:, :KV_WIDTH] = kwn
    win_out[:, KV_WIDTH:] = w[:, KV_WIDTH:]
    win_bf[:, :KV_WIDTH] = kwn.astype(BF16)
    win_bf[:, KV_WIDTH:] = w[:, KV_WIDTH:].astype(BF16)


def _nsa_prep_call(z, qg, ksg, kwg, tm):
    m = z.shape[0]

    def zspec(col, width):
        return pl.BlockSpec((tm, width), lambda i, c=col // width: (i, c))

    def ospec(width):
        return pl.BlockSpec((tm, width), lambda i: (i, 0))

    return pl.pallas_call(
        _nsa_prep_body,
        grid=(m // tm,),
        in_specs=[zspec(C_NQ, 512), zspec(C_KV, 512), zspec(C_WIN, 256),
                  pl.BlockSpec((1, 512), lambda i: (0, 0)),
                  pl.BlockSpec((1, 128), lambda i: (0, 0)),
                  pl.BlockSpec((1, 128), lambda i: (0, 0))],
        out_specs=[ospec(512), ospec(512), ospec(256), ospec(256), ospec(256)],
        out_shape=[jax.ShapeDtypeStruct((m, 512), BF16),
                   jax.ShapeDtypeStruct((m, 512), F32),
                   jax.ShapeDtypeStruct((m, 256), F32),
                   jax.ShapeDtypeStruct((m, 256), BF16),
                   jax.ShapeDtypeStruct((m, 256), BF16)],
        compiler_params=_cparams("parallel"),
        name="nsa_prep",
    )(z, z, z, qg, ksg, kwg)


def _pos_lanes(hi, lo, rows):
    lane = lax.broadcasted_iota(jnp.int32, (rows, 128), 1)
    return jnp.where(lane == 0, hi, jnp.where(lane == 1, lo, 0))


def _nsa_prep_t_body(nq_ref, kv_ref, win_ref, misc_ref, qg_ref, ksg_ref, kwg_ref,
                     qt_out, gt_out, kv_out, win_out, ksel_out, vselt_out, kwin_out, vwint_out,
                     *, tm, length):
    q = _seg_rms64(nq_ref[...], qg_ref[...]) * (NSA_DH ** -0.5)
    qt_out[...] = q.T.astype(BF16)
    gt_out[...] = jax.nn.sigmoid(misc_ref[...]).T[MISC_NG:MISC_NG + 32, :]
    kv = kv_ref[...]
    ksn = _seg_rms64(kv[:, 2 * KV_WIDTH:3 * KV_WIDTH], ksg_ref[...])
    vs = kv[:, 3 * KV_WIDTH:]
    kv_out[:, :2 * KV_WIDTH] = kv[:, :2 * KV_WIDTH]
    kv_out[:, 2 * KV_WIDTH:3 * KV_WIDTH] = ksn
    kv_out[:, 3 * KV_WIDTH:] = vs
    w = win_ref[...]
    kwn = _seg_rms64(w[:, :KV_WIDTH], kwg_ref[...])
    vw = w[:, KV_WIDTH:]
    win_out[:, :KV_WIDTH] = kwn
    win_out[:, KV_WIDTH:] = vw

    pos = (pl.program_id(0) * tm + lax.broadcasted_iota(jnp.int32, (tm, 1), 0)) % length
    seg = pos % ATT_TK
    pos_lanes = _pos_lanes(seg // SEL_BLOCK, seg % SEL_BLOCK, tm)
    lane = lax.broadcasted_iota(jnp.int32, (tm, 128), 1)
    block_lanes = jnp.where(lane - SEL_BLOCK == pos // SEL_BLOCK, 1, 0)
    ksel_out[:, :KV_WIDTH] = ksn.astype(BF16)
    ksel_out[:, KV_WIDTH:] = (pos_lanes + block_lanes).astype(BF16)
    kwin_out[:, :KV_WIDTH] = kwn.astype(BF16)
    kwin_out[:, KV_WIDTH:] = pos_lanes.astype(BF16)
    ones = jnp.where(lax.broadcasted_iota(jnp.int32, (NSA_DH, tm), 0) == 0, 1.0, 0.0)

    def values_t(v):
        vt = v.T
        return jnp.concatenate([vt[:NSA_DH], ones, vt[NSA_DH:], ones], axis=0).astype(BF16)

    vselt_out[...] = values_t(vs)
    vwint_out[...] = values_t(vw)


def _nsa_prep_t_call(z, qg, ksg, kwg, tm, length):
    m = z.shape[0]

    def zspec(col, width):
        return pl.BlockSpec((tm, width), lambda i, c=col // width: (i, c))

    def rows(width):
        return pl.BlockSpec((tm, width), lambda i: (i, 0))

    def cols(height):
        return pl.BlockSpec((height, tm), lambda i: (0, i))

    return pl.pallas_call(
        functools.partial(_nsa_prep_t_body, tm=tm, length=length),
        grid=(m // tm,),
        in_specs=[zspec(C_NQ, 512), zspec(C_KV, 512), zspec(C_WIN, 256), zspec(C_MISC, 128),
                  pl.BlockSpec((1, 512), lambda i: (0, 0)),
                  pl.BlockSpec((1, 128), lambda i: (0, 0)),
                  pl.BlockSpec((1, 128), lambda i: (0, 0))],
        out_specs=[cols(NSA_WIDTH), cols(32), rows(512), rows(256), rows(256), cols(256), rows(256), cols(256)],
        out_shape=[jax.ShapeDtypeStruct((NSA_WIDTH, m), BF16),
                   jax.ShapeDtypeStruct((32, m), F32),
                   jax.ShapeDtypeStruct((m, 512), F32),
                   jax.ShapeDtypeStruct((m, 256), F32),
                   jax.ShapeDtypeStruct((m, 256), BF16),
                   jax.ShapeDtypeStruct((256, m), BF16),
                   jax.ShapeDtypeStruct((m, 256), BF16),
                   jax.ShapeDtypeStruct((256, m), BF16)],
        compiler_params=_cparams("parallel"),
        name="nsa_prep_t",
    )(z, z, z, z, qg, ksg, kwg)


def _compress_rows(load_rows, w_ref, kind, nblk):
    acc = jnp.zeros((nblk, 2 * KV_WIDTH), F32)
    for j in range(CMP_STRIDE):
        acc = acc + _dot(load_rows(j).astype(BF16), w_ref[kind, j])
    row = lax.broadcasted_iota(jnp.int32, (nblk, 1), 0)
    bot_next = pltpu.roll(acc[:, KV_WIDTH:], nblk - 1, 0)
    return jnp.where(row < nblk - 1, acc[:, :KV_WIDTH] + bot_next, 0.0)


def _compress_body(krows_ref, vrows_ref, w_ref, g_ref, kc_out, vct_out, *, nblk):
    def loader(ref):
        return lambda j: ref[pl.ds(j, nblk, stride=CMP_STRIDE), :]

    kc = _compress_rows(loader(krows_ref), w_ref, 0, nblk)
    vc = _compress_rows(loader(vrows_ref), w_ref, 1, nblk)
    kc_out[:, :KV_WIDTH] = _seg_rms64(kc, g_ref[...]).astype(BF16)
    ec = CMP_STRIDE * lax.broadcasted_iota(jnp.int32, (nblk, 1), 0) + (CMP_LEN - 1)
    kc_out[:, KV_WIDTH:] = _pos_lanes(ec // SEL_BLOCK, ec % SEL_BLOCK, nblk).astype(BF16)
    vct_out[...] = vc.T.astype(BF16)


def _compress_call(kv_new3, w_cmp, kcg):
    nb, length, _ = kv_new3.shape
    nblk = length // CMP_STRIDE
    return pl.pallas_call(
        functools.partial(_compress_body, nblk=nblk),
        grid=(nb,),
        in_specs=[pl.BlockSpec((None, length, KV_WIDTH), lambda b: (b, 0, 0)),
                  pl.BlockSpec((None, length, KV_WIDTH), lambda b: (b, 0, 1)),
                  pl.BlockSpec((2, CMP_STRIDE, KV_WIDTH, 2 * KV_WIDTH), lambda b: (0, 0, 0, 0)),
                  pl.BlockSpec((1, KV_WIDTH), lambda b: (0, 0))],
        out_specs=[pl.BlockSpec((None, nblk, 2 * KV_WIDTH), lambda b: (b, 0, 0)),
                   pl.BlockSpec((None, KV_WIDTH, nblk), lambda b: (b, 0, 0))],
        out_shape=[jax.ShapeDtypeStruct((nb, nblk, 2 * KV_WIDTH), BF16),
                   jax.ShapeDtypeStruct((nb, KV_WIDTH, nblk), BF16)],
        compiler_params=_cparams("parallel"),
        name="nsa_compress",
    )(kv_new3, kv_new3, w_cmp, kcg)


def _round_up(a, b):
    return -(-a // b) * b


def _tile_rows(a, n):
    return jnp.concatenate([a] * n, axis=0)


def _softmax_rows(s, valid):
    s = jnp.where(valid, s, -BIG)
    m = jnp.max(s, axis=-1, keepdims=True)
    e = jnp.where(valid, jnp.exp(s - m), 0.0)
    l = jnp.sum(e, axis=-1, keepdims=True)
    return e / jnp.where(l > 0.0, l, 1.0)


def _sel_rows(ns):
    return _round_up(ns, 8)


def _imp_scratch_rows(ns):
    return 8 + SEL_PER * _sel_rows(ns) + 8


def _select_blocks_t(imp_t, imp_scr, sc_scr, t_row, ns, n_sel, n_rank=None):
    nb, tq = imp_t.shape
    nsr = _sel_rows(ns)
    rows = _imp_scratch_rows(ns)
    imp_scr[0:8, :] = jnp.zeros((8, tq), F32)
    imp_scr[8 + nb:rows, :] = jnp.zeros((rows - 8 - nb, tq), F32)
    imp_scr[8:8 + nb, :] = imp_t

    def ld(off):
        return imp_scr[pl.ds(8 + off, nsr, stride=SEL_PER), :]

    p_slc = ld(-1) + 2.0 * (ld(0) + ld(1) + ld(2)) + ld(3)
    j = lax.broadcasted_iota(jnp.int32, (nsr, 1), 0)
    cur = t_row // SEL_BLOCK
    forced = (j == 0) | (j == cur) | (j == cur - 1)
    score = jnp.where(forced, BIG, jnp.where(j <= cur, p_slc, -BIG))
    sc_scr[...] = score

    def rank_body(k, cnt):
        sk = sc_scr[pl.ds(k, 1), :]
        beats = (sk > score) | ((sk == score) & (k < j))
        return cnt + jnp.where(beats, 1.0, 0.0)

    cnt = lax.fori_loop(0, ns if n_rank is None else n_rank, rank_body, jnp.zeros((nsr, tq), F32))
    return jnp.where((cnt < float(n_sel)) & (j < ns), 1.0, 0.0)


def _select_blocks(imp, imp_scr, sc_scr, t_row, ns, n_sel):
    sel_t = _select_blocks_t(imp.T, imp_scr, sc_scr, t_row, ns, n_sel)
    nsp = _round_up(ns, 128)
    if nsp > sel_t.shape[0]:
        sel_t = jnp.concatenate([sel_t, jnp.zeros((nsp - sel_t.shape[0], sel_t.shape[1]), F32)], axis=0)
    return sel_t.T.astype(BF16)


def _online_update(carry, s, valid, pv):
    m, l, acc = carry
    s = jnp.where(valid, s, -BIG)
    m_new = jnp.maximum(m, jnp.max(s, axis=-1, keepdims=True))
    alpha = jnp.exp(m - m_new)
    e = jnp.where(valid, jnp.exp(s - m_new), 0.0)
    l = alpha * l + jnp.sum(e, axis=-1, keepdims=True)
    acc = alpha * acc + pv(e.astype(BF16))
    return m_new, l, acc


def _online_init(rows, width):
    return (jnp.full((rows, 1), -BIG, F32), jnp.zeros((rows, 1), F32), jnp.zeros((rows, width), F32))


def _online_finish(carry):
    _, l, acc = carry
    return acc / jnp.where(l > 0.0, l, 1.0)


NEG_MASK = -(2.0 ** 100)
CMP_TQ = 128
ATT_TQ = 256
ATT_TK = 512


def _slope_lanes(g, tq):
    r = lax.broadcasted_iota(jnp.int32, (1, NSA_REP * tq), 1) // tq
    out = jnp.full((1, NSA_REP * tq), SLOPES[g * NSA_REP], F32)
    for k in range(1, NSA_REP):
        out = jnp.where(r == k, SLOPES[g * NSA_REP + k], out)
    return out


def _q_aug_t(qt_ref, g, tq, qsel_t):
    lanes = NSA_REP * tq
    qrows = jnp.concatenate(
        [qt_ref[(g * NSA_REP + r) * NSA_DH:(g * NSA_REP + r + 1) * NSA_DH, :] for r in range(NSA_REP)], axis=1)
    zero = jnp.zeros((NSA_DH, lanes), BF16)
    row = lax.broadcasted_iota(jnp.int32, (NSA_DH, lanes), 0)
    slope = _slope_lanes(g, tq)
    alibi = jnp.where(row == 0, slope * float(SEL_BLOCK), jnp.where(row == 1, slope, 0.0)).astype(BF16)
    sel = zero if qsel_t is None else jnp.concatenate([qsel_t] * NSA_REP, axis=1)
    groups = [qrows, zero] if g == 0 else [zero, qrows]
    return jnp.concatenate(groups + [alibi, sel], axis=0)


def _tile_lanes(a, n):
    return jnp.concatenate([a] * n, axis=1)


def _cmp_select_body(qt_ref, kc_ref, vct_ref, oct_ref, qselt_ref, cnt_ref, imp_scr, sc_scr, *, length, tq):
    nb = length // CMP_STRIDE
    ns = length // SEL_BLOCK
    nsr = _sel_rows(ns)
    n_sel = min(SEL_TOP_N, ns)
    t0 = pl.program_id(1) * tq
    t_row = t0 + lax.broadcasted_iota(jnp.int32, (1, tq), 1)
    ec = CMP_STRIDE * lax.broadcasted_iota(jnp.int32, (nb, 1), 0) + (CMP_LEN - 1)
    hidden = _tile_lanes(jnp.where(t_row >= ec, 0.0, NEG_MASK), NSA_REP)
    counts = []
    for g in range(NSA_G):
        s_t = _dot(kc_ref[...], _q_aug_t(qt_ref, g, tq, None)) + hidden
        m = jnp.max(s_t, axis=0, keepdims=True)
        e = jnp.where(hidden < 0.0, 0.0, jnp.exp(s_t - m))
        l = jnp.sum(e, axis=0, keepdims=True)
        p_t = e * jnp.where(l > 0.0, 1.0 / l, 0.0)
        o_ct = _dot(vct_ref[g * NSA_DH:(g + 1) * NSA_DH, :], p_t.astype(BF16))
        imp_t = p_t[:, 0:tq]
        for r in range(1, NSA_REP):
            imp_t = imp_t + p_t[:, r * tq:(r + 1) * tq]
        n_rank = jnp.minimum((t0 + tq - 1) // SEL_BLOCK + 1, ns)
        sel_t = _select_blocks_t(imp_t, imp_scr, sc_scr, t_row, ns, n_sel, n_rank)
        qsel = jnp.where(sel_t > 0.5, 0.0, NEG_MASK)
        if nsr < SEL_BLOCK:
            qsel = jnp.concatenate([qsel, jnp.full((SEL_BLOCK - nsr, tq), NEG_MASK, F32)], axis=0)
        qselt_ref[g * SEL_BLOCK:(g + 1) * SEL_BLOCK, :] = qsel.astype(BF16)
        cnt = _dot_nt(jnp.ones((8, tq), BF16), sel_t.astype(BF16))
        if nsr < SEL_BLOCK:
            cnt = jnp.concatenate([cnt, jnp.zeros((8, SEL_BLOCK - nsr), F32)], axis=1)
        counts.append(cnt)
        for r in range(NSA_REP):
            h = g * NSA_REP + r
            oct_ref[h * NSA_DH:(h + 1) * NSA_DH, :] = o_ct[:, r * tq:(r + 1) * tq]
    cnt_ref[...] = jnp.concatenate(counts, axis=1)


def _cmp_select_call(q_t, kc_aug, vc_t, n_batch, length):
    tq = CMP_TQ
    nq = length // tq
    nb = length // CMP_STRIDE
    ns = length // SEL_BLOCK
    assert ns <= SEL_BLOCK
    m = n_batch * length
    return pl.pallas_call(
        functools.partial(_cmp_select_body, length=length, tq=tq),
        grid=(n_batch, nq),
        in_specs=[pl.BlockSpec((NSA_WIDTH, tq), lambda b, i: (0, b * nq + i)),
                  pl.BlockSpec((None, nb, 2 * KV_WIDTH), lambda b, i: (b, 0, 0)),
                  pl.BlockSpec((None, KV_WIDTH, nb), lambda b, i: (b, 0, 0))],
        out_specs=[pl.BlockSpec((NSA_WIDTH, tq), lambda b, i: (0, b * nq + i)),
                   pl.BlockSpec((NSA_G * SEL_BLOCK, tq), lambda b, i: (0, b * nq + i)),
                   pl.BlockSpec((8, NSA_G * SEL_BLOCK), lambda b, i: (b * nq + i, 0))],
        out_shape=[jax.ShapeDtypeStruct((NSA_WIDTH, m), F32),
                   jax.ShapeDtypeStruct((NSA_G * SEL_BLOCK, m), BF16),
                   jax.ShapeDtypeStruct((n_batch * nq * 8, NSA_G * SEL_BLOCK), F32)],
        scratch_shapes=[pltpu.VMEM((_imp_scratch_rows(ns), tq), F32), pltpu.VMEM((_sel_rows(ns), tq), F32)],
        compiler_params=_cparams("parallel", "parallel"),
        name="nsa_cmp_select",
    )(q_t, kc_aug, vc_t)


def _nsa_attn_body(flags_ref, qt_ref, qselt_ref, gt_ref, oct_ref, ksel_ref, vselt_ref, kwin_ref, vwint_ref,
                   nn_ref, o_ref, m_scr, acc_scr, on_scr, *, length, tq, tk):
    b, i = pl.program_id(0), pl.program_id(1)
    nq = pl.num_programs(1)
    n_kt_all = length // tk
    n_win = WINDOW // tq
    t0 = i * tq
    t_row = t0 + lax.broadcasted_iota(jnp.int32, (1, tq), 1)
    for g in range(NSA_G):
        qa = _q_aug_t(qt_ref, g, tq, qselt_ref[g * SEL_BLOCK:(g + 1) * SEL_BLOCK, :])
        slope = _slope_lanes(g, tq)
        vrows = slice(g * 2 * NSA_DH, (g + 1) * 2 * NSA_DH)

        def reset():
            m_scr[...] = jnp.full(m_scr.shape, -BIG, F32)
            acc_scr[...] = jnp.zeros(acc_scr.shape, F32)

        def step(k_ref, vt_ref, k0, rows, mask):
            s_t = _dot(k_ref[pl.ds(k0, rows), :], qa)
            if mask is not None:
                kpos = k0 + lax.broadcasted_iota(jnp.int32, (rows, 1), 0)
                ok = (kpos <= t_row) if mask == 'causal' else (kpos >= t_row - WINDOW)
                s_t = s_t + _tile_lanes(jnp.where(ok, 0.0, NEG_MASK), NSA_REP)
            c = slope * ((k0 // ATT_TK) * ATT_TK - t0).astype(F32)
            m_old = m_scr[...]
            m_new = jnp.maximum(m_old, jnp.max(s_t, axis=0, keepdims=True) + c)
            e = jnp.exp(s_t - (m_new - c)).astype(BF16)
            acc_scr[...] = jnp.exp(m_old - m_new) * acc_scr[...] + _dot(vt_ref[vrows, pl.ds(k0, rows)], e)
            m_scr[...] = m_new

        def finish():
            acc = acc_scr[...]
            l = acc[NSA_DH:NSA_DH + 1, :]
            return acc[0:NSA_DH, :] / jnp.where(l > 0.0, l, 1.0)

        reset()
        last = (t0 + tq + tk - 1) // tk - 1
        fbase = ((b * nq + i) * NSA_G + g) * n_kt_all

        def sel_tile(kt, carry):
            @pl.when(flags_ref[fbase + kt] != 0)
            def _():
                step(ksel_ref, vselt_ref, pl.multiple_of(kt * tk, tk), tk, None)
            return carry

        lax.fori_loop(0, last, sel_tile, 0)
        step(ksel_ref, vselt_ref, pl.multiple_of(last * tk, tk), tk, 'causal')
        o_s = finish()

        reset()

        @pl.when(i >= n_win)
        def _():
            step(kwin_ref, vwint_ref, pl.multiple_of((i - n_win) * tq, tq), tq, 'window')

        def win_tile(w, carry):
            step(kwin_ref, vwint_ref, pl.multiple_of(w * tq, tq), tq, None)
            return carry

        lax.fori_loop(jnp.maximum(i - n_win + 1, 0), i, win_tile, 0)
        step(kwin_ref, vwint_ref, pl.multiple_of(t0, tq), tq, 'causal')
        o_w = finish()

        for r in range(NSA_REP):
            h = g * NSA_REP + r
            hs = slice(h * NSA_DH, (h + 1) * NSA_DH)
            ls = slice(r * tq, (r + 1) * tq)
            on_scr[hs, :] = (gt_ref[h:h + 1, :] * oct_ref[hs, :]
                             + gt_ref[NSA_HEADS + h:NSA_HEADS + h + 1, :] * o_s[:, ls]
                             + gt_ref[2 * NSA_HEADS + h:2 * NSA_HEADS + h + 1, :] * o_w[:, ls])
    on = on_scr[...]
    ms = jnp.sum(on * on, axis=0, keepdims=True) * (1.0 / NSA_WIDTH)
    o_ref[...] = (on * lax.rsqrt(ms + RMS_EPS) * nn_ref[...]).T


def _nsa_attn_call(flags, q_t, qsel_t, g_t, oc_t, ksel, vsel_t, kwin, vwin_t, nn_col, n_batch, length):
    tq, tk = ATT_TQ, ATT_TK
    nq = length // tq
    assert WINDOW % tq == 0 and length % tk == 0 and tk % tq == 0
    body = functools.partial(_nsa_attn_body, length=length, tq=tq, tk=tk)

    def cols(rows):
        return pl.BlockSpec((rows, tq), lambda b, i, fl: (0, b * nq + i))

    def keys():
        return pl.BlockSpec((length, 2 * KV_WIDTH), lambda b, i, fl: (b, 0))

    def values():
        return pl.BlockSpec((2 * KV_WIDTH, length), lambda b, i, fl: (0, b))

    grid_spec = pltpu.PrefetchScalarGridSpec(
        num_scalar_prefetch=1,
        grid=(n_batch, nq),
        in_specs=[cols(NSA_WIDTH), cols(NSA_G * SEL_BLOCK), cols(32), cols(NSA_WIDTH),
                  keys(), values(), keys(), values(),
                  pl.BlockSpec((NSA_WIDTH, 1), lambda b, i, fl: (0, 0))],
        out_specs=pl.BlockSpec((tq, NSA_WIDTH), lambda b, i, fl: (b * nq + i, 0)),
        scratch_shapes=[pltpu.VMEM((1, NSA_REP * tq), F32),
                        pltpu.VMEM((2 * NSA_DH, NSA_REP * tq), F32),
                        pltpu.VMEM((NSA_WIDTH, tq), F32)])
    return pl.pallas_call(
        body,
        grid_spec=grid_spec,
        out_shape=jax.ShapeDtypeStruct((n_batch * length, NSA_WIDTH), F32),
        compiler_params=_cparams("parallel", "parallel"),
        name="nsa_attn",
    )(flags, q_t, qsel_t, g_t, oc_t, ksel, vsel_t, kwin, vwin_t, nn_col)


def _prep_weights(p):
    depth = p['w_in'].shape[0]
    offs = np.concatenate([[0], np.cumsum(IN_SIZES)])
    piece = lambda i: p['w_in'][:, :, int(offs[i]):int(offs[i + 1])]
    order = [0, 1, 2, 4, 5, 6, 7, 8, 9, 10, 11, 3, 12]
    cols = [piece(i) for i in order]
    used = sum(IN_SIZES)
    cols.append(jnp.zeros((depth, D_MODEL, N_Z - used), F32))
    w_in = jnp.concatenate(cols, axis=-1).astype(BF16)
    wa = jnp.zeros((depth, 128, GLA_HEADS * GLA_DK), F32).at[:, :GLA_GATE_RANK, :].set(p['w_gla_a']).astype(BF16)

    def cmp_weights(w):
        eye = jnp.eye(NSA_G, dtype=F32)
        bd = jnp.einsum('gh,ljde->ljgdhe', eye, w).reshape(depth, CMP_LEN, KV_WIDTH, KV_WIDTH)
        return jnp.concatenate([bd[:, :CMP_STRIDE], bd[:, CMP_STRIDE:]], axis=-1)

    w_cmp = jnp.stack([cmp_weights(p['w_ck']), cmp_weights(p['w_cv'])], axis=1).astype(BF16)
    tile = lambda a, n: jnp.tile(a, (1, n))[:, None, :]
    return {
        'w_in': w_in, 'wa': wa, 'ba': p['b_gla_a'][:, None, :], 'w_cmp': w_cmp,
        'norm1': p['norm1'][:, None, :], 'norm2': p['norm2'][:, None, :],
        'gla_norm': p['gla_norm'][:, None, :], 'nsa_norm': p['nsa_norm'][:, None, :],
        'q_norm': tile(p['q_norm'], NSA_HEADS), 'kc_norm': tile(p['kc_norm'], NSA_G),
        'ks_norm': tile(p['ks_norm'], NSA_G), 'kw_norm': tile(p['kw_norm'], NSA_G),
        'w_out': p['w_out'].astype(BF16), 'w_up': p['w_up'].astype(BF16),
        'w_gate': p['w_gate'].astype(BF16), 'w_down': p['w_down'].astype(BF16),
        'conv_w': p['conv_w'], 'conv_b': p['conv_b'][:, None, :],
    }


def _split_mod(mod, shape):
    return [mod[:, k * D_MODEL:(k + 1) * D_MODEL].reshape(shape) for k in range(6)]


def _layer_prompt(x, mod, pw, l):
    nb, length, _ = x.shape
    m = nb * length
    sh1, sc1, g1, sh2, sc2, g2 = _split_mod(mod, (nb, 1, D_MODEL))
    x2 = x.reshape(m, D_MODEL)
    tm = min(512, length)
    z = _inproj_call(x2, pw['norm1'][l], sc1, sh1, pw['w_in'][l], min(1024, length), length)
    s0 = jnp.zeros((nb, GLA_HEADS, GLA_DK, GLA_DV), F32)
    o_g, s_t = _gla_call(z, pw['wa'][l], pw['ba'][l], pw['gla_norm'][l], s0, nb, length, GLA_ROWS, length)
    q_t, g_t, kv_new, win_new, ksel, vsel_t, kwin, vwin_t = _nsa_prep_t_call(
        z, pw['q_norm'][l], pw['ks_norm'][l], pw['kw_norm'][l], tm, length)
    kc_aug, vc_t = _compress_call(kv_new.reshape(nb, length, 4 * KV_WIDTH), pw['w_cmp'][l], pw['kc_norm'][l])
    oc_t, qsel_t, cnt = _cmp_select_call(q_t, kc_aug, vc_t, nb, length)
    per_tile = ATT_TK // SEL_BLOCK
    flags = cnt.reshape(-1, ATT_TQ // CMP_TQ, 8, NSA_G, SEL_BLOCK // per_tile, per_tile)
    flags = jnp.sum(flags[:, :, 0, :, :length // ATT_TK], axis=(1, -1))
    flags = (flags > 0.0).astype(jnp.int32).reshape(-1)
    o_n = _nsa_attn_call(flags, q_t, qsel_t, g_t, oc_t, ksel, vsel_t, kwin, vwin_t,
                         pw['nsa_norm'][l].reshape(NSA_WIDTH, 1), nb, length)
    x_mid, h2 = _outproj_call(o_g, o_n, x2, g1, pw['w_out'][l], pw['norm2'][l], sc2, sh2, tm, length)
    tm_f = min(1024, length)
    x_out, u_keep = _ffn_call(h2, x_mid, g2, pw['w_up'][l], pw['w_gate'][l], pw['w_down'][l],
                              pw['conv_w'][l], pw['conv_b'][l], tm_f, length)
    conv = u_keep.reshape(nb, length // tm_f, 8, D_FF)[:, -1, 8 - (CONV_W - 1):, :]
    wl = min(WINDOW, length)
    return (x_out.reshape(nb, length, D_MODEL),
            jnp.swapaxes(s_t, -1, -2),
            kv_new.reshape(nb, length, 4, NSA_G, NSA_DH),
            win_new.reshape(nb, length, 2, NSA_G, NSA_DH)[:, length - wl:],
            conv)


SAMPLE_ROWS = 16


def _page_dma(pt_ref, ckv_hbm, buf, sem, layer, batch, slot, row0, n_pages, start):
    rows = buf.shape[1]

    def body(p, carry):
        cp = pltpu.make_async_copy(
            ckv_hbm.at[layer, pt_ref[batch, p], pl.ds(row0, rows), :],
            buf.at[slot, :, pl.ds(pl.multiple_of(p * PAGE_SIZE, PAGE_SIZE), PAGE_SIZE)],
            sem.at[slot])
        if start:
            cp.start()
        else:
            cp.wait()
        return carry

    lax.fori_loop(0, n_pages, body, 0)


def _gather_pages(pt_ref, ckv_hbm, buf, sem, layer, row0, n_pages):
    b = pl.program_id(0)
    dma = functools.partial(_page_dma, pt_ref, ckv_hbm, buf, sem, layer)

    @pl.when(b == 0)
    def _():
        dma(0, 0, row0, n_pages, True)

    @pl.when(b + 1 < pl.num_programs(0))
    def _():
        dma(b + 1, (b + 1) % 2, row0, n_pages, True)

    slot = b % 2
    dma(b, slot, row0, n_pages, False)
    return slot


def _q_stack(q_ref, g):
    return jnp.concatenate(
        [q_ref[:, (g * NSA_REP + r) * NSA_DH:(g * NSA_REP + r + 1) * NSA_DH] for r in range(NSA_REP)], axis=0)


def _slope_rows(g, rows):
    return jnp.concatenate([jnp.full((rows, 1), SLOPES[g * NSA_REP + r], F32) for r in range(NSA_REP)], axis=0)


def _s1_body(pt_ref, ckv_hbm, q_ref, w_ref, g_ref, oc_ref, sel_ref, buf, sem, rows_scr, imp_scr, sc_scr,
             *, layer, n_pages, past_len):
    slot = _gather_pages(pt_ref, ckv_hbm, buf, sem, layer, 0, n_pages)
    ts = SAMPLE_ROWS
    nblk = past_len // CMP_STRIDE
    ns = past_len // SEL_BLOCK + 1
    nsp = _round_up(ns, 128)
    n_sel = min(SEL_TOP_N, ns)
    chunk = min(512, past_len)
    for kind in range(2):
        for c in range(past_len // chunk):
            rows_scr[kind, c * chunk:(c + 1) * chunk, :] = (
                buf[slot, kind * KV_WIDTH:(kind + 1) * KV_WIDTH, c * chunk:(c + 1) * chunk].T)

    def loader(kind):
        return lambda j: rows_scr[kind, pl.ds(j, nblk, stride=CMP_STRIDE), :]

    kc = _seg_rms64(_compress_rows(loader(0), w_ref, 0, nblk), g_ref[...]).astype(BF16)
    vc = _compress_rows(loader(1), w_ref, 1, nblk).astype(BF16)

    t_col = past_len + lax.broadcasted_iota(jnp.int32, (ts, 1), 0)
    ec = CMP_STRIDE * lax.broadcasted_iota(jnp.int32, (1, nblk), 1) + (CMP_LEN - 1)
    dist = t_col - ec
    valid4 = _tile_rows(dist >= 0, NSA_REP)
    distf4 = _tile_rows(dist.astype(F32), NSA_REP)
    heads, imps = [], []
    for g in range(NSA_G):
        ks = slice(g * NSA_DH, (g + 1) * NSA_DH)
        s = _dot_nt(_q_stack(q_ref, g), kc[:, ks]) - _slope_rows(g, ts) * distf4
        p = _softmax_rows(s, valid4)
        o_c = _dot(p.astype(BF16), vc[:, ks])
        imp = p[0:ts]
        for r in range(1, NSA_REP):
            imp = imp + p[r * ts:(r + 1) * ts]
        imps.append(imp)
        heads += [o_c[r * ts:(r + 1) * ts] for r in range(NSA_REP)]
    oc_ref[...] = jnp.concatenate(heads, axis=-1)
    imp = jnp.concatenate(imps + [jnp.zeros((128 - NSA_G * ts, nblk), F32)], axis=0)
    t_row = past_len + lax.broadcasted_iota(jnp.int32, (1, 128), 1) % ts
    sel = _select_blocks(imp, imp_scr, sc_scr, t_row, ns, n_sel)
    for g in range(NSA_G):
        sel_ref[:, g * nsp:(g + 1) * nsp] = sel[g * ts:(g + 1) * ts, :].astype(F32)


def _s1_call(page_table, ckv, q_bf, w_cmp, kcg, layer, past_len):
    nbatch, n_pages = page_table.shape
    ts = SAMPLE_ROWS
    ns = past_len // SEL_BLOCK + 1
    nsp = _round_up(ns, 128)
    body = functools.partial(_s1_body, layer=layer, n_pages=n_pages, past_len=past_len)
    grid_spec = pltpu.PrefetchScalarGridSpec(
        num_scalar_prefetch=1,
        grid=(nbatch,),
        in_specs=[pl.BlockSpec(memory_space=pl.ANY),
                  pl.BlockSpec((ts, NSA_WIDTH), lambda b, pt: (b, 0)),
                  pl.BlockSpec((2, CMP_STRIDE, KV_WIDTH, 2 * KV_WIDTH), lambda b, pt: (0, 0, 0, 0)),
                  pl.BlockSpec((1, KV_WIDTH), lambda b, pt: (0, 0))],
        out_specs=[pl.BlockSpec((ts, NSA_WIDTH), lambda b, pt: (b, 0)),
                   pl.BlockSpec((ts, NSA_G * nsp), lambda b, pt: (b, 0))],
        scratch_shapes=[pltpu.VMEM((2, 2 * KV_WIDTH, past_len), F32),
                        pltpu.SemaphoreType.DMA((2,)),
                        pltpu.VMEM((2, past_len, KV_WIDTH), F32),
                        pltpu.VMEM((_imp_scratch_rows(ns), 128), F32),
                        pltpu.VMEM((_sel_rows(ns), 128), F32)])
    return pl.pallas_call(
        body,
        grid_spec=grid_spec,
        out_shape=[jax.ShapeDtypeStruct((nbatch * ts, NSA_WIDTH), F32),
                   jax.ShapeDtypeStruct((nbatch * ts, NSA_G * nsp), F32)],
        compiler_params=_cparams("arbitrary"),
        name="nsa_sample_select",
    )(page_table, ckv, q_bf, w_cmp, kcg)


def _s2_body(pt_ref, ckv_hbm, q_ref, misc_ref, oc_ref, sel_ref, kvn_ref, cwin_ref, wn_ref, e_ref, nn_ref,
             o_ref, buf, sem, *, layer, n_pages, past_len, wb, tk):
    slot = _gather_pages(pt_ref, ckv_hbm, buf, sem, layer, 2 * KV_WIDTH, n_pages)
    ts = SAMPLE_ROWS
    rows = NSA_HEADS * ts
    ns = past_len // SEL_BLOCK + 1
    nsp = sel_ref.shape[1] // NSA_G
    gates = jax.nn.sigmoid(misc_ref[:, MISC_NG:MISC_NG + 3 * NSA_HEADS])
    half = lax.broadcasted_iota(jnp.int32, (ts, KV_WIDTH), 1) // NSA_DH
    q_rows, slopes, sel_rows = [], [], []
    for h in range(NSA_HEADS):
        g = h // NSA_REP
        pair = q_ref[:, (h // 2) * KV_WIDTH:(h // 2 + 1) * KV_WIDTH].astype(F32)
        if h % 2 != g:
            pair = pltpu.roll(pair, NSA_DH, 1)
        q_rows.append(jnp.where(half == g, pair, 0.0))
        slopes.append(jnp.full((ts, 1), SLOPES[h], F32))
        sel_rows.append(sel_ref[:, g * nsp:(g + 1) * nsp])
    q_all = jnp.concatenate(q_rows, axis=0).astype(BF16)
    slope_rows = jnp.concatenate(slopes, axis=0)
    sel_all = jnp.concatenate(sel_rows, axis=0)
    sel_bf = sel_all.astype(BF16)
    t_col = past_len + lax.broadcasted_iota(jnp.int32, (rows, 1), 0) % ts
    zpad = jnp.zeros((128 - ts, KV_WIDTH), BF16)

    def rel(k0, n):
        return k0 - past_len + lax.broadcasted_iota(jnp.int32, (1, n), 1)

    def scores(keys_t, r):
        return _dot(q_all, keys_t) + slope_rows * r.astype(F32)

    def scores_new(ref):
        keys = jnp.concatenate([ref[:, :KV_WIDTH], zpad], axis=0)
        return _dot_nt(q_all, keys) + slope_rows * rel(past_len, 128).astype(F32)

    def pv_new(ref):
        vals = jnp.concatenate([ref[:, KV_WIDTH:], zpad], axis=0)
        return lambda e: _dot(e, vals)

    d_new = t_col - (past_len + lax.broadcasted_iota(jnp.int32, (1, 128), 1))

    def kt_body(kt, carry):
        k0 = pl.multiple_of(kt * tk, tk)
        valid = _dot(sel_bf, e_ref[:, pl.ds(k0, tk)]) > 0.5
        v_t = buf[slot, KV_WIDTH:, pl.ds(k0, tk)].astype(BF16)
        s = scores(buf[slot, :KV_WIDTH, pl.ds(k0, tk)].astype(BF16), rel(k0, tk))
        return _online_update(carry, s, valid, lambda e: _dot_nt(e, v_t))

    carry = lax.fori_loop(0, past_len // tk, kt_body, _online_init(rows, KV_WIDTH))
    valid = (sel_all[:, ns - 1:ns] > 0.5) & (d_new >= 0)
    o_s = _online_finish(_online_update(carry, scores_new(kvn_ref), valid, pv_new(kvn_ref)))

    d_win = t_col - (past_len - wb + lax.broadcasted_iota(jnp.int32, (1, wb), 1))
    vw_t = cwin_ref[KV_WIDTH:, :].astype(BF16)
    carry = _online_update(_online_init(rows, KV_WIDTH),
                           scores(cwin_ref[:KV_WIDTH, :].astype(BF16), rel(past_len - wb, wb)),
                           (d_win >= 0) & (d_win <= WINDOW), lambda e: _dot_nt(e, vw_t))
    valid = (d_new >= 0) & (d_new <= WINDOW)
    o_w = _online_finish(_online_update(carry, scores_new(wn_ref), valid, pv_new(wn_ref)))

    heads = []
    for h in range(NSA_HEADS):
        g = h // NSA_REP
        rs = slice(h * ts, (h + 1) * ts)
        ls = slice(g * NSA_DH, (g + 1) * NSA_DH)
        heads.append(gates[:, h:h + 1] * oc_ref[:, h * NSA_DH:(h + 1) * NSA_DH]
                     + gates[:, NSA_HEADS + h:NSA_HEADS + h + 1] * o_s[rs, ls]
                     + gates[:, 2 * NSA_HEADS + h:2 * NSA_HEADS + h + 1] * o_w[rs, ls])
    o_ref[...] = _rms_rows(jnp.concatenate(heads, axis=-1), nn_ref[...])


def _s2_call(page_table, ckv_t, q_bf, z, o_c, sel, kvs_bf, cwin_t, win_bf, nn, layer, past_len):
    nbatch, n_pages = page_table.shape
    ts = SAMPLE_ROWS
    wb = cwin_t.shape[3]
    tk = min(2048, past_len)
    nsp = sel.shape[1] // NSA_G
    body = functools.partial(_s2_body, layer=layer, n_pages=n_pages, past_len=past_len, wb=wb, tk=tk)
    blk = lax.broadcasted_iota(jnp.int32, (nsp, past_len), 1) // SEL_BLOCK
    expand = (lax.broadcasted_iota(jnp.int32, (nsp, past_len), 0) == blk).astype(BF16)

    def rows(width, col_block=0):
        return pl.BlockSpec((ts, width), lambda b, pt, c=col_block: (b, c))

    grid_spec = pltpu.PrefetchScalarGridSpec(
        num_scalar_prefetch=1,
        grid=(nbatch,),
        in_specs=[pl.BlockSpec(memory_space=pl.ANY),
                  rows(NSA_WIDTH), rows(128, C_MISC // 128), rows(NSA_WIDTH), rows(sel.shape[1]),
                  rows(2 * KV_WIDTH),
                  pl.BlockSpec((None, None, 2 * KV_WIDTH, wb), lambda b, pt: (layer, b, 0, 0)),
                  rows(2 * KV_WIDTH),
                  pl.BlockSpec((nsp, past_len), lambda b, pt: (0, 0)),
                  pl.BlockSpec((1, NSA_WIDTH), lambda b, pt: (0, 0))],
        out_specs=rows(NSA_WIDTH),
        scratch_shapes=[pltpu.VMEM((2, 2 * KV_WIDTH, past_len), F32),
                        pltpu.SemaphoreType.DMA((2,))])
    return pl.pallas_call(
        body,
        grid_spec=grid_spec,
        out_shape=jax.ShapeDtypeStruct((nbatch * ts, NSA_WIDTH), F32),
        compiler_params=_cparams("arbitrary"),
        name="nsa_sample_attn",
    )(page_table, ckv_t, q_bf, z, o_c, sel, kvs_bf, cwin_t, win_bf, expand, nn)


def _layer_sample(x, mod, pw, l, t_real, ckv, cwin, page_table, s0, conv_prev):
    nb, ts, _ = x.shape
    m = nb * ts
    past_len = page_table.shape[1] * PAGE_SIZE
    sh1, sc1, g1, sh2, sc2, g2 = [jnp.repeat(a, ts, axis=0) for a in _split_mod(mod, (nb, D_MODEL))]
    x2 = x.reshape(m, D_MODEL)
    z = _inproj_call(x2, pw['norm1'][l], sc1, sh1, pw['w_in'][l], m, ts)
    o_g, s_t = _gla_call(z, pw['wa'][l], pw['ba'][l], pw['gla_norm'][l], s0, nb, ts, ts, t_real)
    q_bf, kv_new, win_new, kvs_bf, win_bf = _nsa_prep_call(
        z, pw['q_norm'][l], pw['ks_norm'][l], pw['kw_norm'][l], m)
    o_c, sel = _s1_call(page_table, ckv, q_bf, pw['w_cmp'][l], pw['kc_norm'][l], l, past_len)
    o_n = _s2_call(page_table, ckv, q_bf, z, o_c, sel, kvs_bf, cwin, win_bf, pw['nsa_norm'][l], l, past_len)
    x_mid, h2 = _outproj_call(o_g, o_n, x2, g1, pw['w_out'][l], pw['norm2'][l], sc2, sh2, m, ts)
    zrow = jnp.zeros((nb, ts, D_FF), F32)
    p1 = zrow.at[:, 0].set(conv_prev[:, 1]).reshape(m, D_FF)
    p2 = zrow.at[:, 0].set(conv_prev[:, 0]).at[:, 1].set(conv_prev[:, 1]).reshape(m, D_FF)
    x_out, u = _ffn_call(h2, x_mid, g2, pw['w_up'][l], pw['w_gate'][l], pw['w_down'][l],
                         pw['conv_w'][l], pw['conv_b'][l], m, ts, prev=(p1, p2))
    conv = u.reshape(nb, ts, D_FF)[:, t_real - (CONV_W - 1):t_real]
    return (x_out.reshape(nb, ts, D_MODEL),
            jnp.swapaxes(s_t, -1, -2),
            kv_new.reshape(nb, ts, 4, NSA_G, NSA_DH)[:, :t_real],
            win_new.reshape(nb, ts, 2, NSA_G, NSA_DH)[:, :t_real],
            conv)


def kernel(x_prompt, x_sample, cache_kv, cache_win, state_gla, state_conv, page_table, c_prompt, c_sample,
           norm1, norm2, w_ada, b_ada, w_in, w_gla_a, b_gla_a, gla_norm, q_norm, kc_norm, ks_norm, kw_norm,
           w_ck, w_cv, nsa_norm, w_out, w_up, w_gate, conv_w, conv_b, w_down):
    depth = w_in.shape[0]
    bp, bs, t_s = x_prompt.shape[0], x_sample.shape[0], x_sample.shape[1]
    assert CONV_W - 1 <= t_s <= SAMPLE_ROWS
    pw = _prep_weights({'w_in': w_in, 'w_gla_a': w_gla_a, 'b_gla_a': b_gla_a, 'w_ck': w_ck, 'w_cv': w_cv,
                        'norm1': norm1, 'norm2': norm2, 'gla_norm': gla_norm, 'nsa_norm': nsa_norm,
                        'q_norm': q_norm, 'kc_norm': kc_norm, 'ks_norm': ks_norm, 'kw_norm': kw_norm,
                        'w_out': w_out, 'w_up': w_up, 'w_gate': w_gate, 'w_down': w_down,
                        'conv_w': conv_w, 'conv_b': conv_b})
    bc = _round_up(bp + bs, 8)
    c_all = jnp.concatenate([c_prompt, c_sample, jnp.zeros((bc - bp - bs, D_MODEL), F32)], axis=0)
    mod_all = _mod_call(c_all, w_ada, b_ada)
    n_pool = cache_kv.shape[1]
    wb = cache_win.shape[2]
    ckv = jnp.transpose(cache_kv, (0, 1, 3, 4, 5, 2)).reshape(depth, n_pool, 4 * KV_WIDTH, PAGE_SIZE)
    cwin = jnp.transpose(cache_win, (0, 1, 3, 4, 5, 2)).reshape(depth, bs, 2 * KV_WIDTH, wb)
    xp = x_prompt
    xs = jnp.pad(x_sample, ((0, 0), (0, SAMPLE_ROWS - t_s), (0, 0)))
    outs_p, outs_s = [], []
    for l in range(depth):
        res = _layer_prompt(xp, mod_all[l, :bp], pw, l)
        xp = res[0]
        outs_p.append(res[1:])
        res = _layer_sample(xs, mod_all[l, bp:bp + bs], pw, l, t_s, ckv, cwin, page_table,
                            state_gla[l], state_conv[l])
        xs = res[0]
        outs_s.append(res[1:])
    stack = lambda outs, k: jnp.stack([o[k] for o in outs])
    win_s = jnp.stack([jnp.concatenate([cache_win[l][:, t_s:], outs_s[l][2]], axis=1) for l in range(depth)])
    return (xp, xs[:, :t_s], stack(outs_p, 1), stack(outs_s, 1), stack(outs_p, 2), win_s,
            stack(outs_p, 0), stack(outs_s, 0), stack(outs_p, 3), stack(outs_s, 3))
```

```python
import functools

import numpy as np
import jax
import jax.numpy as jnp
from jax import lax
from jax.experimental import pallas as pl
from jax.experimental.pallas import tpu as pltpu

F32 = jnp.float32
BF16 = jnp.bfloat16

D_MODEL = 1024
GLA_HEADS = 4
GLA_DK = 64
GLA_DV = 128
GLA_GATE_RANK = 16
GLA_TAU = 16.0
GLA_CHUNK = 16
GLA_WIDTH = GLA_HEADS * GLA_DV
NSA_HEADS = 8
NSA_G = 2
NSA_REP = NSA_HEADS // NSA_G
NSA_DH = 64
NSA_WIDTH = NSA_HEADS * NSA_DH
KV_WIDTH = NSA_G * NSA_DH
CMP_STRIDE = 16
CMP_LEN = 2 * CMP_STRIDE
SEL_BLOCK = 64
SEL_PER = SEL_BLOCK // CMP_STRIDE
SEL_TOP_N = 16
WINDOW = 512
D_FF = 2816
CONV_W = 3
RMS_EPS = 1e-6
BIG = 1e30
PAGE_SIZE = 128

IN_SIZES = (GLA_HEADS * GLA_DK, GLA_HEADS * GLA_DK, GLA_WIDTH, GLA_GATE_RANK, GLA_WIDTH,
            NSA_WIDTH, KV_WIDTH, KV_WIDTH, KV_WIDTH, KV_WIDTH, KV_WIDTH, KV_WIDTH, 3 * NSA_HEADS)

C_GQ, C_GK, C_GV, C_GR, C_NQ, C_KV, C_WIN, C_MISC = 0, 256, 512, 1024, 1536, 2048, 2560, 2816
MISC_GA, MISC_NG = 0, GLA_GATE_RANK
N_Z = 3072

VMEM_LIMIT_BYTES = 56 * 1024 * 1024
SLOPES = tuple(2.0 ** (-8.0 * h / NSA_HEADS) for h in range(1, NSA_HEADS + 1))


def _cparams(*sem):
    return pltpu.CompilerParams(dimension_semantics=sem, vmem_limit_bytes=VMEM_LIMIT_BYTES)


def _dot(a, b):
    return jnp.dot(a, b, preferred_element_type=F32)


def _dot_nt(a, b):
    return lax.dot_general(a, b, (((1,), (1,)), ((), ())), preferred_element_type=F32)


def _rms_rows(x, gain):
    return x * lax.rsqrt(jnp.mean(x * x, axis=-1, keepdims=True) + RMS_EPS) * gain


def _seg_rms64(x, gain):
    rows, width = x.shape
    lo_lane = (lax.broadcasted_iota(jnp.int32, (1, 128), 1) < 64)
    outs = []
    for c in range(width // 128):
        xb = x[:, c * 128:(c + 1) * 128]
        sq = xb * xb
        lo = jnp.sum(jnp.where(lo_lane, sq, 0.0), axis=-1, keepdims=True)
        hi = jnp.sum(jnp.where(lo_lane, 0.0, sq), axis=-1, keepdims=True)
        ms = jnp.where(lo_lane, lo, hi) * (1.0 / 64.0)
        outs.append(xb * lax.rsqrt(ms + RMS_EPS))
    y = outs[0] if len(outs) == 1 else jnp.concatenate(outs, axis=-1)
    return y * gain


def _mod_body(c_ref, w_ref, b_ref, o_ref):
    c = c_ref[...]
    a = c * jax.nn.sigmoid(c)
    o_ref[...] = _dot(a.astype(BF16), w_ref[...].astype(BF16)) + b_ref[...]


def _mod_call(c_all, w_ada, b_ada):
    depth, d, n = w_ada.shape
    bc = c_all.shape[0]
    tn = 1536
    return pl.pallas_call(
        _mod_body,
        grid=(depth, n // tn),
        in_specs=[pl.BlockSpec((bc, d), lambda l, j: (0, 0)),
                  pl.BlockSpec((None, d, tn), lambda l, j: (l, 0, j)),
                  pl.BlockSpec((None, 1, tn), lambda l, j: (l, 0, j))],
        out_specs=pl.BlockSpec((None, bc, tn), lambda l, j: (l, 0, j)),
        out_shape=jax.ShapeDtypeStruct((depth, bc, n), F32),
        compiler_params=_cparams("parallel", "parallel"),
        name="adaln_mod",
    )(c_all, w_ada, b_ada.reshape(depth, 1, n))


def _inproj_body(x_ref, g_ref, sc_ref, sh_ref, w_ref, z_ref, h_scr):
    @pl.when(pl.program_id(1) == 0)
    def _():
        y = _rms_rows(x_ref[...], g_ref[...])
        h_scr[...] = (y * (1.0 + sc_ref[...]) + sh_ref[...]).astype(BF16)

    z_ref[...] = _dot(h_scr[...], w_ref[...])


def _mod_spec(mod, tm, rows_per_batch):
    if mod.ndim == 3:
        return pl.BlockSpec((None, 1, D_MODEL), lambda i, *_: ((i * tm) // rows_per_batch, 0, 0))
    return pl.BlockSpec((tm, D_MODEL), lambda i, *_: (i, 0))


def _inproj_call(x2d, gain, sc, sh, w_bf, tm, rows_per_batch):
    m = x2d.shape[0]
    tn = 1024
    return pl.pallas_call(
        _inproj_body,
        grid=(m // tm, N_Z // tn),
        in_specs=[pl.BlockSpec((tm, D_MODEL), lambda i, j: (i, 0)),
                  pl.BlockSpec((1, D_MODEL), lambda i, j: (0, 0)),
                  _mod_spec(sc, tm, rows_per_batch),
                  _mod_spec(sh, tm, rows_per_batch),
                  pl.BlockSpec((D_MODEL, tn), lambda i, j: (0, j))],
        out_specs=pl.BlockSpec((tm, tn), lambda i, j: (i, j)),
        out_shape=jax.ShapeDtypeStruct((m, N_Z), F32),
        scratch_shapes=[pltpu.VMEM((tm, D_MODEL), BF16)],
        compiler_params=_cparams("parallel", "arbitrary"),
        name="inproj",
    )(x2d, gain, sc, sh, w_bf)


def _outproj_body(og_ref, on_ref, x_ref, g1_ref, w_ref, n2_ref, sc_ref, sh_ref, xo_ref, h2_ref):
    mix = (_dot(og_ref[...].astype(BF16), w_ref[:GLA_WIDTH, :])
           + _dot(on_ref[...].astype(BF16), w_ref[GLA_WIDTH:, :]))
    xn = x_ref[...] + g1_ref[...] * mix
    xo_ref[...] = xn
    y = _rms_rows(xn, n2_ref[...])
    h2_ref[...] = (y * (1.0 + sc_ref[...]) + sh_ref[...]).astype(BF16)


def _outproj_call(o_g, o_n, x2d, g1, w_bf, gain2, sc2, sh2, tm, rows_per_batch):
    m = x2d.shape[0]
    return pl.pallas_call(
        _outproj_body,
        grid=(m // tm,),
        in_specs=[pl.BlockSpec((tm, GLA_WIDTH), lambda i: (i, 0)),
                  pl.BlockSpec((tm, NSA_WIDTH), lambda i: (i, 0)),
                  pl.BlockSpec((tm, D_MODEL), lambda i: (i, 0)),
                  _mod_spec(g1, tm, rows_per_batch),
                  pl.BlockSpec((D_MODEL, D_MODEL), lambda i: (0, 0)),
                  pl.BlockSpec((1, D_MODEL), lambda i: (0, 0)),
                  _mod_spec(sc2, tm, rows_per_batch),
                  _mod_spec(sh2, tm, rows_per_batch)],
        out_specs=[pl.BlockSpec((tm, D_MODEL), lambda i: (i, 0)),
                   pl.BlockSpec((tm, D_MODEL), lambda i: (i, 0))],
        out_shape=[jax.ShapeDtypeStruct((m, D_MODEL), F32),
                   jax.ShapeDtypeStruct((m, D_MODEL), BF16)],
        compiler_params=_cparams("parallel"),
        name="outproj",
    )(o_g, o_n, x2d, g1, w_bf, gain2, sc2, sh2)


def _gelu_tanh(x):
    return x * (0.5 * (1.0 + jnp.tanh(0.7978845608028654 * (x + 0.044715 * (x * x * x)))))


def _ffn_body(*refs, tm, period, per_row_prev, keep):
    if per_row_prev:
        (h2_ref, x_ref, g2_ref, wu_ref, wg_ref, wd_ref, cw_ref, cb_ref, p1_ref, p2_ref,
         xo_ref, uk_ref, acc_scr, u_scr, g_scr) = refs
    else:
        (h2_ref, x_ref, g2_ref, wu_ref, wg_ref, wd_ref, cw_ref, cb_ref,
         xo_ref, uk_ref, acc_scr, u_scr, g_scr, carry_scr) = refs
    i, f = pl.program_id(0), pl.program_id(1)
    new, old = f % 2, (f + 1) % 2

    @pl.when(f == 0)
    def _():
        acc_scr[...] = jnp.zeros(acc_scr.shape, F32)
        u_scr[1] = jnp.zeros(u_scr.shape[1:], F32)
        g_scr[1] = jnp.zeros(g_scr.shape[1:], F32)

    if not per_row_prev:
        @pl.when((f == 0) & (i == 0))
        def _():
            carry_scr[...] = jnp.zeros(carry_scr.shape, F32)

    u = u_scr[old]
    gt = g_scr[old]
    row = lax.broadcasted_iota(jnp.int32, (tm, 1), 0)
    if per_row_prev:
        t = row % period
        m1, m2 = t >= 1, t >= 2
        prev1, prev2 = p1_ref[...], p2_ref[...]
    else:
        ft = jnp.maximum(f - 1, 0)
        fresh = ((i * tm) % period) == 0
        saved = carry_scr[ft]
        c = jnp.where(fresh, 0.0, saved)
        m1, m2 = row >= 1, row >= 2
        prev1 = c[7:8, :]
        prev2 = jnp.where(row == 0, c[6:7, :], c[7:8, :])
        carry_scr[ft] = jnp.where(f > 0, u[tm - 8:, :], saved)
    u_m1 = jnp.where(m1, pltpu.roll(u, 1, 0), prev1)
    u_m2 = jnp.where(m2, pltpu.roll(u, 2, 0), prev2)
    conv = u_m2 * cw_ref[0:1, :] + u_m1 * cw_ref[1:2, :] + u * cw_ref[2:3, :] + cb_ref[...]
    act = _gelu_tanh(conv) * gt
    acc_scr[...] += _dot(act.astype(BF16), wd_ref[...])
    uk_ref[...] = u[tm - keep:, :]

    h2 = h2_ref[...]
    u_scr[new] = _dot(h2, wu_ref[...])
    g_scr[new] = _dot(h2, wg_ref[...])

    @pl.when(f == pl.num_programs(1) - 1)
    def _():
        xo_ref[...] = x_ref[...] + g2_ref[...] * acc_scr[...]


def _ffn_call(h2, x2d, g2, wu, wg, wd, cw, cb, tm, period, prev=None):
    m = x2d.shape[0]
    tf = 256
    nf = D_FF // tf
    per_row_prev = prev is not None
    keep = tm if per_row_prev else 8
    body = functools.partial(_ffn_body, tm=tm, period=period, per_row_prev=per_row_prev, keep=keep)
    cur = lambda f: jnp.minimum(f, nf - 1)
    prv = lambda f: jnp.maximum(f - 1, 0)
    in_specs = [pl.BlockSpec((tm, D_MODEL), lambda i, f: (i, 0)),
                pl.BlockSpec((tm, D_MODEL), lambda i, f: (i, 0)),
                _mod_spec(g2, tm, period),
                pl.BlockSpec((D_MODEL, tf), lambda i, f: (0, cur(f))),
                pl.BlockSpec((D_MODEL, tf), lambda i, f: (0, cur(f))),
                pl.BlockSpec((tf, D_MODEL), lambda i, f: (prv(f), 0)),
                pl.BlockSpec((CONV_W, tf), lambda i, f: (0, prv(f))),
                pl.BlockSpec((1, tf), lambda i, f: (0, prv(f)))]
    args = [h2, x2d, g2, wu, wg, wd, cw, cb]
    scratch = [pltpu.VMEM((tm, D_MODEL), F32), pltpu.VMEM((2, tm, tf), F32), pltpu.VMEM((2, tm, tf), F32)]
    if per_row_prev:
        in_specs += [pl.BlockSpec((tm, tf), lambda i, f: (i, prv(f)))] * 2
        args += list(prev)
    else:
        scratch.append(pltpu.VMEM((nf, 8, tf), F32))
    return pl.pallas_call(
        body,
        grid=(m // tm, nf + 1),
        in_specs=in_specs,
        out_specs=[pl.BlockSpec((tm, D_MODEL), lambda i, f: (i, 0)),
                   pl.BlockSpec((keep, tf), lambda i, f: (i, prv(f)))],
        out_shape=[jax.ShapeDtypeStruct((m, D_MODEL), F32),
                   jax.ShapeDtypeStruct((m // tm * keep, D_FF), F32)],
        scratch_shapes=scratch,
        compiler_params=_cparams("arbitrary", "arbitrary"),
        name="convffn",
    )(*args)


GLA_ROWS = 128


def _log_sigmoid(x):
    return jnp.minimum(x, 0.0) - jnp.log1p(jnp.exp(-jnp.abs(x)))


def _gla_body(gq_ref, gk_ref, gv_ref, gr_ref, misc_ref, wa_ref, ba_ref, gn_ref, s0_ref,
              o_ref, sT_out_ref, sT_scr, *, rows_in, t_valid):
    R, C = GLA_ROWS, GLA_CHUNK
    nchunk = R // C
    step = pl.program_id(1)

    @pl.when(step == 0)
    def _():
        for h in range(GLA_HEADS):
            sT_scr[h] = s0_ref[h].T

    def padded(ref):
        x = ref[...]
        if rows_in < R:
            x = jnp.concatenate([x, jnp.zeros((R - rows_in, x.shape[1]), x.dtype)], axis=0)
        return x

    row = lax.broadcasted_iota(jnp.int32, (R, 1), 0)
    live = row < t_valid
    q = padded(gq_ref) * (GLA_DK ** -0.5)
    k = jnp.where(live, padded(gk_ref), 0.0)
    v = jnp.where(live, padded(gv_ref), 0.0)
    xa = _dot(padded(misc_ref).astype(BF16), wa_ref[...]) + ba_ref[...]
    la = jnp.where(live, _log_sigmoid(xa) * (1.0 / GLA_TAU), 0.0)

    rc = row % C
    b = la
    shift = 1
    while shift < C:
        b = b + jnp.where(rc >= shift, pltpu.roll(b, shift, 0), 0.0)
        shift *= 2
    b3 = b.reshape(nchunk, C, GLA_HEADS * GLA_DK)
    b_last = jnp.broadcast_to(b3[:, C - 1:C, :], b3.shape).reshape(R, GLA_HEADS * GLA_DK)
    qe = (q * jnp.exp(b)).astype(BF16)
    ke = (k * jnp.exp(-b)).astype(BF16)
    kd = (k * jnp.exp(b_last - b)).astype(BF16)
    decay = jnp.exp(b_last)

    ri = lax.broadcasted_iota(jnp.int32, (R, R), 0)
    ci = lax.broadcasted_iota(jnp.int32, (R, R), 1)
    causal = (ri // C == ci // C) & (ci <= ri)
    col_chunk = lax.broadcasted_iota(jnp.int32, (GLA_DV, R), 1) // C

    outs = []
    for h in range(GLA_HEADS):
        ks = slice(h * GLA_DK, (h + 1) * GLA_DK)
        vs = slice(h * GLA_DV, (h + 1) * GLA_DV)
        v_h = v[:, vs]
        att = jnp.where(causal, _dot_nt(qe[:, ks], ke[:, ks]), 0.0)
        o_h = _dot(att.astype(BF16), v_h.astype(BF16))
        vT = v_h.T
        stack = jnp.concatenate([jnp.where(col_chunk == n, vT, 0.0) for n in range(nchunk)], axis=0)
        incT = _dot(stack.astype(BF16), kd[:, ks])
        sT = sT_scr[h]
        inter = []
        for n in range(nchunk):
            inter.append(_dot_nt(qe[n * C:(n + 1) * C, ks], sT.astype(BF16)))
            sT = decay[n * C + C - 1:n * C + C, ks] * sT + incT[n * GLA_DV:(n + 1) * GLA_DV, :]
        sT_scr[h] = sT
        o_h = o_h + jnp.concatenate(inter, axis=0)
        outs.append(_rms_rows(o_h, gn_ref[:, vs]))
    o = jnp.concatenate(outs, axis=-1)
    gr = padded(gr_ref)
    o = o * (gr * jax.nn.sigmoid(gr))
    o_ref[...] = o[:rows_in, :]

    @pl.when(step == pl.num_programs(1) - 1)
    def _():
        sT_out_ref[...] = sT_scr[...]


def _gla_call(z, wa_pad_bf, ba, gnorm, s0, n_batch, rows_per_batch, rows_in, t_valid):
    steps = rows_per_batch // rows_in
    body = functools.partial(_gla_body, rows_in=rows_in, t_valid=t_valid)

    def zspec(col, width):
        return pl.BlockSpec((rows_in, width), lambda b, s, c=col // width: (b * steps + s, c))

    return pl.pallas_call(
        body,
        grid=(n_batch, steps),
        in_specs=[zspec(C_GQ, 256), zspec(C_GK, 256), zspec(C_GV, 512), zspec(C_GR, 512),
                  zspec(C_MISC, 128),
                  pl.BlockSpec((128, GLA_HEADS * GLA_DK), lambda b, s: (0, 0)),
                  pl.BlockSpec((1, GLA_HEADS * GLA_DK), lambda b, s: (0, 0)),
                  pl.BlockSpec((1, GLA_WIDTH), lambda b, s: (0, 0)),
                  pl.BlockSpec((None, GLA_HEADS, GLA_DK, GLA_DV), lambda b, s: (b, 0, 0, 0))],
        out_specs=[pl.BlockSpec((rows_in, GLA_WIDTH), lambda b, s: (b * steps + s, 0)),
                   pl.BlockSpec((None, GLA_HEADS, GLA_DV, GLA_DK), lambda b, s: (b, 0, 0, 0))],
        out_shape=[jax.ShapeDtypeStruct((n_batch * rows_per_batch, GLA_WIDTH), F32),
                   jax.ShapeDtypeStruct((n_batch, GLA_HEADS, GLA_DV, GLA_DK), F32)],
        scratch_shapes=[pltpu.VMEM((GLA_HEADS, GLA_DV, GLA_DK), F32)],
        compiler_params=_cparams("parallel", "arbitrary"),
        name="gla",
    )(z, z, z, z, z, wa_pad_bf, ba, gnorm, s0)


def _nsa_prep_body(nq_ref, kv_ref, win_ref, qg_ref, ksg_ref, kwg_ref,
                   q_out, kv_out, win_out, kvs_bf, win_bf):
    q = _seg_rms64(nq_ref[...], qg_ref[...]) * (NSA_DH ** -0.5)
    q_out[...] = q.astype(BF16)
    kv = kv_ref[...]
    ksn = _seg_rms64(kv[:, 2 * KV_WIDTH:3 * KV_WIDTH], ksg_ref[...])
    vs = kv[:, 3 * KV_WIDTH:]
    kv_out[:, :2 * KV_WIDTH] = kv[:, :2 * KV_WIDTH]
    kv_out[:, 2 * KV_WIDTH:3 * KV_WIDTH] = ksn
    kv_out[:, 3 * KV_WIDTH:] = vs
    kvs_bf[:, :KV_WIDTH] = ksn.astype(BF16)
    kvs_bf[:, KV_WIDTH:] = vs.astype(BF16)
    w = win_ref[...]
    kwn = _seg_rms64(w[:, :KV_WIDTH], kwg_ref[...])
    win_out[:, :KV_WIDTH] = kwn
    win_out[:, KV_WIDTH:] = w[:, KV_WIDTH:]
    win_bf[:, :KV_WIDTH] = kwn.astype(BF16)
    win_bf[:, KV_WIDTH:] = w[:, KV_WIDTH:].astype(BF16)


def _nsa_prep_call(z, qg, ksg, kwg, tm):
    m = z.shape[0]

    def zspec(col, width):
        return pl.BlockSpec((tm, width), lambda i, c=col // width: (i, c))

    def ospec(width):
        return pl.BlockSpec((tm, width), lambda i: (i, 0))

    return pl.pallas_call(
        _nsa_prep_body,
        grid=(m // tm,),
        in_specs=[zspec(C_NQ, 512), zspec(C_KV, 512), zspec(C_WIN, 256),
                  pl.BlockSpec((1, 512), lambda i: (0, 0)),
                  pl.BlockSpec((1, 128), lambda i: (0, 0)),
                  pl.BlockSpec((1, 128), lambda i: (0, 0))],
        out_specs=[ospec(512), ospec(512), ospec(256), ospec(256), ospec(256)],
        out_shape=[jax.ShapeDtypeStruct((m, 512), BF16),
                   jax.ShapeDtypeStruct((m, 512), F32),
                   jax.ShapeDtypeStruct((m, 256), F32),
                   jax.ShapeDtypeStruct((m, 256), BF16),
                   jax.ShapeDtypeStruct((m, 256), BF16)],
        compiler_params=_cparams("parallel"),
        name="nsa_prep",
    )(z, z, z, qg, ksg, kwg)


def _with_pos_lanes(keys, blk, lo):
    lane = lax.broadcasted_iota(jnp.int32, keys.shape, 1) - NSA_DH
    pos = jnp.where(lane == 0, lo, jnp.where((lane > 0) & (lane == blk), 1, 0)).astype(F32)
    return jnp.where(lane < 0, keys, pos).astype(BF16)


def _group_keys(k2, g):
    return k2 if g == 0 else pltpu.roll(k2, NSA_DH, 1)


def _nsa_prep_t_body(nq_ref, kv_ref, win_ref, misc_ref, qg_ref, ksg_ref, kwg_ref,
                     qt_out, gt_out, kv_out, win_out, ksel_out, vselt_out, kwin_out, vwint_out,
                     *, tm, length):
    q = _seg_rms64(nq_ref[...], qg_ref[...]) * (NSA_DH ** -0.5)
    qt_out[...] = q.T.astype(BF16)
    gt_out[...] = jax.nn.sigmoid(misc_ref[...]).T[MISC_NG:MISC_NG + 32, :]
    kv = kv_ref[...]
    ksn = _seg_rms64(kv[:, 2 * KV_WIDTH:3 * KV_WIDTH], ksg_ref[...])
    vs = kv[:, 3 * KV_WIDTH:]
    kv_out[:, :2 * KV_WIDTH] = kv[:, :2 * KV_WIDTH]
    kv_out[:, 2 * KV_WIDTH:3 * KV_WIDTH] = ksn
    kv_out[:, 3 * KV_WIDTH:] = vs
    w = win_ref[...]
    kwn = _seg_rms64(w[:, :KV_WIDTH], kwg_ref[...])
    vw = w[:, KV_WIDTH:]
    win_out[:, :KV_WIDTH] = kwn
    win_out[:, KV_WIDTH:] = vw

    pos = (pl.program_id(0) * tm + lax.broadcasted_iota(jnp.int32, (tm, 1), 0)) % length
    lane = lax.broadcasted_iota(jnp.int32, (tm, KV_WIDTH), 1)
    for g in range(NSA_G):
        gs = slice(g * KV_WIDTH, (g + 1) * KV_WIDTH)
        ksel_out[:, gs] = _with_pos_lanes(_group_keys(ksn, g), pos // SEL_BLOCK, pos % SEL_BLOCK)
        kwin_out[:, gs] = jnp.where(lane < NSA_DH, _group_keys(kwn, g), 0.0).astype(BF16)
    ones = jnp.where(lax.broadcasted_iota(jnp.int32, (NSA_DH, tm), 0) == 0, 1.0, 0.0)

    def values_t(v):
        vt = v.T
        return jnp.concatenate([vt[:NSA_DH], ones, vt[NSA_DH:], ones], axis=0).astype(BF16)

    vselt_out[...] = values_t(vs)
    vwint_out[...] = values_t(vw)


def _nsa_prep_t_call(z, qg, ksg, kwg, tm, length):
    m = z.shape[0]

    def zspec(col, width):
        return pl.BlockSpec((tm, width), lambda i, c=col // width: (i, c))

    def rows(width):
        return pl.BlockSpec((tm, width), lambda i: (i, 0))

    def cols(height):
        return pl.BlockSpec((height, tm), lambda i: (0, i))

    return pl.pallas_call(
        functools.partial(_nsa_prep_t_body, tm=tm, length=length),
        grid=(m // tm,),
        in_specs=[zspec(C_NQ, 512), zspec(C_KV, 512), zspec(C_WIN, 256), zspec(C_MISC, 128),
                  pl.BlockSpec((1, 512), lambda i: (0, 0)),
                  pl.BlockSpec((1, 128), lambda i: (0, 0)),
                  pl.BlockSpec((1, 128), lambda i: (0, 0))],
        out_specs=[cols(NSA_WIDTH), cols(32), rows(512), rows(256), rows(256), cols(256), rows(256), cols(256)],
        out_shape=[jax.ShapeDtypeStruct((NSA_WIDTH, m), BF16),
                   jax.ShapeDtypeStruct((32, m), F32),
                   jax.ShapeDtypeStruct((m, 512), F32),
                   jax.ShapeDtypeStruct((m, 256), F32),
                   jax.ShapeDtypeStruct((m, 256), BF16),
                   jax.ShapeDtypeStruct((256, m), BF16),
                   jax.ShapeDtypeStruct((m, 256), BF16),
                   jax.ShapeDtypeStruct((256, m), BF16)],
        compiler_params=_cparams("parallel"),
        name="nsa_prep_t",
    )(z, z, z, z, qg, ksg, kwg)


def _compress_rows(load_rows, w_ref, kind, nblk):
    acc = jnp.zeros((nblk, 2 * KV_WIDTH), F32)
    for j in range(CMP_STRIDE):
        acc = acc + _dot(load_rows(j).astype(BF16), w_ref[kind, j])
    row = lax.broadcasted_iota(jnp.int32, (nblk, 1), 0)
    bot_next = pltpu.roll(acc[:, KV_WIDTH:], nblk - 1, 0)
    return jnp.where(row < nblk - 1, acc[:, :KV_WIDTH] + bot_next, 0.0)


def _compress_body(krows_ref, vrows_ref, w_ref, g_ref, kc_out, vct_out, *, nblk):
    def loader(ref):
        return lambda j: ref[pl.ds(j, nblk, stride=CMP_STRIDE), :]

    kc = _compress_rows(loader(krows_ref), w_ref, 0, nblk)
    vc = _compress_rows(loader(vrows_ref), w_ref, 1, nblk)
    kcn = _seg_rms64(kc, g_ref[...])
    ec = CMP_STRIDE * lax.broadcasted_iota(jnp.int32, (nblk, 1), 0) + (CMP_LEN - 1)
    for g in range(NSA_G):
        kc_out[:, g * KV_WIDTH:(g + 1) * KV_WIDTH] = _with_pos_lanes(
            _group_keys(kcn, g), ec // SEL_BLOCK, ec % SEL_BLOCK)
    vct_out[...] = vc.T.astype(BF16)


def _compress_call(kv_new3, w_cmp, kcg):
    nb, length, _ = kv_new3.shape
    nblk = length // CMP_STRIDE
    return pl.pallas_call(
        functools.partial(_compress_body, nblk=nblk),
        grid=(nb,),
        in_specs=[pl.BlockSpec((None, length, KV_WIDTH), lambda b: (b, 0, 0)),
                  pl.BlockSpec((None, length, KV_WIDTH), lambda b: (b, 0, 1)),
                  pl.BlockSpec((2, CMP_STRIDE, KV_WIDTH, 2 * KV_WIDTH), lambda b: (0, 0, 0, 0)),
                  pl.BlockSpec((1, KV_WIDTH), lambda b: (0, 0))],
        out_specs=[pl.BlockSpec((None, nblk, 2 * KV_WIDTH), lambda b: (b, 0, 0)),
                   pl.BlockSpec((None, KV_WIDTH, nblk), lambda b: (b, 0, 0))],
        out_shape=[jax.ShapeDtypeStruct((nb, nblk, 2 * KV_WIDTH), BF16),
                   jax.ShapeDtypeStruct((nb, KV_WIDTH, nblk), BF16)],
        compiler_params=_cparams("parallel"),
        name="nsa_compress",
    )(kv_new3, kv_new3, w_cmp, kcg)


def _round_up(a, b):
    return -(-a // b) * b


def _tile_rows(a, n):
    return jnp.concatenate([a] * n, axis=0)


def _softmax_rows(s, valid):
    s = jnp.where(valid, s, -BIG)
    m = jnp.max(s, axis=-1, keepdims=True)
    e = jnp.where(valid, jnp.exp(s - m), 0.0)
    l = jnp.sum(e, axis=-1, keepdims=True)
    return e / jnp.where(l > 0.0, l, 1.0)


def _sel_rows(ns):
    return _round_up(ns, 8)


def _imp_scratch_rows(ns):
    return 8 + SEL_PER * _sel_rows(ns) + 8


def _select_blocks_t(imp_t, imp_scr, sc_scr, t_row, ns, n_sel, n_rank=None):
    nb, tq = imp_t.shape
    nsr = _sel_rows(ns)
    rows = _imp_scratch_rows(ns)
    imp_scr[0:8, :] = jnp.zeros((8, tq), F32)
    imp_scr[8 + nb:rows, :] = jnp.zeros((rows - 8 - nb, tq), F32)
    imp_scr[8:8 + nb, :] = imp_t

    def ld(off):
        return imp_scr[pl.ds(8 + off, nsr, stride=SEL_PER), :]

    p_slc = ld(-1) + 2.0 * (ld(0) + ld(1) + ld(2)) + ld(3)
    j = lax.broadcasted_iota(jnp.int32, (nsr, 1), 0)
    cur = t_row // SEL_BLOCK
    forced = (j == 0) | (j == cur) | (j == cur - 1)
    score = jnp.where(forced, BIG, jnp.where(j <= cur, p_slc, -BIG))
    sc_scr[...] = score

    def rank_body(k, cnt):
        sk = sc_scr[pl.ds(k, 1), :]
        beats = (sk > score) | ((sk == score) & (k < j))
        return cnt + jnp.where(beats, 1.0, 0.0)

    cnt = lax.fori_loop(0, ns if n_rank is None else n_rank, rank_body, jnp.zeros((nsr, tq), F32))
    return jnp.where((cnt < float(n_sel)) & (j < ns), 1.0, 0.0)


def _select_blocks(imp, imp_scr, sc_scr, t_row, ns, n_sel):
    sel_t = _select_blocks_t(imp.T, imp_scr, sc_scr, t_row, ns, n_sel)
    nsp = _round_up(ns, 128)
    if nsp > sel_t.shape[0]:
        sel_t = jnp.concatenate([sel_t, jnp.zeros((nsp - sel_t.shape[0], sel_t.shape[1]), F32)], axis=0)
    return sel_t.T.astype(BF16)


def _online_update(carry, s, valid, pv):
    m, l, acc = carry
    s = jnp.where(valid, s, -BIG)
    m_new = jnp.maximum(m, jnp.max(s, axis=-1, keepdims=True))
    alpha = jnp.exp(m - m_new)
    e = jnp.where(valid, jnp.exp(s - m_new), 0.0)
    l = alpha * l + jnp.sum(e, axis=-1, keepdims=True)
    acc = alpha * acc + pv(e.astype(BF16))
    return m_new, l, acc


def _online_init(rows, width):
    return (jnp.full((rows, 1), -BIG, F32), jnp.zeros((rows, 1), F32), jnp.zeros((rows, width), F32))


def _online_finish(carry):
    _, l, acc = carry
    return acc / jnp.where(l > 0.0, l, 1.0)


NEG_MASK = -(2.0 ** 100)
CMP_TQ = 128
ATT_TQ = 256
ATT_TK = 512


def _slope_lanes(g, tq):
    r = lax.broadcasted_iota(jnp.int32, (1, NSA_REP * tq), 1) // tq
    out = jnp.full((1, NSA_REP * tq), SLOPES[g * NSA_REP], F32)
    for k in range(1, NSA_REP):
        out = jnp.where(r == k, SLOPES[g * NSA_REP + k], out)
    return out


def _q_aug_t(qt_ref, g, tq, qsel_t):
    lanes = NSA_REP * tq
    qrows = jnp.concatenate(
        [qt_ref[(g * NSA_REP + r) * NSA_DH:(g * NSA_REP + r + 1) * NSA_DH, :] for r in range(NSA_REP)], axis=1)
    j = lax.broadcasted_iota(jnp.int32, (NSA_DH, lanes), 0)
    slope = _slope_lanes(g, tq)
    aug = jnp.where(j == 0, slope, slope * (float(SEL_BLOCK) * j.astype(F32)))
    if qsel_t is not None:
        aug = jnp.where(j == 0, aug, aug + _tile_lanes(qsel_t.astype(F32), NSA_REP))
    return jnp.concatenate([qrows, aug.astype(BF16)], axis=0)


def _tile_lanes(a, n):
    return jnp.concatenate([a] * n, axis=1)


def _cmp_select_body(qt_ref, kc_ref, vct_ref, oct_ref, qselt_ref, cnt_ref, imp_scr, sc_scr, *, length, tq):
    nb = length // CMP_STRIDE
    ns = length // SEL_BLOCK
    nsr = _sel_rows(ns)
    n_sel = min(SEL_TOP_N, ns)
    t0 = pl.program_id(1) * tq
    t_row = t0 + lax.broadcasted_iota(jnp.int32, (1, tq), 1)
    ec = CMP_STRIDE * lax.broadcasted_iota(jnp.int32, (nb, 1), 0) + (CMP_LEN - 1)
    hidden = _tile_lanes(jnp.where(t_row >= ec, 0.0, NEG_MASK), NSA_REP)
    counts = []
    for g in range(NSA_G):
        s_t = _dot(kc_ref[:, g * KV_WIDTH:(g + 1) * KV_WIDTH], _q_aug_t(qt_ref, g, tq, None)) + hidden
        m = jnp.max(s_t, axis=0, keepdims=True)
        e = jnp.where(hidden < 0.0, 0.0, jnp.exp(s_t - m))
        l = jnp.sum(e, axis=0, keepdims=True)
        p_t = e * jnp.where(l > 0.0, 1.0 / l, 0.0)
        o_ct = _dot(vct_ref[g * NSA_DH:(g + 1) * NSA_DH, :], p_t.astype(BF16))
        imp_t = p_t[:, 0:tq]
        for r in range(1, NSA_REP):
            imp_t = imp_t + p_t[:, r * tq:(r + 1) * tq]
        n_rank = jnp.minimum((t0 + tq - 1) // SEL_BLOCK + 1, ns)
        sel_t = _select_blocks_t(imp_t, imp_scr, sc_scr, t_row, ns, n_sel, n_rank)
        qsel = jnp.where(sel_t > 0.5, 0.0, NEG_MASK)
        if nsr < SEL_BLOCK:
            qsel = jnp.concatenate([qsel, jnp.full((SEL_BLOCK - nsr, tq), NEG_MASK, F32)], axis=0)
        qselt_ref[g * SEL_BLOCK:(g + 1) * SEL_BLOCK, :] = qsel.astype(BF16)
        cnt = _dot_nt(jnp.ones((8, tq), BF16), sel_t.astype(BF16))
        if nsr < SEL_BLOCK:
            cnt = jnp.concatenate([cnt, jnp.zeros((8, SEL_BLOCK - nsr), F32)], axis=1)
        counts.append(cnt)
        for r in range(NSA_REP):
            h = g * NSA_REP + r
            oct_ref[h * NSA_DH:(h + 1) * NSA_DH, :] = o_ct[:, r * tq:(r + 1) * tq]
    cnt_ref[...] = jnp.concatenate(counts, axis=1)


def _cmp_select_call(q_t, kc_aug, vc_t, n_batch, length):
    tq = CMP_TQ
    nq = length // tq
    nb = length // CMP_STRIDE
    ns = length // SEL_BLOCK
    assert ns <= SEL_BLOCK
    m = n_batch * length
    return pl.pallas_call(
        functools.partial(_cmp_select_body, length=length, tq=tq),
        grid=(n_batch, nq),
        in_specs=[pl.BlockSpec((NSA_WIDTH, tq), lambda b, i: (0, b * nq + i)),
                  pl.BlockSpec((None, nb, 2 * KV_WIDTH), lambda b, i: (b, 0, 0)),
                  pl.BlockSpec((None, KV_WIDTH, nb), lambda b, i: (b, 0, 0))],
        out_specs=[pl.BlockSpec((NSA_WIDTH, tq), lambda b, i: (0, b * nq + i)),
                   pl.BlockSpec((NSA_G * SEL_BLOCK, tq), lambda b, i: (0, b * nq + i)),
                   pl.BlockSpec((8, NSA_G * SEL_BLOCK), lambda b, i: (b * nq + i, 0))],
        out_shape=[jax.ShapeDtypeStruct((NSA_WIDTH, m), F32),
                   jax.ShapeDtypeStruct((NSA_G * SEL_BLOCK, m), BF16),
                   jax.ShapeDtypeStruct((n_batch * nq * 8, NSA_G * SEL_BLOCK), F32)],
        scratch_shapes=[pltpu.VMEM((_imp_scratch_rows(ns), tq), F32), pltpu.VMEM((_sel_rows(ns), tq), F32)],
        compiler_params=_cparams("parallel", "parallel"),
        name="nsa_cmp_select",
    )(q_t, kc_aug, vc_t)


def _nsa_attn_body(flags_ref, qt_ref, qselt_ref, gt_ref, oct_ref, ksel_ref, vselt_ref, kwin_ref, vwint_ref,
                   nn_ref, o_ref, m_scr, acc_scr, on_scr, *, length, tq, tk):
    b, i = pl.program_id(0), pl.program_id(1)
    nq = pl.num_programs(1)
    n_kt_all = length // tk
    t0 = i * tq
    t_row = t0 + lax.broadcasted_iota(jnp.int32, (1, tq), 1)
    span = WINDOW + tq
    w0 = pl.multiple_of(jnp.maximum(t0 - WINDOW, 0), tq)
    w_rel = lax.broadcasted_iota(jnp.int32, (span, 1), 0)
    in_window = (w0 + w_rel <= t_row) & (w0 + w_rel >= t_row - WINDOW)
    win_mask = _tile_lanes(jnp.where(in_window, 0.0, NEG_MASK), NSA_REP)
    for g in range(NSA_G):
        qa = _q_aug_t(qt_ref, g, tq, qselt_ref[g * SEL_BLOCK:(g + 1) * SEL_BLOCK, :])
        gs = slice(g * KV_WIDTH, (g + 1) * KV_WIDTH)
        vrows = slice(g * 2 * NSA_DH, (g + 1) * 2 * NSA_DH)

        def reset():
            m_scr[...] = jnp.full(m_scr.shape, -BIG, F32)
            acc_scr[...] = jnp.zeros(acc_scr.shape, F32)

        def step(k_ref, vt_ref, k0, rows, causal):
            s_t = _dot(k_ref[pl.ds(k0, rows), gs], qa)
            if causal:
                kpos = k0 + lax.broadcasted_iota(jnp.int32, (rows, 1), 0)
                s_t = s_t + _tile_lanes(jnp.where(kpos <= t_row, 0.0, NEG_MASK), NSA_REP)
            m_old = m_scr[...]
            m_new = jnp.maximum(m_old, jnp.max(s_t, axis=0, keepdims=True))
            e = jnp.exp(s_t - m_new).astype(BF16)
            acc_scr[...] = jnp.exp(m_old - m_new) * acc_scr[...] + _dot(vt_ref[vrows, pl.ds(k0, rows)], e)
            m_scr[...] = m_new

        def finish():
            acc = acc_scr[...]
            l = acc[NSA_DH:NSA_DH + 1, :]
            return acc[0:NSA_DH, :] / jnp.where(l > 0.0, l, 1.0)

        reset()
        last = (t0 + tq + tk - 1) // tk - 1
        fbase = ((b * nq + i) * NSA_G + g) * n_kt_all

        def sel_tile(kt, carry):
            @pl.when(flags_ref[fbase + kt] != 0)
            def _():
                step(ksel_ref, vselt_ref, pl.multiple_of(kt * tk, tk), tk, False)
            return carry

        lax.fori_loop(0, last, sel_tile, 0)
        step(ksel_ref, vselt_ref, pl.multiple_of(last * tk, tk), tk, True)
        o_s = finish()

        k_w = _with_pos_lanes(kwin_ref[pl.ds(w0, span), gs].astype(F32), w_rel // SEL_BLOCK, w_rel % SEL_BLOCK)
        s_w = _dot(k_w, _q_aug_t(qt_ref, g, tq, None)) + win_mask
        e = jnp.exp(s_w - jnp.max(s_w, axis=0, keepdims=True)).astype(BF16)
        acc = _dot(vwint_ref[vrows, pl.ds(w0, span)], e)
        o_w = acc[0:NSA_DH, :] / acc[NSA_DH:NSA_DH + 1, :]

        for r in range(NSA_REP):
            h = g * NSA_REP + r
            hs = slice(h * NSA_DH, (h + 1) * NSA_DH)
            ls = slice(r * tq, (r + 1) * tq)
            on_scr[hs, :] = (gt_ref[h:h + 1, :] * oct_ref[hs, :]
                             + gt_ref[NSA_HEADS + h:NSA_HEADS + h + 1, :] * o_s[:, ls]
                             + gt_ref[2 * NSA_HEADS + h:2 * NSA_HEADS + h + 1, :] * o_w[:, ls])
    on = on_scr[...]
    ms = jnp.sum(on * on, axis=0, keepdims=True) * (1.0 / NSA_WIDTH)
    o_ref[...] = (on * lax.rsqrt(ms + RMS_EPS) * nn_ref[...]).T


def _nsa_attn_call(flags, q_t, qsel_t, g_t, oc_t, ksel, vsel_t, kwin, vwin_t, nn_col, n_batch, length):
    tq, tk = ATT_TQ, ATT_TK
    nq = length // tq
    assert WINDOW % tq == 0 and length % tk == 0 and tk % tq == 0 and length >= WINDOW + tq
    body = functools.partial(_nsa_attn_body, length=length, tq=tq, tk=tk)

    def cols(rows):
        return pl.BlockSpec((rows, tq), lambda b, i, fl: (0, b * nq + i))

    def keys(width=2 * KV_WIDTH):
        return pl.BlockSpec((length, width), lambda b, i, fl: (b, 0))

    def values():
        return pl.BlockSpec((2 * KV_WIDTH, length), lambda b, i, fl: (0, b))

    grid_spec = pltpu.PrefetchScalarGridSpec(
        num_scalar_prefetch=1,
        grid=(n_batch, nq),
        in_specs=[cols(NSA_WIDTH), cols(NSA_G * SEL_BLOCK), cols(32), cols(NSA_WIDTH),
                  keys(), values(), keys(), values(),
                  pl.BlockSpec((NSA_WIDTH, 1), lambda b, i, fl: (0, 0))],
        out_specs=pl.BlockSpec((tq, NSA_WIDTH), lambda b, i, fl: (b * nq + i, 0)),
        scratch_shapes=[pltpu.VMEM((1, NSA_REP * tq), F32),
                        pltpu.VMEM((2 * NSA_DH, NSA_REP * tq), F32),
                        pltpu.VMEM((NSA_WIDTH, tq), F32)])
    return pl.pallas_call(
        body,
        grid_spec=grid_spec,
        out_shape=jax.ShapeDtypeStruct((n_batch * length, NSA_WIDTH), F32),
        compiler_params=_cparams("parallel", "parallel"),
        name="nsa_attn",
    )(flags, q_t, qsel_t, g_t, oc_t, ksel, vsel_t, kwin, vwin_t, nn_col)


def _prep_weights(p):
    depth = p['w_in'].shape[0]
    offs = np.concatenate([[0], np.cumsum(IN_SIZES)])
    piece = lambda i: p['w_in'][:, :, int(offs[i]):int(offs[i + 1])]
    order = [0, 1, 2, 4, 5, 6, 7, 8, 9, 10, 11, 3, 12]
    cols = [piece(i) for i in order]
    used = sum(IN_SIZES)
    cols.append(jnp.zeros((depth, D_MODEL, N_Z - used), F32))
    w_in = jnp.concatenate(cols, axis=-1).astype(BF16)
    wa = jnp.zeros((depth, 128, GLA_HEADS * GLA_DK), F32).at[:, :GLA_GATE_RANK, :].set(p['w_gla_a']).astype(BF16)

    def cmp_weights(w):
        eye = jnp.eye(NSA_G, dtype=F32)
        bd = jnp.einsum('gh,ljde->ljgdhe', eye, w).reshape(depth, CMP_LEN, KV_WIDTH, KV_WIDTH)
        return jnp.concatenate([bd[:, :CMP_STRIDE], bd[:, CMP_STRIDE:]], axis=-1)

    w_cmp = jnp.stack([cmp_weights(p['w_ck']), cmp_weights(p['w_cv'])], axis=1).astype(BF16)
    tile = lambda a, n: jnp.tile(a, (1, n))[:, None, :]
    return {
        'w_in': w_in, 'wa': wa, 'ba': p['b_gla_a'][:, None, :], 'w_cmp': w_cmp,
        'norm1': p['norm1'][:, None, :], 'norm2': p['norm2'][:, None, :],
        'gla_norm': p['gla_norm'][:, None, :], 'nsa_norm': p['nsa_norm'][:, None, :],
        'q_norm': tile(p['q_norm'], NSA_HEADS), 'kc_norm': tile(p['kc_norm'], NSA_G),
        'ks_norm': tile(p['ks_norm'], NSA_G), 'kw_norm': tile(p['kw_norm'], NSA_G),
        'w_out': p['w_out'].astype(BF16), 'w_up': p['w_up'].astype(BF16),
        'w_gate': p['w_gate'].astype(BF16), 'w_down': p['w_down'].astype(BF16),
        'conv_w': p['conv_w'], 'conv_b': p['conv_b'][:, None, :],
    }


def _split_mod(mod, shape):
    return [mod[:, k * D_MODEL:(k + 1) * D_MODEL].reshape(shape) for k in range(6)]


def _layer_prompt(x, mod, pw, l):
    nb, length, _ = x.shape
    m = nb * length
    sh1, sc1, g1, sh2, sc2, g2 = _split_mod(mod, (nb, 1, D_MODEL))
    x2 = x.reshape(m, D_MODEL)
    tm = min(512, length)
    z = _inproj_call(x2, pw['norm1'][l], sc1, sh1, pw['w_in'][l], min(1024, length), length)
    s0 = jnp.zeros((nb, GLA_HEADS, GLA_DK, GLA_DV), F32)
    o_g, s_t = _gla_call(z, pw['wa'][l], pw['ba'][l], pw['gla_norm'][l], s0, nb, length, GLA_ROWS, length)
    q_t, g_t, kv_new, win_new, ksel, vsel_t, kwin, vwin_t = _nsa_prep_t_call(
        z, pw['q_norm'][l], pw['ks_norm'][l], pw['kw_norm'][l], tm, length)
    kc_aug, vc_t = _compress_call(kv_new.reshape(nb, length, 4 * KV_WIDTH), pw['w_cmp'][l], pw['kc_norm'][l])
    oc_t, qsel_t, cnt = _cmp_select_call(q_t, kc_aug, vc_t, nb, length)
    per_tile = ATT_TK // SEL_BLOCK
    flags = cnt.reshape(-1, ATT_TQ // CMP_TQ, 8, NSA_G, SEL_BLOCK // per_tile, per_tile)
    flags = jnp.sum(flags[:, :, 0, :, :length // ATT_TK], axis=(1, -1))
    flags = (flags > 0.0).astype(jnp.int32).reshape(-1)
    o_n = _nsa_attn_call(flags, q_t, qsel_t, g_t, oc_t, ksel, vsel_t, kwin, vwin_t,
                         pw['nsa_norm'][l].reshape(NSA_WIDTH, 1), nb, length)
    x_mid, h2 = _outproj_call(o_g, o_n, x2, g1, pw['w_out'][l], pw['norm2'][l], sc2, sh2, tm, length)
    tm_f = min(1024, length)
    x_out, u_keep = _ffn_call(h2, x_mid, g2, pw['w_up'][l], pw['w_gate'][l], pw['w_down'][l],
                              pw['conv_w'][l], pw['conv_b'][l], tm_f, length)
    conv = u_keep.reshape(nb, length // tm_f, 8, D_FF)[:, -1, 8 - (CONV_W - 1):, :]
    wl = min(WINDOW, length)
    return (x_out.reshape(nb, length, D_MODEL),
            jnp.swapaxes(s_t, -1, -2),
            kv_new.reshape(nb, length, 4, NSA_G, NSA_DH),
            win_new.reshape(nb, length, 2, NSA_G, NSA_DH)[:, length - wl:],
            conv)


SAMPLE_ROWS = 16


def _page_dma(pt_ref, ckv_hbm, buf, sem, layer, batch, slot, row0, n_pages, start):
    rows = buf.shape[1]

    def body(p, carry):
        cp = pltpu.make_async_copy(
            ckv_hbm.at[layer, pt_ref[batch, p], pl.ds(row0, rows), :],
            buf.at[slot, :, pl.ds(pl.multiple_of(p * PAGE_SIZE, PAGE_SIZE), PAGE_SIZE)],
            sem.at[slot])
        if start:
            cp.start()
        else:
            cp.wait()
        return carry

    lax.fori_loop(0, n_pages, body, 0)


def _gather_pages(pt_ref, ckv_hbm, buf, sem, layer, row0, n_pages):
    b = pl.program_id(0)
    dma = functools.partial(_page_dma, pt_ref, ckv_hbm, buf, sem, layer)

    @pl.when(b == 0)
    def _():
        dma(0, 0, row0, n_pages, True)

    @pl.when(b + 1 < pl.num_programs(0))
    def _():
        dma(b + 1, (b + 1) % 2, row0, n_pages, True)

    slot = b % 2
    dma(b, slot, row0, n_pages, False)
    return slot


def _q_stack(q_ref, g):
    return jnp.concatenate(
        [q_ref[:, (g * NSA_REP + r) * NSA_DH:(g * NSA_REP + r + 1) * NSA_DH] for r in range(NSA_REP)], axis=0)


def _slope_rows(g, rows):
    return jnp.concatenate([jnp.full((rows, 1), SLOPES[g * NSA_REP + r], F32) for r in range(NSA_REP)], axis=0)


def _s1_body(pt_ref, ckv_hbm, q_ref, w_ref, g_ref, oc_ref, sel_ref, buf, sem, rows_scr, imp_scr, sc_scr,
             *, layer, n_pages, past_len):
    slot = _gather_pages(pt_ref, ckv_hbm, buf, sem, layer, 0, n_pages)
    ts = SAMPLE_ROWS
    nblk = past_len // CMP_STRIDE
    ns = past_len // SEL_BLOCK + 1
    nsp = _round_up(ns, 128)
    n_sel = min(SEL_TOP_N, ns)
    chunk = min(512, past_len)
    for kind in range(2):
        for c in range(past_len // chunk):
            rows_scr[kind, c * chunk:(c + 1) * chunk, :] = (
                buf[slot, kind * KV_WIDTH:(kind + 1) * KV_WIDTH, c * chunk:(c + 1) * chunk].T)

    def loader(kind):
        return lambda j: rows_scr[kind, pl.ds(j, nblk, stride=CMP_STRIDE), :]

    kc = _seg_rms64(_compress_rows(loader(0), w_ref, 0, nblk), g_ref[...]).astype(BF16)
    vc = _compress_rows(loader(1), w_ref, 1, nblk).astype(BF16)

    t_col = past_len + lax.broadcasted_iota(jnp.int32, (ts, 1), 0)
    ec = CMP_STRIDE * lax.broadcasted_iota(jnp.int32, (1, nblk), 1) + (CMP_LEN - 1)
    dist = t_col - ec
    valid4 = _tile_rows(dist >= 0, NSA_REP)
    distf4 = _tile_rows(dist.astype(F32), NSA_REP)
    heads, imps = [], []
    for g in range(NSA_G):
        ks = slice(g * NSA_DH, (g + 1) * NSA_DH)
        s = _dot_nt(_q_stack(q_ref, g), kc[:, ks]) - _slope_rows(g, ts) * distf4
        p = _softmax_rows(s, valid4)
        o_c = _dot(p.astype(BF16), vc[:, ks])
        imp = p[0:ts]
        for r in range(1, NSA_REP):
            imp = imp + p[r * ts:(r + 1) * ts]
        imps.append(imp)
        heads += [o_c[r * ts:(r + 1) * ts] for r in range(NSA_REP)]
    oc_ref[...] = jnp.concatenate(heads, axis=-1)
    imp = jnp.concatenate(imps + [jnp.zeros((128 - NSA_G * ts, nblk), F32)], axis=0)
    t_row = past_len + lax.broadcasted_iota(jnp.int32, (1, 128), 1) % ts
    sel = _select_blocks(imp, imp_scr, sc_scr, t_row, ns, n_sel)
    for g in range(NSA_G):
        sel_ref[:, g * nsp:(g + 1) * nsp] = sel[g * ts:(g + 1) * ts, :].astype(F32)


def _s1_call(page_table, ckv, q_bf, w_cmp, kcg, layer, past_len):
    nbatch, n_pages = page_table.shape
    ts = SAMPLE_ROWS
    ns = past_len // SEL_BLOCK + 1
    nsp = _round_up(ns, 128)
    body = functools.partial(_s1_body, layer=layer, n_pages=n_pages, past_len=past_len)
    grid_spec = pltpu.PrefetchScalarGridSpec(
        num_scalar_prefetch=1,
        grid=(nbatch,),
        in_specs=[pl.BlockSpec(memory_space=pl.ANY),
                  pl.BlockSpec((ts, NSA_WIDTH), lambda b, pt: (b, 0)),
                  pl.BlockSpec((2, CMP_STRIDE, KV_WIDTH, 2 * KV_WIDTH), lambda b, pt: (0, 0, 0, 0)),
                  pl.BlockSpec((1, KV_WIDTH), lambda b, pt: (0, 0))],
        out_specs=[pl.BlockSpec((ts, NSA_WIDTH), lambda b, pt: (b, 0)),
                   pl.BlockSpec((ts, NSA_G * nsp), lambda b, pt: (b, 0))],
        scratch_shapes=[pltpu.VMEM((2, 2 * KV_WIDTH, past_len), F32),
                        pltpu.SemaphoreType.DMA((2,)),
                        pltpu.VMEM((2, past_len, KV_WIDTH), F32),
                        pltpu.VMEM((_imp_scratch_rows(ns), 128), F32),
                        pltpu.VMEM((_sel_rows(ns), 128), F32)])
    return pl.pallas_call(
        body,
        grid_spec=grid_spec,
        out_shape=[jax.ShapeDtypeStruct((nbatch * ts, NSA_WIDTH), F32),
                   jax.ShapeDtypeStruct((nbatch * ts, NSA_G * nsp), F32)],
        compiler_params=_cparams("arbitrary"),
        name="nsa_sample_select",
    )(page_table, ckv, q_bf, w_cmp, kcg)


def _s2_body(pt_ref, ckv_hbm, q_ref, misc_ref, oc_ref, sel_ref, kvn_ref, cwin_ref, wn_ref, e_ref, nn_ref,
             o_ref, buf, sem, *, layer, n_pages, past_len, wb, tk):
    slot = _gather_pages(pt_ref, ckv_hbm, buf, sem, layer, 2 * KV_WIDTH, n_pages)
    ts = SAMPLE_ROWS
    rows = NSA_HEADS * ts
    ns = past_len // SEL_BLOCK + 1
    nsp = sel_ref.shape[1] // NSA_G
    gates = jax.nn.sigmoid(misc_ref[:, MISC_NG:MISC_NG + 3 * NSA_HEADS])
    half = lax.broadcasted_iota(jnp.int32, (ts, KV_WIDTH), 1) // NSA_DH
    q_rows, slopes, sel_rows = [], [], []
    for h in range(NSA_HEADS):
        g = h // NSA_REP
        pair = q_ref[:, (h // 2) * KV_WIDTH:(h // 2 + 1) * KV_WIDTH].astype(F32)
        if h % 2 != g:
            pair = pltpu.roll(pair, NSA_DH, 1)
        q_rows.append(jnp.where(half == g, pair, 0.0))
        slopes.append(jnp.full((ts, 1), SLOPES[h], F32))
        sel_rows.append(sel_ref[:, g * nsp:(g + 1) * nsp])
    q_all = jnp.concatenate(q_rows, axis=0).astype(BF16)
    slope_rows = jnp.concatenate(slopes, axis=0)
    sel_all = jnp.concatenate(sel_rows, axis=0)
    sel_bf = sel_all.astype(BF16)
    t_col = past_len + lax.broadcasted_iota(jnp.int32, (rows, 1), 0) % ts
    zpad = jnp.zeros((128 - ts, KV_WIDTH), BF16)

    def rel(k0, n):
        return k0 - past_len + lax.broadcasted_iota(jnp.int32, (1, n), 1)

    def scores(keys_t, r):
        return _dot(q_all, keys_t) + slope_rows * r.astype(F32)

    def scores_new(ref):
        keys = jnp.concatenate([ref[:, :KV_WIDTH], zpad], axis=0)
        return _dot_nt(q_all, keys) + slope_rows * rel(past_len, 128).astype(F32)

    def pv_new(ref):
        vals = jnp.concatenate([ref[:, KV_WIDTH:], zpad], axis=0)
        return lambda e: _dot(e, vals)

    d_new = t_col - (past_len + lax.broadcasted_iota(jnp.int32, (1, 128), 1))

    def kt_body(kt, carry):
        k0 = pl.multiple_of(kt * tk, tk)
        valid = _dot(sel_bf, e_ref[:, pl.ds(k0, tk)]) > 0.5
        v_t = buf[slot, KV_WIDTH:, pl.ds(k0, tk)].astype(BF16)
        s = scores(buf[slot, :KV_WIDTH, pl.ds(k0, tk)].astype(BF16), rel(k0, tk))
        return _online_update(carry, s, valid, lambda e: _dot_nt(e, v_t))

    carry = lax.fori_loop(0, past_len // tk, kt_body, _online_init(rows, KV_WIDTH))
    valid = (sel_all[:, ns - 1:ns] > 0.5) & (d_new >= 0)
    o_s = _online_finish(_online_update(carry, scores_new(kvn_ref), valid, pv_new(kvn_ref)))

    d_win = t_col - (past_len - wb + lax.broadcasted_iota(jnp.int32, (1, wb), 1))
    vw_t = cwin_ref[KV_WIDTH:, :].astype(BF16)
    carry = _online_update(_online_init(rows, KV_WIDTH),
                           scores(cwin_ref[:KV_WIDTH, :].astype(BF16), rel(past_len - wb, wb)),
                           (d_win >= 0) & (d_win <= WINDOW), lambda e: _dot_nt(e, vw_t))
    valid = (d_new >= 0) & (d_new <= WINDOW)
    o_w = _online_finish(_online_update(carry, scores_new(wn_ref), valid, pv_new(wn_ref)))

    heads = []
    for h in range(NSA_HEADS):
        g = h // NSA_REP
        rs = slice(h * ts, (h + 1) * ts)
        ls = slice(g * NSA_DH, (g + 1) * NSA_DH)
        heads.append(gates[:, h:h + 1] * oc_ref[:, h * NSA_DH:(h + 1) * NSA_DH]
                     + gates[:, NSA_HEADS + h:NSA_HEADS + h + 1] * o_s[rs, ls]
                     + gates[:, 2 * NSA_HEADS + h:2 * NSA_HEADS + h + 1] * o_w[rs, ls])
    o_ref[...] = _rms_rows(jnp.concatenate(heads, axis=-1), nn_ref[...])


def _s2_call(page_table, ckv_t, q_bf, z, o_c, sel, kvs_bf, cwin_t, win_bf, nn, layer, past_len):
    nbatch, n_pages = page_table.shape
    ts = SAMPLE_ROWS
    wb = cwin_t.shape[3]
    tk = min(2048, past_len)
    nsp = sel.shape[1] // NSA_G
    body = functools.partial(_s2_body, layer=layer, n_pages=n_pages, past_len=past_len, wb=wb, tk=tk)
    blk = lax.broadcasted_iota(jnp.int32, (nsp, past_len), 1) // SEL_BLOCK
    expand = (lax.broadcasted_iota(jnp.int32, (nsp, past_len), 0) == blk).astype(BF16)

    def rows(width, col_block=0):
        return pl.BlockSpec((ts, width), lambda b, pt, c=col_block: (b, c))

    grid_spec = pltpu.PrefetchScalarGridSpec(
        num_scalar_prefetch=1,
        grid=(nbatch,),
        in_specs=[pl.BlockSpec(memory_space=pl.ANY),
                  rows(NSA_WIDTH), rows(128, C_MISC // 128), rows(NSA_WIDTH), rows(sel.shape[1]),
                  rows(2 * KV_WIDTH),
                  pl.BlockSpec((None, None, 2 * KV_WIDTH, wb), lambda b, pt: (layer, b, 0, 0)),
                  rows(2 * KV_WIDTH),
                  pl.BlockSpec((nsp, past_len), lambda b, pt: (0, 0)),
                  pl.BlockSpec((1, NSA_WIDTH), lambda b, pt: (0, 0))],
        out_specs=rows(NSA_WIDTH),
        scratch_shapes=[pltpu.VMEM((2, 2 * KV_WIDTH, past_len), F32),
                        pltpu.SemaphoreType.DMA((2,))])
    return pl.pallas_call(
        body,
        grid_spec=grid_spec,
        out_shape=jax.ShapeDtypeStruct((nbatch * ts, NSA_WIDTH), F32),
        compiler_params=_cparams("arbitrary"),
        name="nsa_sample_attn",
    )(page_table, ckv_t, q_bf, z, o_c, sel, kvs_bf, cwin_t, win_bf, expand, nn)


def _layer_sample(x, mod, pw, l, t_real, ckv, cwin, page_table, s0, conv_prev):
    nb, ts, _ = x.shape
    m = nb * ts
    past_len = page_table.shape[1] * PAGE_SIZE
    sh1, sc1, g1, sh2, sc2, g2 = [jnp.repeat(a, ts, axis=0) for a in _split_mod(mod, (nb, D_MODEL))]
    x2 = x.reshape(m, D_MODEL)
    z = _inproj_call(x2, pw['norm1'][l], sc1, sh1, pw['w_in'][l], m, ts)
    o_g, s_t = _gla_call(z, pw['wa'][l], pw['ba'][l], pw['gla_norm'][l], s0, nb, ts, ts, t_real)
    q_bf, kv_new, win_new, kvs_bf, win_bf = _nsa_prep_call(
        z, pw['q_norm'][l], pw['ks_norm'][l], pw['kw_norm'][l], m)
    o_c, sel = _s1_call(page_table, ckv, q_bf, pw['w_cmp'][l], pw['kc_norm'][l], l, past_len)
    o_n = _s2_call(page_table, ckv, q_bf, z, o_c, sel, kvs_bf, cwin, win_bf, pw['nsa_norm'][l], l, past_len)
    x_mid, h2 = _outproj_call(o_g, o_n, x2, g1, pw['w_out'][l], pw['norm2'][l], sc2, sh2, m, ts)
    zrow = jnp.zeros((nb, ts, D_FF), F32)
    p1 = zrow.at[:, 0].set(conv_prev[:, 1]).reshape(m, D_FF)
    p2 = zrow.at[:, 0].set(conv_prev[:, 0]).at[:, 1].set(conv_prev[:, 1]).reshape(m, D_FF)
    x_out, u = _ffn_call(h2, x_mid, g2, pw['w_up'][l], pw['w_gate'][l], pw['w_down'][l],
                         pw['conv_w'][l], pw['conv_b'][l], m, ts, prev=(p1, p2))
    conv = u.reshape(nb, ts, D_FF)[:, t_real - (CONV_W - 1):t_real]
    return (x_out.reshape(nb, ts, D_MODEL),
            jnp.swapaxes(s_t, -1, -2),
            kv_new.reshape(nb, ts, 4, NSA_G, NSA_DH)[:, :t_real],
            win_new.reshape(nb, ts, 2, NSA_G, NSA_DH)[:, :t_real],
            conv)


def kernel(x_prompt, x_sample, cache_kv, cache_win, state_gla, state_conv, page_table, c_prompt, c_sample,
           norm1, norm2, w_ada, b_ada, w_in, w_gla_a, b_gla_a, gla_norm, q_norm, kc_norm, ks_norm, kw_norm,
           w_ck, w_cv, nsa_norm, w_out, w_up, w_gate, conv_w, conv_b, w_down):
    depth = w_in.shape[0]
    bp, bs, t_s = x_prompt.shape[0], x_sample.shape[0], x_sample.shape[1]
    assert CONV_W - 1 <= t_s <= SAMPLE_ROWS
    pw = _prep_weights({'w_in': w_in, 'w_gla_a': w_gla_a, 'b_gla_a': b_gla_a, 'w_ck': w_ck, 'w_cv': w_cv,
                        'norm1': norm1, 'norm2': norm2, 'gla_norm': gla_norm, 'nsa_norm': nsa_norm,
                        'q_norm': q_norm, 'kc_norm': kc_norm, 'ks_norm': ks_norm, 'kw_norm': kw_norm,
                        'w_out': w_out, 'w_up': w_up, 'w_gate': w_gate, 'w_down': w_down,
                        'conv_w': conv_w, 'conv_b': conv_b})
    bc = _round_up(bp + bs, 8)
    c_all = jnp.concatenate([c_prompt, c_sample, jnp.zeros((bc - bp - bs, D_MODEL), F32)], axis=0)
    mod_all = _mod_call(c_all, w_ada, b_ada)
    n_pool = cache_kv.shape[1]
    wb = cache_win.shape[2]
    ckv = jnp.transpose(cache_kv, (0, 1, 3, 4, 5, 2)).reshape(depth, n_pool, 4 * KV_WIDTH, PAGE_SIZE)
    cwin = jnp.transpose(cache_win, (0, 1, 3, 4, 5, 2)).reshape(depth, bs, 2 * KV_WIDTH, wb)
    xp = x_prompt
    xs = jnp.pad(x_sample, ((0, 0), (0, SAMPLE_ROWS - t_s), (0, 0)))
    outs_p, outs_s = [], []
    for l in range(depth):
        res = _layer_prompt(xp, mod_all[l, :bp], pw, l)
        xp = res[0]
        outs_p.append(res[1:])
        res = _layer_sample(xs, mod_all[l, bp:bp + bs], pw, l, t_s, ckv, cwin, page_table,
                            state_gla[l], state_conv[l])
        xs = res[0]
        outs_s.append(res[1:])
    stack = lambda outs, k: jnp.stack([o[k] for o in outs])
    win_s = jnp.stack([jnp.concatenate([cache_win[l][:, t_s:], outs_s[l][2]], axis=1) for l in range(depth)])
    return (xp, xs[:, :t_s], stack(outs_p, 1), stack(outs_s, 1), stack(outs_p, 2), win_s,
            stack(outs_p, 0), stack(outs_s, 0), stack(outs_p, 3), stack(outs_s, 3))
```

```python
import functools

import numpy as np
import jax
import jax.numpy as jnp
from jax import lax
from jax.experimental import pallas as pl
from jax.experimental.pallas import tpu as pltpu

F32 = jnp.float32
BF16 = jnp.bfloat16

D_MODEL = 1024
GLA_HEADS = 4
GLA_DK = 64
GLA_DV = 128
GLA_GATE_RANK = 16
GLA_TAU = 16.0
GLA_CHUNK = 16
GLA_WIDTH = GLA_HEADS * GLA_DV
NSA_HEADS = 8
NSA_G = 2
NSA_REP = NSA_HEADS // NSA_G
NSA_DH = 64
NSA_WIDTH = NSA_HEADS * NSA_DH
KV_WIDTH = NSA_G * NSA_DH
CMP_STRIDE = 16
CMP_LEN = 2 * CMP_STRIDE
SEL_BLOCK = 64
SEL_PER = SEL_BLOCK // CMP_STRIDE
SEL_TOP_N = 16
WINDOW = 512
D_FF = 2816
CONV_W = 3
RMS_EPS = 1e-6
BIG = 1e30
PAGE_SIZE = 128

IN_SIZES = (GLA_HEADS * GLA_DK, GLA_HEADS * GLA_DK, GLA_WIDTH, GLA_GATE_RANK, GLA_WIDTH,
            NSA_WIDTH, KV_WIDTH, KV_WIDTH, KV_WIDTH, KV_WIDTH, KV_WIDTH, KV_WIDTH, 3 * NSA_HEADS)

C_GQ, C_GK, C_GV, C_GR, C_NQ, C_KV, C_WIN, C_MISC = 0, 256, 512, 1024, 1536, 2048, 2560, 2816
MISC_GA, MISC_NG = 0, GLA_GATE_RANK
N_Z = 3072

VMEM_LIMIT_BYTES = 56 * 1024 * 1024
SLOPES = tuple(2.0 ** (-8.0 * h / NSA_HEADS) for h in range(1, NSA_HEADS + 1))


def _cparams(*sem):
    return pltpu.CompilerParams(dimension_semantics=sem, vmem_limit_bytes=VMEM_LIMIT_BYTES)


def _dot(a, b):
    return jnp.dot(a, b, preferred_element_type=F32)


def _dot_nt(a, b):
    return lax.dot_general(a, b, (((1,), (1,)), ((), ())), preferred_element_type=F32)


def _rms_rows(x, gain):
    return x * lax.rsqrt(jnp.mean(x * x, axis=-1, keepdims=True) + RMS_EPS) * gain


def _seg_rms64(x, gain):
    rows, width = x.shape
    lo_lane = (lax.broadcasted_iota(jnp.int32, (1, 128), 1) < 64)
    outs = []
    for c in range(width // 128):
        xb = x[:, c * 128:(c + 1) * 128]
        sq = xb * xb
        lo = jnp.sum(jnp.where(lo_lane, sq, 0.0), axis=-1, keepdims=True)
        hi = jnp.sum(jnp.where(lo_lane, 0.0, sq), axis=-1, keepdims=True)
        ms = jnp.where(lo_lane, lo, hi) * (1.0 / 64.0)
        outs.append(xb * lax.rsqrt(ms + RMS_EPS))
    y = outs[0] if len(outs) == 1 else jnp.concatenate(outs, axis=-1)
    return y * gain


def _mod_body(c_ref, w_ref, b_ref, o_ref):
    c = c_ref[...]
    a = c * jax.nn.sigmoid(c)
    o_ref[...] = _dot(a.astype(BF16), w_ref[...].astype(BF16)) + b_ref[...]


def _mod_call(c_all, w_ada, b_ada):
    depth, d, n = w_ada.shape
    bc = c_all.shape[0]
    tn = 1536
    return pl.pallas_call(
        _mod_body,
        grid=(depth, n // tn),
        in_specs=[pl.BlockSpec((bc, d), lambda l, j: (0, 0)),
                  pl.BlockSpec((None, d, tn), lambda l, j: (l, 0, j)),
                  pl.BlockSpec((None, 1, tn), lambda l, j: (l, 0, j))],
        out_specs=pl.BlockSpec((None, bc, tn), lambda l, j: (l, 0, j)),
        out_shape=jax.ShapeDtypeStruct((depth, bc, n), F32),
        compiler_params=_cparams("parallel", "parallel"),
        name="adaln_mod",
    )(c_all, w_ada, b_ada.reshape(depth, 1, n))


def _inproj_body(x_ref, g_ref, sc_ref, sh_ref, w_ref, z_ref, h_scr):
    @pl.when(pl.program_id(1) == 0)
    def _():
        y = _rms_rows(x_ref[...], g_ref[...])
        h_scr[...] = (y * (1.0 + sc_ref[...]) + sh_ref[...]).astype(BF16)

    z_ref[...] = _dot(h_scr[...], w_ref[...])


def _mod_spec(mod, tm, rows_per_batch):
    if mod.ndim == 3:
        return pl.BlockSpec((None, 1, D_MODEL), lambda i, *_: ((i * tm) // rows_per_batch, 0, 0))
    return pl.BlockSpec((tm, D_MODEL), lambda i, *_: (i, 0))


def _inproj_call(x2d, gain, sc, sh, w_bf, tm, rows_per_batch):
    m = x2d.shape[0]
    tn = 1024
    return pl.pallas_call(
        _inproj_body,
        grid=(m // tm, N_Z // tn),
        in_specs=[pl.BlockSpec((tm, D_MODEL), lambda i, j: (i, 0)),
                  pl.BlockSpec((1, D_MODEL), lambda i, j: (0, 0)),
                  _mod_spec(sc, tm, rows_per_batch),
                  _mod_spec(sh, tm, rows_per_batch),
                  pl.BlockSpec((D_MODEL, tn), lambda i, j: (0, j))],
        out_specs=pl.BlockSpec((tm, tn), lambda i, j: (i, j)),
        out_shape=jax.ShapeDtypeStruct((m, N_Z), F32),
        scratch_shapes=[pltpu.VMEM((tm, D_MODEL), BF16)],
        compiler_params=_cparams("parallel", "arbitrary"),
        name="inproj",
    )(x2d, gain, sc, sh, w_bf)


def _outproj_body(og_ref, on_ref, x_ref, g1_ref, w_ref, n2_ref, sc_ref, sh_ref, xo_ref, h2_ref):
    mix = (_dot(og_ref[...].astype(BF16), w_ref[:GLA_WIDTH, :])
           + _dot(on_ref[...].astype(BF16), w_ref[GLA_WIDTH:, :]))
    xn = x_ref[...] + g1_ref[...] * mix
    xo_ref[...] = xn
    y = _rms_rows(xn, n2_ref[...])
    h2_ref[...] = (y * (1.0 + sc_ref[...]) + sh_ref[...]).astype(BF16)


def _outproj_call(o_g, o_n, x2d, g1, w_bf, gain2, sc2, sh2, tm, rows_per_batch):
    m = x2d.shape[0]
    return pl.pallas_call(
        _outproj_body,
        grid=(m // tm,),
        in_specs=[pl.BlockSpec((tm, GLA_WIDTH), lambda i: (i, 0)),
                  pl.BlockSpec((tm, NSA_WIDTH), lambda i: (i, 0)),
                  pl.BlockSpec((tm, D_MODEL), lambda i: (i, 0)),
                  _mod_spec(g1, tm, rows_per_batch),
                  pl.BlockSpec((D_MODEL, D_MODEL), lambda i: (0, 0)),
                  pl.BlockSpec((1, D_MODEL), lambda i: (0, 0)),
                  _mod_spec(sc2, tm, rows_per_batch),
                  _mod_spec(sh2, tm, rows_per_batch)],
        out_specs=[pl.BlockSpec((tm, D_MODEL), lambda i: (i, 0)),
                   pl.BlockSpec((tm, D_MODEL), lambda i: (i, 0))],
        out_shape=[jax.ShapeDtypeStruct((m, D_MODEL), F32),
                   jax.ShapeDtypeStruct((m, D_MODEL), BF16)],
        compiler_params=_cparams("parallel"),
        name="outproj",
    )(o_g, o_n, x2d, g1, w_bf, gain2, sc2, sh2)


def _gelu_tanh(x):
    return x * (0.5 * (1.0 + jnp.tanh(0.7978845608028654 * (x + 0.044715 * (x * x * x)))))


def _ffn_body(*refs, tm, period, per_row_prev, keep):
    if per_row_prev:
        (h2_ref, x_ref, g2_ref, wu_ref, wg_ref, wd_ref, cw_ref, cb_ref, p1_ref, p2_ref,
         xo_ref, uk_ref, acc_scr, u_scr, g_scr) = refs
    else:
        (h2_ref, x_ref, g2_ref, wu_ref, wg_ref, wd_ref, cw_ref, cb_ref,
         xo_ref, uk_ref, acc_scr, u_scr, g_scr, carry_scr) = refs
    i, f = pl.program_id(0), pl.program_id(1)
    new, old = f % 2, (f + 1) % 2

    @pl.when(f == 0)
    def _():
        acc_scr[...] = jnp.zeros(acc_scr.shape, F32)
        u_scr[1] = jnp.zeros(u_scr.shape[1:], F32)
        g_scr[1] = jnp.zeros(g_scr.shape[1:], F32)

    if not per_row_prev:
        @pl.when((f == 0) & (i == 0))
        def _():
            carry_scr[...] = jnp.zeros(carry_scr.shape, F32)

    u = u_scr[old]
    gt = g_scr[old]
    row = lax.broadcasted_iota(jnp.int32, (tm, 1), 0)
    if per_row_prev:
        t = row % period
        m1, m2 = t >= 1, t >= 2
        prev1, prev2 = p1_ref[...], p2_ref[...]
    else:
        ft = jnp.maximum(f - 1, 0)
        fresh = ((i * tm) % period) == 0
        saved = carry_scr[ft]
        c = jnp.where(fresh, 0.0, saved)
        m1, m2 = row >= 1, row >= 2
        prev1 = c[7:8, :]
        prev2 = jnp.where(row == 0, c[6:7, :], c[7:8, :])
        carry_scr[ft] = jnp.where(f > 0, u[tm - 8:, :], saved)
    u_m1 = jnp.where(m1, pltpu.roll(u, 1, 0), prev1)
    u_m2 = jnp.where(m2, pltpu.roll(u, 2, 0), prev2)
    conv = u_m2 * cw_ref[0:1, :] + u_m1 * cw_ref[1:2, :] + u * cw_ref[2:3, :] + cb_ref[...]
    act = _gelu_tanh(conv) * gt
    acc_scr[...] += _dot(act.astype(BF16), wd_ref[...])
    uk_ref[...] = u[tm - keep:, :]

    h2 = h2_ref[...]
    u_scr[new] = _dot(h2, wu_ref[...])
    g_scr[new] = _dot(h2, wg_ref[...])

    @pl.when(f == pl.num_programs(1) - 1)
    def _():
        xo_ref[...] = x_ref[...] + g2_ref[...] * acc_scr[...]


def _ffn_call(h2, x2d, g2, wu, wg, wd, cw, cb, tm, period, prev=None):
    m = x2d.shape[0]
    tf = 256
    nf = D_FF // tf
    per_row_prev = prev is not None
    keep = tm if per_row_prev else 8
    body = functools.partial(_ffn_body, tm=tm, period=period, per_row_prev=per_row_prev, keep=keep)
    cur = lambda f: jnp.minimum(f, nf - 1)
    prv = lambda f: jnp.maximum(f - 1, 0)
    in_specs = [pl.BlockSpec((tm, D_MODEL), lambda i, f: (i, 0)),
                pl.BlockSpec((tm, D_MODEL), lambda i, f: (i, 0)),
                _mod_spec(g2, tm, period),
                pl.BlockSpec((D_MODEL, tf), lambda i, f: (0, cur(f))),
                pl.BlockSpec((D_MODEL, tf), lambda i, f: (0, cur(f))),
                pl.BlockSpec((tf, D_MODEL), lambda i, f: (prv(f), 0)),
                pl.BlockSpec((CONV_W, tf), lambda i, f: (0, prv(f))),
                pl.BlockSpec((1, tf), lambda i, f: (0, prv(f)))]
    args = [h2, x2d, g2, wu, wg, wd, cw, cb]
    scratch = [pltpu.VMEM((tm, D_MODEL), F32), pltpu.VMEM((2, tm, tf), F32), pltpu.VMEM((2, tm, tf), F32)]
    if per_row_prev:
        in_specs += [pl.BlockSpec((tm, tf), lambda i, f: (i, prv(f)))] * 2
        args += list(prev)
    else:
        scratch.append(pltpu.VMEM((nf, 8, tf), F32))
    return pl.pallas_call(
        body,
        grid=(m // tm, nf + 1),
        in_specs=in_specs,
        out_specs=[pl.BlockSpec((tm, D_MODEL), lambda i, f: (i, 0)),
                   pl.BlockSpec((keep, tf), lambda i, f: (i, prv(f)))],
        out_shape=[jax.ShapeDtypeStruct((m, D_MODEL), F32),
                   jax.ShapeDtypeStruct((m // tm * keep, D_FF), F32)],
        scratch_shapes=scratch,
        compiler_params=_cparams("arbitrary", "arbitrary"),
        name="convffn",
    )(*args)


GLA_ROWS = 128


def _log_sigmoid(x):
    return jnp.minimum(x, 0.0) - jnp.log1p(jnp.exp(-jnp.abs(x)))


def _gla_body(gq_ref, gk_ref, gv_ref, gr_ref, misc_ref, wa_ref, ba_ref, gn_ref, s0_ref,
              o_ref, sT_out_ref, sT_scr, *, rows_in, t_valid):
    R, C = GLA_ROWS, GLA_CHUNK
    nchunk = R // C
    n_live = min(nchunk, -(-min(t_valid, rows_in) // C))
    step = pl.program_id(1)

    @pl.when(step == 0)
    def _():
        for h in range(GLA_HEADS):
            sT_scr[h] = s0_ref[h].T

    def padded(ref):
        x = ref[...]
        if rows_in < R:
            x = jnp.concatenate([x, jnp.zeros((R - rows_in, x.shape[1]), x.dtype)], axis=0)
        return x

    row = lax.broadcasted_iota(jnp.int32, (R, 1), 0)
    live = row < t_valid
    q = padded(gq_ref) * (GLA_DK ** -0.5)
    k = jnp.where(live, padded(gk_ref), 0.0)
    v = jnp.where(live, padded(gv_ref), 0.0)
    xa = _dot(padded(misc_ref).astype(BF16), wa_ref[...]) + ba_ref[...]
    la = jnp.where(live, _log_sigmoid(xa) * (1.0 / GLA_TAU), 0.0)

    rc = row % C
    b = la
    shift = 1
    while shift < C:
        b = b + jnp.where(rc >= shift, pltpu.roll(b, shift, 0), 0.0)
        shift *= 2
    b3 = b.reshape(nchunk, C, GLA_HEADS * GLA_DK)
    b_last = jnp.broadcast_to(b3[:, C - 1:C, :], b3.shape).reshape(R, GLA_HEADS * GLA_DK)
    qe = (q * jnp.exp(b)).astype(BF16)
    ke = (k * jnp.exp(-b)).astype(BF16)
    kd = (k * jnp.exp(b_last - b)).astype(BF16)
    decay = jnp.exp(b_last)

    ri = lax.broadcasted_iota(jnp.int32, (R, R), 0)
    ci = lax.broadcasted_iota(jnp.int32, (R, R), 1)
    causal = (ri // C == ci // C) & (ci <= ri)
    col_chunk = lax.broadcasted_iota(jnp.int32, (GLA_DV, R), 1) // C

    outs = []
    for h in range(GLA_HEADS):
        ks = slice(h * GLA_DK, (h + 1) * GLA_DK)
        vs = slice(h * GLA_DV, (h + 1) * GLA_DV)
        v_h = v[:, vs]
        att = jnp.where(causal, _dot_nt(qe[:, ks], ke[:, ks]), 0.0)
        o_h = _dot(att.astype(BF16), v_h.astype(BF16))
        vT = v_h.T
        stack = jnp.concatenate([jnp.where(col_chunk == n, vT, 0.0) for n in range(n_live)], axis=0)
        incT = _dot(stack.astype(BF16), kd[:, ks])
        sT = sT_scr[h]
        inter = []
        for n in range(n_live):
            inter.append(_dot_nt(qe[n * C:(n + 1) * C, ks], sT.astype(BF16)))
            sT = decay[n * C + C - 1:n * C + C, ks] * sT + incT[n * GLA_DV:(n + 1) * GLA_DV, :]
        sT_scr[h] = sT
        if n_live < nchunk:
            inter.append(jnp.zeros(((nchunk - n_live) * C, GLA_DV), F32))
        o_h = o_h + jnp.concatenate(inter, axis=0)
        outs.append(_rms_rows(o_h, gn_ref[:, vs]))
    o = jnp.concatenate(outs, axis=-1)
    gr = padded(gr_ref)
    o = o * (gr * jax.nn.sigmoid(gr))
    o_ref[...] = o[:rows_in, :]

    @pl.when(step == pl.num_programs(1) - 1)
    def _():
        sT_out_ref[...] = sT_scr[...]


def _gla_call(z, wa_pad_bf, ba, gnorm, s0, n_batch, rows_per_batch, rows_in, t_valid):
    steps = rows_per_batch // rows_in
    body = functools.partial(_gla_body, rows_in=rows_in, t_valid=t_valid)

    def zspec(col, width):
        return pl.BlockSpec((rows_in, width), lambda b, s, c=col // width: (b * steps + s, c))

    return pl.pallas_call(
        body,
        grid=(n_batch, steps),
        in_specs=[zspec(C_GQ, 256), zspec(C_GK, 256), zspec(C_GV, 512), zspec(C_GR, 512),
                  zspec(C_MISC, 128),
                  pl.BlockSpec((128, GLA_HEADS * GLA_DK), lambda b, s: (0, 0)),
                  pl.BlockSpec((1, GLA_HEADS * GLA_DK), lambda b, s: (0, 0)),
                  pl.BlockSpec((1, GLA_WIDTH), lambda b, s: (0, 0)),
                  pl.BlockSpec((None, GLA_HEADS, GLA_DK, GLA_DV), lambda b, s: (b, 0, 0, 0))],
        out_specs=[pl.BlockSpec((rows_in, GLA_WIDTH), lambda b, s: (b * steps + s, 0)),
                   pl.BlockSpec((None, GLA_HEADS, GLA_DV, GLA_DK), lambda b, s: (b, 0, 0, 0))],
        out_shape=[jax.ShapeDtypeStruct((n_batch * rows_per_batch, GLA_WIDTH), F32),
                   jax.ShapeDtypeStruct((n_batch, GLA_HEADS, GLA_DV, GLA_DK), F32)],
        scratch_shapes=[pltpu.VMEM((GLA_HEADS, GLA_DV, GLA_DK), F32)],
        compiler_params=_cparams("parallel", "arbitrary"),
        name="gla",
    )(z, z, z, z, z, wa_pad_bf, ba, gnorm, s0)


def _nsa_prep_body(nq_ref, kv_ref, win_ref, qg_ref, ksg_ref, kwg_ref,
                   q_out, kv_out, win_out, kvs_bf, win_bf):
    q = _seg_rms64(nq_ref[...], qg_ref[...]) * (NSA_DH ** -0.5)
    q_out[...] = q.astype(BF16)
    kv = kv_ref[...]
    ksn = _seg_rms64(kv[:, 2 * KV_WIDTH:3 * KV_WIDTH], ksg_ref[...])
    vs = kv[:, 3 * KV_WIDTH:]
    kv_out[:, :2 * KV_WIDTH] = kv[:, :2 * KV_WIDTH]
    kv_out[:, 2 * KV_WIDTH:3 * KV_WIDTH] = ksn
    kv_out[:, 3 * KV_WIDTH:] = vs
    kvs_bf[:, :KV_WIDTH] = ksn.astype(BF16)
    kvs_bf[:, KV_WIDTH:] = vs.astype(BF16)
    w = win_ref[...]
    kwn = _seg_rms64(w[:, :KV_WIDTH], kwg_ref[...])
    win_out[:, :KV_WIDTH] = kwn
    win_out[:, KV_WIDTH:] = w[:, KV_WIDTH:]
    win_bf[:, :KV_WIDTH] = kwn.astype(BF16)
    win_bf[:, KV_WIDTH:] = w[:, KV_WIDTH:].astype(BF16)


def _nsa_prep_call(z, qg, ksg, kwg, tm):
    m = z.shape[0]

    def zspec(col, width):
        return pl.BlockSpec((tm, width), lambda i, c=col // width: (i, c))

    def ospec(width):
        return pl.BlockSpec((tm, width), lambda i: (i, 0))

    return pl.pallas_call(
        _nsa_prep_body,
        grid=(m // tm,),
        in_specs=[zspec(C_NQ, 512), zspec(C_KV, 512), zspec(C_WIN, 256),
                  pl.BlockSpec((1, 512), lambda i: (0, 0)),
                  pl.BlockSpec((1, 128), lambda i: (0, 0)),
                  pl.BlockSpec((1, 128), lambda i: (0, 0))],
        out_specs=[ospec(512), ospec(512), ospec(256), ospec(256), ospec(256)],
        out_shape=[jax.ShapeDtypeStruct((m, 512), BF16),
                   jax.ShapeDtypeStruct((m, 512), F32),
                   jax.ShapeDtypeStruct((m, 256), F32),
                   jax.ShapeDtypeStruct((m, 256), BF16),
                   jax.ShapeDtypeStruct((m, 256), BF16)],
        compiler_params=_cparams("parallel"),
        name="nsa_prep",
    )(z, z, z, qg, ksg, kwg)


def _with_pos_lanes(keys, blk, lo):
    lane = lax.broadcasted_iota(jnp.int32, keys.shape, 1) - NSA_DH
    pos = jnp.where(lane == 0, lo, jnp.where((lane > 0) & (lane == blk), 1, 0)).astype(F32)
    return jnp.where(lane < 0, keys, pos).astype(BF16)


def _group_keys(k2, g):
    return k2 if g == 0 else pltpu.roll(k2, NSA_DH, 1)


def _nsa_prep_t_body(nq_ref, kv_ref, win_ref, misc_ref, qg_ref, ksg_ref, kwg_ref,
                     qt_out, gt_out, kv_out, win_out, ksel_out, vselt_out, kwin_out, vwint_out,
                     *, tm, length):
    q = _seg_rms64(nq_ref[...], qg_ref[...]) * (NSA_DH ** -0.5)
    qt_out[...] = q.T.astype(BF16)
    gt_out[...] = jax.nn.sigmoid(misc_ref[...]).T[MISC_NG:MISC_NG + 32, :]
    kv = kv_ref[...]
    ksn = _seg_rms64(kv[:, 2 * KV_WIDTH:3 * KV_WIDTH], ksg_ref[...])
    vs = kv[:, 3 * KV_WIDTH:]
    kv_out[:, :2 * KV_WIDTH] = kv[:, :2 * KV_WIDTH]
    kv_out[:, 2 * KV_WIDTH:3 * KV_WIDTH] = ksn
    kv_out[:, 3 * KV_WIDTH:] = vs
    w = win_ref[...]
    kwn = _seg_rms64(w[:, :KV_WIDTH], kwg_ref[...])
    vw = w[:, KV_WIDTH:]
    win_out[:, :KV_WIDTH] = kwn
    win_out[:, KV_WIDTH:] = vw

    pos = (pl.program_id(0) * tm + lax.broadcasted_iota(jnp.int32, (tm, 1), 0)) % length
    lane = lax.broadcasted_iota(jnp.int32, (tm, KV_WIDTH), 1)
    for g in range(NSA_G):
        gs = slice(g * KV_WIDTH, (g + 1) * KV_WIDTH)
        ksel_out[:, gs] = _with_pos_lanes(_group_keys(ksn, g), pos // SEL_BLOCK, pos % SEL_BLOCK)
        kwin_out[:, gs] = jnp.where(lane < NSA_DH, _group_keys(kwn, g), 0.0).astype(BF16)
    ones = jnp.where(lax.broadcasted_iota(jnp.int32, (NSA_DH, tm), 0) == 0, 1.0, 0.0)

    def values_t(v):
        vt = v.T
        return jnp.concatenate([vt[:NSA_DH], ones, vt[NSA_DH:], ones], axis=0).astype(BF16)

    vselt_out[...] = values_t(vs)
    vwint_out[...] = values_t(vw)


def _nsa_prep_t_call(z, qg, ksg, kwg, tm, length):
    m = z.shape[0]

    def zspec(col, width):
        return pl.BlockSpec((tm, width), lambda i, c=col // width: (i, c))

    def rows(width):
        return pl.BlockSpec((tm, width), lambda i: (i, 0))

    def cols(height):
        return pl.BlockSpec((height, tm), lambda i: (0, i))

    return pl.pallas_call(
        functools.partial(_nsa_prep_t_body, tm=tm, length=length),
        grid=(m // tm,),
        in_specs=[zspec(C_NQ, 512), zspec(C_KV, 512), zspec(C_WIN, 256), zspec(C_MISC, 128),
                  pl.BlockSpec((1, 512), lambda i: (0, 0)),
                  pl.BlockSpec((1, 128), lambda i: (0, 0)),
                  pl.BlockSpec((1, 128), lambda i: (0, 0))],
        out_specs=[cols(NSA_WIDTH), cols(32), rows(512), rows(256), rows(256), cols(256), rows(256), cols(256)],
        out_shape=[jax.ShapeDtypeStruct((NSA_WIDTH, m), BF16),
                   jax.ShapeDtypeStruct((32, m), F32),
                   jax.ShapeDtypeStruct((m, 512), F32),
                   jax.ShapeDtypeStruct((m, 256), F32),
                   jax.ShapeDtypeStruct((m, 256), BF16),
                   jax.ShapeDtypeStruct((256, m), BF16),
                   jax.ShapeDtypeStruct((m, 256), BF16),
                   jax.ShapeDtypeStruct((256, m), BF16)],
        compiler_params=_cparams("parallel"),
        name="nsa_prep_t",
    )(z, z, z, z, qg, ksg, kwg)


def _compress_rows(load_rows, w_ref, kind, nblk):
    acc = jnp.zeros((nblk, 2 * KV_WIDTH), F32)
    for j in range(CMP_STRIDE):
        acc = acc + _dot(load_rows(j).astype(BF16), w_ref[kind, j])
    row = lax.broadcasted_iota(jnp.int32, (nblk, 1), 0)
    bot_next = pltpu.roll(acc[:, KV_WIDTH:], nblk - 1, 0)
    return jnp.where(row < nblk - 1, acc[:, :KV_WIDTH] + bot_next, 0.0)


def _compress_body(krows_ref, vrows_ref, w_ref, g_ref, kc_out, vct_out, *, nblk):
    def loader(ref):
        return lambda j: ref[pl.ds(j, nblk, stride=CMP_STRIDE), :]

    kc = _compress_rows(loader(krows_ref), w_ref, 0, nblk)
    vc = _compress_rows(loader(vrows_ref), w_ref, 1, nblk)
    kcn = _seg_rms64(kc, g_ref[...])
    ec = CMP_STRIDE * lax.broadcasted_iota(jnp.int32, (nblk, 1), 0) + (CMP_LEN - 1)
    for g in range(NSA_G):
        kc_out[:, g * KV_WIDTH:(g + 1) * KV_WIDTH] = _with_pos_lanes(
            _group_keys(kcn, g), ec // SEL_BLOCK, ec % SEL_BLOCK)
    vct_out[...] = vc.T.astype(BF16)


def _compress_call(kv_new3, w_cmp, kcg):
    nb, length, _ = kv_new3.shape
    nblk = length // CMP_STRIDE
    return pl.pallas_call(
        functools.partial(_compress_body, nblk=nblk),
        grid=(nb,),
        in_specs=[pl.BlockSpec((None, length, KV_WIDTH), lambda b: (b, 0, 0)),
                  pl.BlockSpec((None, length, KV_WIDTH), lambda b: (b, 0, 1)),
                  pl.BlockSpec((2, CMP_STRIDE, KV_WIDTH, 2 * KV_WIDTH), lambda b: (0, 0, 0, 0)),
                  pl.BlockSpec((1, KV_WIDTH), lambda b: (0, 0))],
        out_specs=[pl.BlockSpec((None, nblk, 2 * KV_WIDTH), lambda b: (b, 0, 0)),
                   pl.BlockSpec((None, KV_WIDTH, nblk), lambda b: (b, 0, 0))],
        out_shape=[jax.ShapeDtypeStruct((nb, nblk, 2 * KV_WIDTH), BF16),
                   jax.ShapeDtypeStruct((nb, KV_WIDTH, nblk), BF16)],
        compiler_params=_cparams("parallel"),
        name="nsa_compress",
    )(kv_new3, kv_new3, w_cmp, kcg)


def _round_up(a, b):
    return -(-a // b) * b


def _tile_rows(a, n):
    return jnp.concatenate([a] * n, axis=0)


def _softmax_rows(s, valid):
    s = jnp.where(valid, s, -BIG)
    m = jnp.max(s, axis=-1, keepdims=True)
    e = jnp.where(valid, jnp.exp(s - m), 0.0)
    l = jnp.sum(e, axis=-1, keepdims=True)
    return e / jnp.where(l > 0.0, l, 1.0)


def _sel_rows(ns):
    return _round_up(ns, 8)


def _imp_scratch_rows(ns):
    return 8 + SEL_PER * _sel_rows(ns) + 8


def _select_blocks_t(imp_t, imp_scr, sc_scr, t_row, ns, n_sel, n_rank=None):
    nb, tq = imp_t.shape
    nsr = _sel_rows(ns)
    rows = _imp_scratch_rows(ns)
    imp_scr[0:8, :] = jnp.zeros((8, tq), F32)
    imp_scr[8 + nb:rows, :] = jnp.zeros((rows - 8 - nb, tq), F32)
    imp_scr[8:8 + nb, :] = imp_t

    def ld(off):
        return imp_scr[pl.ds(8 + off, nsr, stride=SEL_PER), :]

    p_slc = ld(-1) + 2.0 * (ld(0) + ld(1) + ld(2)) + ld(3)
    j = lax.broadcasted_iota(jnp.int32, (nsr, 1), 0)
    cur = t_row // SEL_BLOCK
    forced = (j == 0) | (j == cur) | (j == cur - 1)
    score = jnp.where(forced, BIG, jnp.where(j <= cur, p_slc, -BIG))
    sc_scr[...] = score

    def rank_body(k, cnt):
        sk = sc_scr[pl.ds(k, 1), :]
        beats = (sk > score) | ((sk == score) & (k < j))
        return cnt + jnp.where(beats, 1.0, 0.0)

    cnt = lax.fori_loop(0, ns if n_rank is None else n_rank, rank_body, jnp.zeros((nsr, tq), F32))
    return jnp.where((cnt < float(n_sel)) & (j < ns), 1.0, 0.0)


def _select_blocks(imp, imp_scr, sc_scr, t_row, ns, n_sel):
    sel_t = _select_blocks_t(imp.T, imp_scr, sc_scr, t_row, ns, n_sel)
    nsp = _round_up(ns, 128)
    if nsp > sel_t.shape[0]:
        sel_t = jnp.concatenate([sel_t, jnp.zeros((nsp - sel_t.shape[0], sel_t.shape[1]), F32)], axis=0)
    return sel_t.T.astype(BF16)


def _online_update(carry, s, valid, pv):
    m, l, acc = carry
    s = jnp.where(valid, s, -BIG)
    m_new = jnp.maximum(m, jnp.max(s, axis=-1, keepdims=True))
    alpha = jnp.exp(m - m_new)
    e = jnp.where(valid, jnp.exp(s - m_new), 0.0)
    l = alpha * l + jnp.sum(e, axis=-1, keepdims=True)
    acc = alpha * acc + pv(e.astype(BF16))
    return m_new, l, acc


def _online_init(rows, width):
    return (jnp.full((rows, 1), -BIG, F32), jnp.zeros((rows, 1), F32), jnp.zeros((rows, width), F32))


def _online_finish(carry):
    _, l, acc = carry
    return acc / jnp.where(l > 0.0, l, 1.0)


NEG_MASK = -(2.0 ** 100)
CMP_TQ = 128
ATT_TQ = 256
ATT_TK = 512


def _slope_lanes(g, tq):
    r = lax.broadcasted_iota(jnp.int32, (1, NSA_REP * tq), 1) // tq
    out = jnp.full((1, NSA_REP * tq), SLOPES[g * NSA_REP], F32)
    for k in range(1, NSA_REP):
        out = jnp.where(r == k, SLOPES[g * NSA_REP + k], out)
    return out


def _q_aug_t(qt_ref, g, tq, qsel_t):
    lanes = NSA_REP * tq
    qrows = jnp.concatenate(
        [qt_ref[(g * NSA_REP + r) * NSA_DH:(g * NSA_REP + r + 1) * NSA_DH, :] for r in range(NSA_REP)], axis=1)
    j = lax.broadcasted_iota(jnp.int32, (NSA_DH, lanes), 0)
    slope = _slope_lanes(g, tq)
    aug = jnp.where(j == 0, slope, slope * (float(SEL_BLOCK) * j.astype(F32)))
    if qsel_t is not None:
        aug = jnp.where(j == 0, aug, aug + _tile_lanes(qsel_t.astype(F32), NSA_REP))
    return jnp.concatenate([qrows, aug.astype(BF16)], axis=0)


def _tile_lanes(a, n):
    return jnp.concatenate([a] * n, axis=1)


def _cmp_select_body(qt_ref, kc_ref, vct_ref, oct_ref, qselt_ref, cnt_ref, imp_scr, sc_scr, *, length, tq):
    nb = length // CMP_STRIDE
    ns = length // SEL_BLOCK
    nsr = _sel_rows(ns)
    n_sel = min(SEL_TOP_N, ns)
    t0 = pl.program_id(1) * tq
    t_row = t0 + lax.broadcasted_iota(jnp.int32, (1, tq), 1)
    ec = CMP_STRIDE * lax.broadcasted_iota(jnp.int32, (nb, 1), 0) + (CMP_LEN - 1)
    hidden = _tile_lanes(jnp.where(t_row >= ec, 0.0, NEG_MASK), NSA_REP)
    counts = []
    for g in range(NSA_G):
        s_t = _dot(kc_ref[:, g * KV_WIDTH:(g + 1) * KV_WIDTH], _q_aug_t(qt_ref, g, tq, None)) + hidden
        m = jnp.max(s_t, axis=0, keepdims=True)
        e = jnp.where(hidden < 0.0, 0.0, jnp.exp(s_t - m))
        l = jnp.sum(e, axis=0, keepdims=True)
        p_t = e * jnp.where(l > 0.0, 1.0 / l, 0.0)
        o_ct = _dot(vct_ref[g * NSA_DH:(g + 1) * NSA_DH, :], p_t.astype(BF16))
        imp_t = p_t[:, 0:tq]
        for r in range(1, NSA_REP):
            imp_t = imp_t + p_t[:, r * tq:(r + 1) * tq]
        n_rank = jnp.minimum((t0 + tq - 1) // SEL_BLOCK + 1, ns)
        sel_t = _select_blocks_t(imp_t, imp_scr, sc_scr, t_row, ns, n_sel, n_rank)
        qsel = jnp.where(sel_t > 0.5, 0.0, NEG_MASK)
        if nsr < SEL_BLOCK:
            qsel = jnp.concatenate([qsel, jnp.full((SEL_BLOCK - nsr, tq), NEG_MASK, F32)], axis=0)
        qselt_ref[g * SEL_BLOCK:(g + 1) * SEL_BLOCK, :] = qsel.astype(BF16)
        cnt = _dot_nt(jnp.ones((8, tq), BF16), sel_t.astype(BF16))
        if nsr < SEL_BLOCK:
            cnt = jnp.concatenate([cnt, jnp.zeros((8, SEL_BLOCK - nsr), F32)], axis=1)
        counts.append(cnt)
        for r in range(NSA_REP):
            h = g * NSA_REP + r
            oct_ref[h * NSA_DH:(h + 1) * NSA_DH, :] = o_ct[:, r * tq:(r + 1) * tq]
    cnt_ref[...] = jnp.concatenate(counts, axis=1)


def _cmp_select_call(q_t, kc_aug, vc_t, n_batch, length):
    tq = CMP_TQ
    nq = length // tq
    nb = length // CMP_STRIDE
    ns = length // SEL_BLOCK
    assert ns <= SEL_BLOCK
    m = n_batch * length
    return pl.pallas_call(
        functools.partial(_cmp_select_body, length=length, tq=tq),
        grid=(n_batch, nq),
        in_specs=[pl.BlockSpec((NSA_WIDTH, tq), lambda b, i: (0, b * nq + i)),
                  pl.BlockSpec((None, nb, 2 * KV_WIDTH), lambda b, i: (b, 0, 0)),
                  pl.BlockSpec((None, KV_WIDTH, nb), lambda b, i: (b, 0, 0))],
        out_specs=[pl.BlockSpec((NSA_WIDTH, tq), lambda b, i: (0, b * nq + i)),
                   pl.BlockSpec((NSA_G * SEL_BLOCK, tq), lambda b, i: (0, b * nq + i)),
                   pl.BlockSpec((8, NSA_G * SEL_BLOCK), lambda b, i: (b * nq + i, 0))],
        out_shape=[jax.ShapeDtypeStruct((NSA_WIDTH, m), F32),
                   jax.ShapeDtypeStruct((NSA_G * SEL_BLOCK, m), BF16),
                   jax.ShapeDtypeStruct((n_batch * nq * 8, NSA_G * SEL_BLOCK), F32)],
        scratch_shapes=[pltpu.VMEM((_imp_scratch_rows(ns), tq), F32), pltpu.VMEM((_sel_rows(ns), tq), F32)],
        compiler_params=_cparams("parallel", "parallel"),
        name="nsa_cmp_select",
    )(q_t, kc_aug, vc_t)


def _nsa_attn_body(flags_ref, qt_ref, qselt_ref, gt_ref, oct_ref, ksel_ref, vselt_ref, kwin_ref, vwint_ref,
                   nn_ref, o_ref, m_scr, acc_scr, on_scr, *, length, tq, tk):
    b, i = pl.program_id(0), pl.program_id(1)
    nq = pl.num_programs(1)
    n_kt_all = length // tk
    t0 = i * tq
    t_row = t0 + lax.broadcasted_iota(jnp.int32, (1, tq), 1)
    span = WINDOW + tq
    w0 = pl.multiple_of(jnp.maximum(t0 - WINDOW, 0), tq)
    w_rel = lax.broadcasted_iota(jnp.int32, (span, 1), 0)
    in_window = (w0 + w_rel <= t_row) & (w0 + w_rel >= t_row - WINDOW)
    win_mask = _tile_lanes(jnp.where(in_window, 0.0, NEG_MASK), NSA_REP)
    for g in range(NSA_G):
        qa = _q_aug_t(qt_ref, g, tq, qselt_ref[g * SEL_BLOCK:(g + 1) * SEL_BLOCK, :])
        gs = slice(g * KV_WIDTH, (g + 1) * KV_WIDTH)
        vrows = slice(g * 2 * NSA_DH, (g + 1) * 2 * NSA_DH)

        def reset():
            m_scr[...] = jnp.full(m_scr.shape, -BIG, F32)
            acc_scr[...] = jnp.zeros(acc_scr.shape, F32)

        def step(k_ref, vt_ref, k0, rows, causal):
            s_t = _dot(k_ref[pl.ds(k0, rows), gs], qa)
            if causal:
                kpos = k0 + lax.broadcasted_iota(jnp.int32, (rows, 1), 0)
                s_t = s_t + _tile_lanes(jnp.where(kpos <= t_row, 0.0, NEG_MASK), NSA_REP)
            m_old = m_scr[...]
            m_new = jnp.maximum(m_old, jnp.max(s_t, axis=0, keepdims=True))
            e = jnp.exp(s_t - m_new).astype(BF16)
            acc_scr[...] = jnp.exp(m_old - m_new) * acc_scr[...] + _dot(vt_ref[vrows, pl.ds(k0, rows)], e)
            m_scr[...] = m_new

        def finish():
            acc = acc_scr[...]
            l = acc[NSA_DH:NSA_DH + 1, :]
            return acc[0:NSA_DH, :] / jnp.where(l > 0.0, l, 1.0)

        reset()
        last = (t0 + tq + tk - 1) // tk - 1
        fbase = ((b * nq + i) * NSA_G + g) * n_kt_all

        def sel_tile(kt, carry):
            @pl.when(flags_ref[fbase + kt] != 0)
            def _():
                step(ksel_ref, vselt_ref, pl.multiple_of(kt * tk, tk), tk, False)
            return carry

        lax.fori_loop(0, last, sel_tile, 0)
        step(ksel_ref, vselt_ref, pl.multiple_of(last * tk, tk), tk, True)
        o_s = finish()

        k_w = _with_pos_lanes(kwin_ref[pl.ds(w0, span), gs].astype(F32), w_rel // SEL_BLOCK, w_rel % SEL_BLOCK)
        s_w = _dot(k_w, _q_aug_t(qt_ref, g, tq, None)) + win_mask
        e = jnp.exp(s_w - jnp.max(s_w, axis=0, keepdims=True)).astype(BF16)
        acc = _dot(vwint_ref[vrows, pl.ds(w0, span)], e)
        o_w = acc[0:NSA_DH, :] / acc[NSA_DH:NSA_DH + 1, :]

        for r in range(NSA_REP):
            h = g * NSA_REP + r
            hs = slice(h * NSA_DH, (h + 1) * NSA_DH)
            ls = slice(r * tq, (r + 1) * tq)
            on_scr[hs, :] = (gt_ref[h:h + 1, :] * oct_ref[hs, :]
                             + gt_ref[NSA_HEADS + h:NSA_HEADS + h + 1, :] * o_s[:, ls]
                             + gt_ref[2 * NSA_HEADS + h:2 * NSA_HEADS + h + 1, :] * o_w[:, ls])
    on = on_scr[...]
    ms = jnp.sum(on * on, axis=0, keepdims=True) * (1.0 / NSA_WIDTH)
    o_ref[...] = (on * lax.rsqrt(ms + RMS_EPS) * nn_ref[...]).T


def _nsa_attn_call(flags, q_t, qsel_t, g_t, oc_t, ksel, vsel_t, kwin, vwin_t, nn_col, n_batch, length):
    tq, tk = ATT_TQ, ATT_TK
    nq = length // tq
    assert WINDOW % tq == 0 and length % tk == 0 and tk % tq == 0 and length >= WINDOW + tq
    body = functools.partial(_nsa_attn_body, length=length, tq=tq, tk=tk)

    def cols(rows):
        return pl.BlockSpec((rows, tq), lambda b, i, fl: (0, b * nq + i))

    def keys(width=2 * KV_WIDTH):
        return pl.BlockSpec((length, width), lambda b, i, fl: (b, 0))

    def values():
        return pl.BlockSpec((2 * KV_WIDTH, length), lambda b, i, fl: (0, b))

    grid_spec = pltpu.PrefetchScalarGridSpec(
        num_scalar_prefetch=1,
        grid=(n_batch, nq),
        in_specs=[cols(NSA_WIDTH), cols(NSA_G * SEL_BLOCK), cols(32), cols(NSA_WIDTH),
                  keys(), values(), keys(), values(),
                  pl.BlockSpec((NSA_WIDTH, 1), lambda b, i, fl: (0, 0))],
        out_specs=pl.BlockSpec((tq, NSA_WIDTH), lambda b, i, fl: (b * nq + i, 0)),
        scratch_shapes=[pltpu.VMEM((1, NSA_REP * tq), F32),
                        pltpu.VMEM((2 * NSA_DH, NSA_REP * tq), F32),
                        pltpu.VMEM((NSA_WIDTH, tq), F32)])
    return pl.pallas_call(
        body,
        grid_spec=grid_spec,
        out_shape=jax.ShapeDtypeStruct((n_batch * length, NSA_WIDTH), F32),
        compiler_params=_cparams("parallel", "parallel"),
        name="nsa_attn",
    )(flags, q_t, qsel_t, g_t, oc_t, ksel, vsel_t, kwin, vwin_t, nn_col)


def _prep_weights(p):
    depth = p['w_in'].shape[0]
    offs = np.concatenate([[0], np.cumsum(IN_SIZES)])
    piece = lambda i: p['w_in'][:, :, int(offs[i]):int(offs[i + 1])]
    order = [0, 1, 2, 4, 5, 6, 7, 8, 9, 10, 11, 3, 12]
    cols = [piece(i) for i in order]
    used = sum(IN_SIZES)
    cols.append(jnp.zeros((depth, D_MODEL, N_Z - used), F32))
    w_in = jnp.concatenate(cols, axis=-1).astype(BF16)
    wa = jnp.zeros((depth, 128, GLA_HEADS * GLA_DK), F32).at[:, :GLA_GATE_RANK, :].set(p['w_gla_a']).astype(BF16)

    def cmp_weights(w):
        eye = jnp.eye(NSA_G, dtype=F32)
        bd = jnp.einsum('gh,ljde->ljgdhe', eye, w).reshape(depth, CMP_LEN, KV_WIDTH, KV_WIDTH)
        return jnp.concatenate([bd[:, :CMP_STRIDE], bd[:, CMP_STRIDE:]], axis=-1)

    w_cmp = jnp.stack([cmp_weights(p['w_ck']), cmp_weights(p['w_cv'])], axis=1).astype(BF16)
    tile = lambda a, n: jnp.tile(a, (1, n))[:, None, :]
    return {
        'w_in': w_in, 'wa': wa, 'ba': p['b_gla_a'][:, None, :], 'w_cmp': w_cmp,
        'norm1': p['norm1'][:, None, :], 'norm2': p['norm2'][:, None, :],
        'gla_norm': p['gla_norm'][:, None, :], 'nsa_norm': p['nsa_norm'][:, None, :],
        'q_norm': tile(p['q_norm'], NSA_HEADS), 'kc_norm': tile(p['kc_norm'], NSA_G),
        'ks_norm': tile(p['ks_norm'], NSA_G), 'kw_norm': tile(p['kw_norm'], NSA_G),
        'w_out': p['w_out'].astype(BF16), 'w_up': p['w_up'].astype(BF16),
        'w_gate': p['w_gate'].astype(BF16), 'w_down': p['w_down'].astype(BF16),
        'conv_w': p['conv_w'], 'conv_b': p['conv_b'][:, None, :],
    }


def _split_mod(mod, shape):
    return [mod[:, k * D_MODEL:(k + 1) * D_MODEL].reshape(shape) for k in range(6)]


def _layer_prompt(x, mod, pw, l):
    nb, length, _ = x.shape
    m = nb * length
    sh1, sc1, g1, sh2, sc2, g2 = _split_mod(mod, (nb, 1, D_MODEL))
    x2 = x.reshape(m, D_MODEL)
    tm = min(512, length)
    z = _inproj_call(x2, pw['norm1'][l], sc1, sh1, pw['w_in'][l], min(1024, length), length)
    s0 = jnp.zeros((nb, GLA_HEADS, GLA_DK, GLA_DV), F32)
    o_g, s_t = _gla_call(z, pw['wa'][l], pw['ba'][l], pw['gla_norm'][l], s0, nb, length, GLA_ROWS, length)
    q_t, g_t, kv_new, win_new, ksel, vsel_t, kwin, vwin_t = _nsa_prep_t_call(
        z, pw['q_norm'][l], pw['ks_norm'][l], pw['kw_norm'][l], tm, length)
    kc_aug, vc_t = _compress_call(kv_new.reshape(nb, length, 4 * KV_WIDTH), pw['w_cmp'][l], pw['kc_norm'][l])
    oc_t, qsel_t, cnt = _cmp_select_call(q_t, kc_aug, vc_t, nb, length)
    per_tile = ATT_TK // SEL_BLOCK
    flags = cnt.reshape(-1, ATT_TQ // CMP_TQ, 8, NSA_G, SEL_BLOCK // per_tile, per_tile)
    flags = jnp.sum(flags[:, :, 0, :, :length // ATT_TK], axis=(1, -1))
    flags = (flags > 0.0).astype(jnp.int32).reshape(-1)
    o_n = _nsa_attn_call(flags, q_t, qsel_t, g_t, oc_t, ksel, vsel_t, kwin, vwin_t,
                         pw['nsa_norm'][l].reshape(NSA_WIDTH, 1), nb, length)
    x_mid, h2 = _outproj_call(o_g, o_n, x2, g1, pw['w_out'][l], pw['norm2'][l], sc2, sh2, tm, length)
    tm_f = min(1024, length)
    x_out, u_keep = _ffn_call(h2, x_mid, g2, pw['w_up'][l], pw['w_gate'][l], pw['w_down'][l],
                              pw['conv_w'][l], pw['conv_b'][l], tm_f, length)
    conv = u_keep.reshape(nb, length // tm_f, 8, D_FF)[:, -1, 8 - (CONV_W - 1):, :]
    wl = min(WINDOW, length)
    return (x_out.reshape(nb, length, D_MODEL),
            jnp.swapaxes(s_t, -1, -2),
            kv_new.reshape(nb, length, 4, NSA_G, NSA_DH),
            win_new.reshape(nb, length, 2 * KV_WIDTH)[:, length - wl:].reshape(nb, wl, 2, NSA_G, NSA_DH),
            conv)


SAMPLE_ROWS = 16


def _page_dma(pt_ref, ckv_hbm, buf, sem, layer, batch, slot, row0, n_pages, start):
    rows = buf.shape[1]

    def body(p, carry):
        cp = pltpu.make_async_copy(
            ckv_hbm.at[layer, pt_ref[batch, p], pl.ds(row0, rows), :],
            buf.at[slot, :, pl.ds(pl.multiple_of(p * PAGE_SIZE, PAGE_SIZE), PAGE_SIZE)],
            sem.at[slot])
        if start:
            cp.start()
        else:
            cp.wait()
        return carry

    lax.fori_loop(0, n_pages, body, 0)


def _gather_pages(pt_ref, ckv_hbm, buf, sem, layer, row0, n_pages):
    b = pl.program_id(0)
    dma = functools.partial(_page_dma, pt_ref, ckv_hbm, buf, sem, layer)

    @pl.when(b == 0)
    def _():
        dma(0, 0, row0, n_pages, True)

    @pl.when(b + 1 < pl.num_programs(0))
    def _():
        dma(b + 1, (b + 1) % 2, row0, n_pages, True)

    slot = b % 2
    dma(b, slot, row0, n_pages, False)
    return slot


def _q_stack(q_ref, g):
    return jnp.concatenate(
        [q_ref[:, (g * NSA_REP + r) * NSA_DH:(g * NSA_REP + r + 1) * NSA_DH] for r in range(NSA_REP)], axis=0)


def _slope_rows(g, rows):
    return jnp.concatenate([jnp.full((rows, 1), SLOPES[g * NSA_REP + r], F32) for r in range(NSA_REP)], axis=0)


def _s1_body(pt_ref, ckv_hbm, q_ref, w_ref, g_ref, oc_ref, sel_ref, buf, sem, rows_scr, imp_scr, sc_scr,
             *, layer, n_pages, past_len):
    slot = _gather_pages(pt_ref, ckv_hbm, buf, sem, layer, 0, n_pages)
    ts = SAMPLE_ROWS
    nblk = past_len // CMP_STRIDE
    ns = past_len // SEL_BLOCK + 1
    nsp = _round_up(ns, 128)
    n_sel = min(SEL_TOP_N, ns)
    chunk = min(512, past_len)
    for kind in range(2):
        for c in range(past_len // chunk):
            rows_scr[kind, c * chunk:(c + 1) * chunk, :] = (
                buf[slot, kind * KV_WIDTH:(kind + 1) * KV_WIDTH, c * chunk:(c + 1) * chunk].T)

    def loader(kind):
        return lambda j: rows_scr[kind, pl.ds(j, nblk, stride=CMP_STRIDE), :]

    kc = _seg_rms64(_compress_rows(loader(0), w_ref, 0, nblk), g_ref[...]).astype(BF16)
    vc = _compress_rows(loader(1), w_ref, 1, nblk).astype(BF16)

    t_col = past_len + lax.broadcasted_iota(jnp.int32, (ts, 1), 0)
    ec = CMP_STRIDE * lax.broadcasted_iota(jnp.int32, (1, nblk), 1) + (CMP_LEN - 1)
    dist = t_col - ec
    valid4 = _tile_rows(dist >= 0, NSA_REP)
    distf4 = _tile_rows(dist.astype(F32), NSA_REP)
    heads, imps = [], []
    for g in range(NSA_G):
        ks = slice(g * NSA_DH, (g + 1) * NSA_DH)
        s = _dot_nt(_q_stack(q_ref, g), kc[:, ks]) - _slope_rows(g, ts) * distf4
        p = _softmax_rows(s, valid4)
        o_c = _dot(p.astype(BF16), vc[:, ks])
        imp = p[0:ts]
        for r in range(1, NSA_REP):
            imp = imp + p[r * ts:(r + 1) * ts]
        imps.append(imp)
        heads += [o_c[r * ts:(r + 1) * ts] for r in range(NSA_REP)]
    oc_ref[...] = jnp.concatenate(heads, axis=-1)
    imp = jnp.concatenate(imps + [jnp.zeros((128 - NSA_G * ts, nblk), F32)], axis=0)
    t_row = past_len + lax.broadcasted_iota(jnp.int32, (1, 128), 1) % ts
    sel = _select_blocks(imp, imp_scr, sc_scr, t_row, ns, n_sel)
    for g in range(NSA_G):
        sel_ref[:, g * nsp:(g + 1) * nsp] = sel[g * ts:(g + 1) * ts, :].astype(F32)


def _s1_call(page_table, ckv, q_bf, w_cmp, kcg, layer, past_len):
    nbatch, n_pages = page_table.shape
    ts = SAMPLE_ROWS
    ns = past_len // SEL_BLOCK + 1
    nsp = _round_up(ns, 128)
    body = functools.partial(_s1_body, layer=layer, n_pages=n_pages, past_len=past_len)
    grid_spec = pltpu.PrefetchScalarGridSpec(
        num_scalar_prefetch=1,
        grid=(nbatch,),
        in_specs=[pl.BlockSpec(memory_space=pl.ANY),
                  pl.BlockSpec((ts, NSA_WIDTH), lambda b, pt: (b, 0)),
                  pl.BlockSpec((2, CMP_STRIDE, KV_WIDTH, 2 * KV_WIDTH), lambda b, pt: (0, 0, 0, 0)),
                  pl.BlockSpec((1, KV_WIDTH), lambda b, pt: (0, 0))],
        out_specs=[pl.BlockSpec((ts, NSA_WIDTH), lambda b, pt: (b, 0)),
                   pl.BlockSpec((ts, NSA_G * nsp), lambda b, pt: (b, 0))],
        scratch_shapes=[pltpu.VMEM((2, 2 * KV_WIDTH, past_len), F32),
                        pltpu.SemaphoreType.DMA((2,)),
                        pltpu.VMEM((2, past_len, KV_WIDTH), F32),
                        pltpu.VMEM((_imp_scratch_rows(ns), 128), F32),
                        pltpu.VMEM((_sel_rows(ns), 128), F32)])
    return pl.pallas_call(
        body,
        grid_spec=grid_spec,
        out_shape=[jax.ShapeDtypeStruct((nbatch * ts, NSA_WIDTH), F32),
                   jax.ShapeDtypeStruct((nbatch * ts, NSA_G * nsp), F32)],
        compiler_params=_cparams("arbitrary"),
        name="nsa_sample_select",
    )(page_table, ckv, q_bf, w_cmp, kcg)


def _s2_body(pt_ref, ckv_hbm, q_ref, misc_ref, oc_ref, sel_ref, kvn_ref, cwin_ref, wn_ref, e_ref, nn_ref,
             o_ref, buf, sem, *, layer, n_pages, past_len, wb, tk):
    slot = _gather_pages(pt_ref, ckv_hbm, buf, sem, layer, 2 * KV_WIDTH, n_pages)
    ts = SAMPLE_ROWS
    rows = NSA_HEADS * ts
    ns = past_len // SEL_BLOCK + 1
    nsp = sel_ref.shape[1] // NSA_G
    gates = jax.nn.sigmoid(misc_ref[:, MISC_NG:MISC_NG + 3 * NSA_HEADS])
    half = lax.broadcasted_iota(jnp.int32, (ts, KV_WIDTH), 1) // NSA_DH
    q_rows, slopes, sel_rows = [], [], []
    for h in range(NSA_HEADS):
        g = h // NSA_REP
        pair = q_ref[:, (h // 2) * KV_WIDTH:(h // 2 + 1) * KV_WIDTH].astype(F32)
        if h % 2 != g:
            pair = pltpu.roll(pair, NSA_DH, 1)
        q_rows.append(jnp.where(half == g, pair, 0.0))
        slopes.append(jnp.full((ts, 1), SLOPES[h], F32))
        sel_rows.append(sel_ref[:, g * nsp:(g + 1) * nsp])
    q_all = jnp.concatenate(q_rows, axis=0).astype(BF16)
    slope_rows = jnp.concatenate(slopes, axis=0)
    sel_all = jnp.concatenate(sel_rows, axis=0)
    sel_bf = sel_all.astype(BF16)
    t_col = past_len + lax.broadcasted_iota(jnp.int32, (rows, 1), 0) % ts
    zpad = jnp.zeros((128 - ts, KV_WIDTH), BF16)

    def rel(k0, n):
        return k0 - past_len + lax.broadcasted_iota(jnp.int32, (1, n), 1)

    def scores(keys_t, r):
        return _dot(q_all, keys_t) + slope_rows * r.astype(F32)

    def scores_new(ref):
        keys = jnp.concatenate([ref[:, :KV_WIDTH], zpad], axis=0)
        return _dot_nt(q_all, keys) + slope_rows * rel(past_len, 128).astype(F32)

    def pv_new(ref):
        vals = jnp.concatenate([ref[:, KV_WIDTH:], zpad], axis=0)
        return lambda e: _dot(e, vals)

    d_new = t_col - (past_len + lax.broadcasted_iota(jnp.int32, (1, 128), 1))

    def kt_body(kt, carry):
        k0 = pl.multiple_of(kt * tk, tk)
        valid = _dot(sel_bf, e_ref[:, pl.ds(k0, tk)]) > 0.5
        v_t = buf[slot, KV_WIDTH:, pl.ds(k0, tk)].astype(BF16)
        s = scores(buf[slot, :KV_WIDTH, pl.ds(k0, tk)].astype(BF16), rel(k0, tk))
        return _online_update(carry, s, valid, lambda e: _dot_nt(e, v_t))

    carry = lax.fori_loop(0, past_len // tk, kt_body, _online_init(rows, KV_WIDTH))
    valid = (sel_all[:, ns - 1:ns] > 0.5) & (d_new >= 0)
    o_s = _online_finish(_online_update(carry, scores_new(kvn_ref), valid, pv_new(kvn_ref)))

    d_win = t_col - (past_len - wb + lax.broadcasted_iota(jnp.int32, (1, wb), 1))
    vw_t = cwin_ref[KV_WIDTH:, :].astype(BF16)
    carry = _online_update(_online_init(rows, KV_WIDTH),
                           scores(cwin_ref[:KV_WIDTH, :].astype(BF16), rel(past_len - wb, wb)),
                           (d_win >= 0) & (d_win <= WINDOW), lambda e: _dot_nt(e, vw_t))
    valid = (d_new >= 0) & (d_new <= WINDOW)
    o_w = _online_finish(_online_update(carry, scores_new(wn_ref), valid, pv_new(wn_ref)))

    heads = []
    for h in range(NSA_HEADS):
        g = h // NSA_REP
        rs = slice(h * ts, (h + 1) * ts)
        ls = slice(g * NSA_DH, (g + 1) * NSA_DH)
        heads.append(gates[:, h:h + 1] * oc_ref[:, h * NSA_DH:(h + 1) * NSA_DH]
                     + gates[:, NSA_HEADS + h:NSA_HEADS + h + 1] * o_s[rs, ls]
                     + gates[:, 2 * NSA_HEADS + h:2 * NSA_HEADS + h + 1] * o_w[rs, ls])
    o_ref[...] = _rms_rows(jnp.concatenate(heads, axis=-1), nn_ref[...])


def _s2_call(page_table, ckv_t, q_bf, z, o_c, sel, kvs_bf, cwin_t, win_bf, nn, layer, past_len):
    nbatch, n_pages = page_table.shape
    ts = SAMPLE_ROWS
    wb = cwin_t.shape[3]
    tk = min(2048, past_len)
    nsp = sel.shape[1] // NSA_G
    body = functools.partial(_s2_body, layer=layer, n_pages=n_pages, past_len=past_len, wb=wb, tk=tk)
    blk = lax.broadcasted_iota(jnp.int32, (nsp, past_len), 1) // SEL_BLOCK
    expand = (lax.broadcasted_iota(jnp.int32, (nsp, past_len), 0) == blk).astype(BF16)

    def rows(width, col_block=0):
        return pl.BlockSpec((ts, width), lambda b, pt, c=col_block: (b, c))

    grid_spec = pltpu.PrefetchScalarGridSpec(
        num_scalar_prefetch=1,
        grid=(nbatch,),
        in_specs=[pl.BlockSpec(memory_space=pl.ANY),
                  rows(NSA_WIDTH), rows(128, C_MISC // 128), rows(NSA_WIDTH), rows(sel.shape[1]),
                  rows(2 * KV_WIDTH),
                  pl.BlockSpec((None, None, 2 * KV_WIDTH, wb), lambda b, pt: (layer, b, 0, 0)),
                  rows(2 * KV_WIDTH),
                  pl.BlockSpec((nsp, past_len), lambda b, pt: (0, 0)),
                  pl.BlockSpec((1, NSA_WIDTH), lambda b, pt: (0, 0))],
        out_specs=rows(NSA_WIDTH),
        scratch_shapes=[pltpu.VMEM((2, 2 * KV_WIDTH, past_len), F32),
                        pltpu.SemaphoreType.DMA((2,))])
    return pl.pallas_call(
        body,
        grid_spec=grid_spec,
        out_shape=jax.ShapeDtypeStruct((nbatch * ts, NSA_WIDTH), F32),
        compiler_params=_cparams("arbitrary"),
        name="nsa_sample_attn",
    )(page_table, ckv_t, q_bf, z, o_c, sel, kvs_bf, cwin_t, win_bf, expand, nn)


def _layer_sample(x, mod, pw, l, t_real, ckv, cwin, page_table, s0, conv_prev):
    nb, ts, _ = x.shape
    m = nb * ts
    past_len = page_table.shape[1] * PAGE_SIZE
    sh1, sc1, g1, sh2, sc2, g2 = [jnp.repeat(a, ts, axis=0) for a in _split_mod(mod, (nb, D_MODEL))]
    x2 = x.reshape(m, D_MODEL)
    z = _inproj_call(x2, pw['norm1'][l], sc1, sh1, pw['w_in'][l], m, ts)
    o_g, s_t = _gla_call(z, pw['wa'][l], pw['ba'][l], pw['gla_norm'][l], s0, nb, ts, ts, t_real)
    q_bf, kv_new, win_new, kvs_bf, win_bf = _nsa_prep_call(
        z, pw['q_norm'][l], pw['ks_norm'][l], pw['kw_norm'][l], m)
    o_c, sel = _s1_call(page_table, ckv, q_bf, pw['w_cmp'][l], pw['kc_norm'][l], l, past_len)
    o_n = _s2_call(page_table, ckv, q_bf, z, o_c, sel, kvs_bf, cwin, win_bf, pw['nsa_norm'][l], l, past_len)
    x_mid, h2 = _outproj_call(o_g, o_n, x2, g1, pw['w_out'][l], pw['norm2'][l], sc2, sh2, m, ts)
    zrow = jnp.zeros((nb, ts, D_FF), F32)
    p1 = zrow.at[:, 0].set(conv_prev[:, 1]).reshape(m, D_FF)
    p2 = zrow.at[:, 0].set(conv_prev[:, 0]).at[:, 1].set(conv_prev[:, 1]).reshape(m, D_FF)
    x_out, u = _ffn_call(h2, x_mid, g2, pw['w_up'][l], pw['w_gate'][l], pw['w_down'][l],
                         pw['conv_w'][l], pw['conv_b'][l], m, ts, prev=(p1, p2))
    conv = u.reshape(nb, ts, D_FF)[:, t_real - (CONV_W - 1):t_real]
    return (x_out.reshape(nb, ts, D_MODEL),
            jnp.swapaxes(s_t, -1, -2),
            kv_new.reshape(nb, ts, 4, NSA_G, NSA_DH)[:, :t_real],
            win_new.reshape(nb, ts, 2, NSA_G, NSA_DH)[:, :t_real],
            conv)


def kernel(x_prompt, x_sample, cache_kv, cache_win, state_gla, state_conv, page_table, c_prompt, c_sample,
           norm1, norm2, w_ada, b_ada, w_in, w_gla_a, b_gla_a, gla_norm, q_norm, kc_norm, ks_norm, kw_norm,
           w_ck, w_cv, nsa_norm, w_out, w_up, w_gate, conv_w, conv_b, w_down):
    depth = w_in.shape[0]
    bp, bs, t_s = x_prompt.shape[0], x_sample.shape[0], x_sample.shape[1]
    assert CONV_W - 1 <= t_s <= SAMPLE_ROWS
    pw = _prep_weights({'w_in': w_in, 'w_gla_a': w_gla_a, 'b_gla_a': b_gla_a, 'w_ck': w_ck, 'w_cv': w_cv,
                        'norm1': norm1, 'norm2': norm2, 'gla_norm': gla_norm, 'nsa_norm': nsa_norm,
                        'q_norm': q_norm, 'kc_norm': kc_norm, 'ks_norm': ks_norm, 'kw_norm': kw_norm,
                        'w_out': w_out, 'w_up': w_up, 'w_gate': w_gate, 'w_down': w_down,
                        'conv_w': conv_w, 'conv_b': conv_b})
    bc = _round_up(bp + bs, 8)
    c_all = jnp.concatenate([c_prompt, c_sample, jnp.zeros((bc - bp - bs, D_MODEL), F32)], axis=0)
    mod_all = _mod_call(c_all, w_ada, b_ada)
    n_pool = cache_kv.shape[1]
    wb = cache_win.shape[2]
    ckv = jnp.transpose(cache_kv, (0, 1, 3, 4, 5, 2)).reshape(depth, n_pool, 4 * KV_WIDTH, PAGE_SIZE)
    cwin = jnp.transpose(cache_win, (0, 1, 3, 4, 5, 2)).reshape(depth, bs, 2 * KV_WIDTH, wb)
    xp = x_prompt
    xs = jnp.pad(x_sample, ((0, 0), (0, SAMPLE_ROWS - t_s), (0, 0)))
    outs_p, outs_s = [], []
    for l in range(depth):
        res = _layer_prompt(xp, mod_all[l, :bp], pw, l)
        xp = res[0]
        outs_p.append(res[1:])
        res = _layer_sample(xs, mod_all[l, bp:bp + bs], pw, l, t_s, ckv, cwin, page_table,
                            state_gla[l], state_conv[l])
        xs = res[0]
        outs_s.append(res[1:])
    stack = lambda outs, k: jnp.stack([o[k] for o in outs])
    win_s = jnp.concatenate([cache_win[:, :, t_s:], stack(outs_s, 2)], axis=2)
    return (xp, xs[:, :t_s], stack(outs_p, 1), stack(outs_s, 1), stack(outs_p, 2), win_s,
            stack(outs_p, 0), stack(outs_s, 0), stack(outs_p, 3), stack(outs_s, 3))
```

```python
import functools

import numpy as np
import jax
import jax.numpy as jnp
from jax import lax
from jax.experimental import pallas as pl
from jax.experimental.pallas import tpu as pltpu

F32 = jnp.float32
BF16 = jnp.bfloat16

D_MODEL = 1024
GLA_HEADS = 4
GLA_DK = 64
GLA_DV = 128
GLA_GATE_RANK = 16
GLA_TAU = 16.0
GLA_CHUNK = 16
GLA_WIDTH = GLA_HEADS * GLA_DV
NSA_HEADS = 8
NSA_G = 2
NSA_REP = NSA_HEADS // NSA_G
NSA_DH = 64
NSA_WIDTH = NSA_HEADS * NSA_DH
KV_WIDTH = NSA_G * NSA_DH
CMP_STRIDE = 16
CMP_LEN = 2 * CMP_STRIDE
SEL_BLOCK = 64
SEL_PER = SEL_BLOCK // CMP_STRIDE
SEL_TOP_N = 16
WINDOW = 512
D_FF = 2816
CONV_W = 3
RMS_EPS = 1e-6
BIG = 1e30
PAGE_SIZE = 128

IN_SIZES = (GLA_HEADS * GLA_DK, GLA_HEADS * GLA_DK, GLA_WIDTH, GLA_GATE_RANK, GLA_WIDTH,
            NSA_WIDTH, KV_WIDTH, KV_WIDTH, KV_WIDTH, KV_WIDTH, KV_WIDTH, KV_WIDTH, 3 * NSA_HEADS)

C_GQ, C_GK, C_GV, C_GR, C_NQ, C_KV, C_WIN, C_MISC = 0, 256, 512, 1024, 1536, 2048, 2560, 2816
MISC_GA, MISC_NG = 0, GLA_GATE_RANK
N_Z = 3072

VMEM_LIMIT_BYTES = 56 * 1024 * 1024
SLOPES = tuple(2.0 ** (-8.0 * h / NSA_HEADS) for h in range(1, NSA_HEADS + 1))


def _cparams(*sem):
    return pltpu.CompilerParams(dimension_semantics=sem, vmem_limit_bytes=VMEM_LIMIT_BYTES)


def _dot(a, b):
    return jnp.dot(a, b, preferred_element_type=F32)


def _dot_nt(a, b):
    return lax.dot_general(a, b, (((1,), (1,)), ((), ())), preferred_element_type=F32)


def _rms_rows(x, gain):
    return x * lax.rsqrt(jnp.mean(x * x, axis=-1, keepdims=True) + RMS_EPS) * gain


def _seg_rms64(x, gain):
    rows, width = x.shape
    lo_lane = (lax.broadcasted_iota(jnp.int32, (1, 128), 1) < 64)
    outs = []
    for c in range(width // 128):
        xb = x[:, c * 128:(c + 1) * 128]
        sq = xb * xb
        lo = jnp.sum(jnp.where(lo_lane, sq, 0.0), axis=-1, keepdims=True)
        hi = jnp.sum(jnp.where(lo_lane, 0.0, sq), axis=-1, keepdims=True)
        ms = jnp.where(lo_lane, lo, hi) * (1.0 / 64.0)
        outs.append(xb * lax.rsqrt(ms + RMS_EPS))
    y = outs[0] if len(outs) == 1 else jnp.concatenate(outs, axis=-1)
    return y * gain


def _mod_body(c_ref, w_ref, b_ref, o_ref):
    c = c_ref[...]
    a = c * jax.nn.sigmoid(c)
    o_ref[...] = _dot(a.astype(BF16), w_ref[...].astype(BF16)) + b_ref[...]


def _mod_call(c_all, w_ada, b_ada):
    depth, d, n = w_ada.shape
    bc = c_all.shape[0]
    tn = 1536
    return pl.pallas_call(
        _mod_body,
        grid=(depth, n // tn),
        in_specs=[pl.BlockSpec((bc, d), lambda l, j: (0, 0)),
                  pl.BlockSpec((None, d, tn), lambda l, j: (l, 0, j)),
                  pl.BlockSpec((None, 1, tn), lambda l, j: (l, 0, j))],
        out_specs=pl.BlockSpec((None, bc, tn), lambda l, j: (l, 0, j)),
        out_shape=jax.ShapeDtypeStruct((depth, bc, n), F32),
        compiler_params=_cparams("parallel", "parallel"),
        name="adaln_mod",
    )(c_all, w_ada, b_ada.reshape(depth, 1, n))


def _inproj_body(x_ref, g_ref, sc_ref, sh_ref, w_ref, z_ref, h_scr):
    @pl.when(pl.program_id(1) == 0)
    def _():
        y = _rms_rows(x_ref[...], g_ref[...])
        h_scr[...] = (y * (1.0 + sc_ref[...]) + sh_ref[...]).astype(BF16)

    z_ref[...] = _dot(h_scr[...], w_ref[...])


def _mod_spec(mod, tm, rows_per_batch):
    if mod.ndim == 3:
        return pl.BlockSpec((None, 1, D_MODEL), lambda i, *_: ((i * tm) // rows_per_batch, 0, 0))
    return pl.BlockSpec((tm, D_MODEL), lambda i, *_: (i, 0))


def _inproj_call(x2d, gain, sc, sh, w_bf, tm, rows_per_batch):
    m = x2d.shape[0]
    tn = 1024
    return pl.pallas_call(
        _inproj_body,
        grid=(m // tm, N_Z // tn),
        in_specs=[pl.BlockSpec((tm, D_MODEL), lambda i, j: (i, 0)),
                  pl.BlockSpec((1, D_MODEL), lambda i, j: (0, 0)),
                  _mod_spec(sc, tm, rows_per_batch),
                  _mod_spec(sh, tm, rows_per_batch),
                  pl.BlockSpec((D_MODEL, tn), lambda i, j: (0, j))],
        out_specs=pl.BlockSpec((tm, tn), lambda i, j: (i, j)),
        out_shape=jax.ShapeDtypeStruct((m, N_Z), F32),
        scratch_shapes=[pltpu.VMEM((tm, D_MODEL), BF16)],
        compiler_params=_cparams("parallel", "arbitrary"),
        name="inproj",
    )(x2d, gain, sc, sh, w_bf)


def _outproj_body(og_ref, on_ref, x_ref, g1_ref, w_ref, n2_ref, sc_ref, sh_ref, xo_ref, h2_ref):
    mix = (_dot(og_ref[...].astype(BF16), w_ref[:GLA_WIDTH, :])
           + _dot(on_ref[...].astype(BF16), w_ref[GLA_WIDTH:, :]))
    xn = x_ref[...] + g1_ref[...] * mix
    xo_ref[...] = xn
    y = _rms_rows(xn, n2_ref[...])
    h2_ref[...] = (y * (1.0 + sc_ref[...]) + sh_ref[...]).astype(BF16)


def _outproj_call(o_g, o_n, x2d, g1, w_bf, gain2, sc2, sh2, tm, rows_per_batch):
    m = x2d.shape[0]
    return pl.pallas_call(
        _outproj_body,
        grid=(m // tm,),
        in_specs=[pl.BlockSpec((tm, GLA_WIDTH), lambda i: (i, 0)),
                  pl.BlockSpec((tm, NSA_WIDTH), lambda i: (i, 0)),
                  pl.BlockSpec((tm, D_MODEL), lambda i: (i, 0)),
                  _mod_spec(g1, tm, rows_per_batch),
                  pl.BlockSpec((D_MODEL, D_MODEL), lambda i: (0, 0)),
                  pl.BlockSpec((1, D_MODEL), lambda i: (0, 0)),
                  _mod_spec(sc2, tm, rows_per_batch),
                  _mod_spec(sh2, tm, rows_per_batch)],
        out_specs=[pl.BlockSpec((tm, D_MODEL), lambda i: (i, 0)),
                   pl.BlockSpec((tm, D_MODEL), lambda i: (i, 0))],
        out_shape=[jax.ShapeDtypeStruct((m, D_MODEL), F32),
                   jax.ShapeDtypeStruct((m, D_MODEL), BF16)],
        compiler_params=_cparams("parallel"),
        name="outproj",
    )(o_g, o_n, x2d, g1, w_bf, gain2, sc2, sh2)


def _gelu_tanh(x):
    return x * (0.5 * (1.0 + jnp.tanh(0.7978845608028654 * (x + 0.044715 * (x * x * x)))))


def _ffn_body(*refs, tm, period, per_row_prev, keep):
    if per_row_prev:
        (h2_ref, x_ref, g2_ref, wu_ref, wg_ref, wd_ref, cw_ref, cb_ref, p1_ref, p2_ref,
         xo_ref, uk_ref, acc_scr, u_scr, g_scr) = refs
    else:
        (h2_ref, x_ref, g2_ref, wu_ref, wg_ref, wd_ref, cw_ref, cb_ref,
         xo_ref, uk_ref, acc_scr, u_scr, g_scr, carry_scr) = refs
    i, f = pl.program_id(0), pl.program_id(1)
    new, old = f % 2, (f + 1) % 2

    @pl.when(f == 0)
    def _():
        acc_scr[...] = jnp.zeros(acc_scr.shape, F32)
        u_scr[1] = jnp.zeros(u_scr.shape[1:], F32)
        g_scr[1] = jnp.zeros(g_scr.shape[1:], F32)

    if not per_row_prev:
        @pl.when((f == 0) & (i == 0))
        def _():
            carry_scr[...] = jnp.zeros(carry_scr.shape, F32)

    u = u_scr[old]
    gt = g_scr[old]
    row = lax.broadcasted_iota(jnp.int32, (tm, 1), 0)
    if per_row_prev:
        t = row % period
        m1, m2 = t >= 1, t >= 2
        prev1, prev2 = p1_ref[...], p2_ref[...]
    else:
        ft = jnp.maximum(f - 1, 0)
        fresh = ((i * tm) % period) == 0
        saved = carry_scr[ft]
        c = jnp.where(fresh, 0.0, saved)
        m1, m2 = row >= 1, row >= 2
        prev1 = c[7:8, :]
        prev2 = jnp.where(row == 0, c[6:7, :], c[7:8, :])
        carry_scr[ft] = jnp.where(f > 0, u[tm - 8:, :], saved)
    u_m1 = jnp.where(m1, pltpu.roll(u, 1, 0), prev1)
    u_m2 = jnp.where(m2, pltpu.roll(u, 2, 0), prev2)
    conv = u_m2 * cw_ref[0:1, :] + u_m1 * cw_ref[1:2, :] + u * cw_ref[2:3, :] + cb_ref[...]
    act = _gelu_tanh(conv) * gt
    acc_scr[...] += _dot(act.astype(BF16), wd_ref[...])
    uk_ref[...] = u[tm - keep:, :]

    h2 = h2_ref[...]
    u_scr[new] = _dot(h2, wu_ref[...])
    g_scr[new] = _dot(h2, wg_ref[...])

    @pl.when(f == pl.num_programs(1) - 1)
    def _():
        xo_ref[...] = x_ref[...] + g2_ref[...] * acc_scr[...]


def _ffn_call(h2, x2d, g2, wu, wg, wd, cw, cb, tm, period, prev=None):
    m = x2d.shape[0]
    tf = 256
    nf = D_FF // tf
    per_row_prev = prev is not None
    keep = tm if per_row_prev else 8
    body = functools.partial(_ffn_body, tm=tm, period=period, per_row_prev=per_row_prev, keep=keep)
    cur = lambda f: jnp.minimum(f, nf - 1)
    prv = lambda f: jnp.maximum(f - 1, 0)
    in_specs = [pl.BlockSpec((tm, D_MODEL), lambda i, f: (i, 0)),
                pl.BlockSpec((tm, D_MODEL), lambda i, f: (i, 0)),
                _mod_spec(g2, tm, period),
                pl.BlockSpec((D_MODEL, tf), lambda i, f: (0, cur(f))),
                pl.BlockSpec((D_MODEL, tf), lambda i, f: (0, cur(f))),
                pl.BlockSpec((tf, D_MODEL), lambda i, f: (prv(f), 0)),
                pl.BlockSpec((CONV_W, tf), lambda i, f: (0, prv(f))),
                pl.BlockSpec((1, tf), lambda i, f: (0, prv(f)))]
    args = [h2, x2d, g2, wu, wg, wd, cw, cb]
    scratch = [pltpu.VMEM((tm, D_MODEL), F32), pltpu.VMEM((2, tm, tf), F32), pltpu.VMEM((2, tm, tf), F32)]
    if per_row_prev:
        in_specs += [pl.BlockSpec((tm, tf), lambda i, f: (i, prv(f)))] * 2
        args += list(prev)
    else:
        scratch.append(pltpu.VMEM((nf, 8, tf), F32))
    return pl.pallas_call(
        body,
        grid=(m // tm, nf + 1),
        in_specs=in_specs,
        out_specs=[pl.BlockSpec((tm, D_MODEL), lambda i, f: (i, 0)),
                   pl.BlockSpec((keep, tf), lambda i, f: (i, prv(f)))],
        out_shape=[jax.ShapeDtypeStruct((m, D_MODEL), F32),
                   jax.ShapeDtypeStruct((m // tm * keep, D_FF), F32)],
        scratch_shapes=scratch,
        compiler_params=_cparams("arbitrary", "arbitrary"),
        name="convffn",
    )(*args)


GLA_ROWS = 256


def _log_sigmoid(x):
    return jnp.minimum(x, 0.0) - jnp.log1p(jnp.exp(-jnp.abs(x)))


def _gla_body(gq_ref, gk_ref, gv_ref, gr_ref, misc_ref, wa_ref, ba_ref, gn_ref, s0_ref,
              o_ref, sT_out_ref, sT_scr, *, rows_in, t_valid):
    R, C = _round_up(rows_in, 128), GLA_CHUNK
    nchunk = R // C
    n_live = min(nchunk, -(-min(t_valid, rows_in) // C))
    step = pl.program_id(1)

    @pl.when(step == 0)
    def _():
        for h in range(GLA_HEADS):
            sT_scr[h] = s0_ref[h].T

    def padded(ref):
        x = ref[...]
        if rows_in < R:
            x = jnp.concatenate([x, jnp.zeros((R - rows_in, x.shape[1]), x.dtype)], axis=0)
        return x

    row = lax.broadcasted_iota(jnp.int32, (R, 1), 0)
    live = row < t_valid
    q = padded(gq_ref) * (GLA_DK ** -0.5)
    k = jnp.where(live, padded(gk_ref), 0.0)
    v = jnp.where(live, padded(gv_ref), 0.0)
    xa = _dot(padded(misc_ref).astype(BF16), wa_ref[...]) + ba_ref[...]
    la = jnp.where(live, _log_sigmoid(xa) * (1.0 / GLA_TAU), 0.0)

    rc = row % C
    b = la
    shift = 1
    while shift < C:
        b = b + jnp.where(rc >= shift, pltpu.roll(b, shift, 0), 0.0)
        shift *= 2
    b3 = b.reshape(nchunk, C, GLA_HEADS * GLA_DK)
    b_last = jnp.broadcast_to(b3[:, C - 1:C, :], b3.shape).reshape(R, GLA_HEADS * GLA_DK)
    qe = (q * jnp.exp(b)).astype(BF16)
    ke = (k * jnp.exp(-b)).astype(BF16)
    kd = (k * jnp.exp(b_last - b)).astype(BF16)
    decay = jnp.exp(b_last)

    ri = lax.broadcasted_iota(jnp.int32, (R, R), 0)
    ci = lax.broadcasted_iota(jnp.int32, (R, R), 1)
    causal = (ri // C == ci // C) & (ci <= ri)
    col_chunk = lax.broadcasted_iota(jnp.int32, (GLA_DV, R), 1) // C

    outs = []
    for h in range(GLA_HEADS):
        ks = slice(h * GLA_DK, (h + 1) * GLA_DK)
        vs = slice(h * GLA_DV, (h + 1) * GLA_DV)
        v_h = v[:, vs]
        att = jnp.where(causal, _dot_nt(qe[:, ks], ke[:, ks]), 0.0)
        o_h = _dot(att.astype(BF16), v_h.astype(BF16))
        vT = v_h.T
        stack = jnp.concatenate([jnp.where(col_chunk == n, vT, 0.0) for n in range(n_live)], axis=0)
        incT = _dot(stack.astype(BF16), kd[:, ks])
        sT = sT_scr[h]
        inter = []
        for n in range(n_live):
            inter.append(_dot_nt(qe[n * C:(n + 1) * C, ks], sT.astype(BF16)))
            sT = decay[n * C + C - 1:n * C + C, ks] * sT + incT[n * GLA_DV:(n + 1) * GLA_DV, :]
        sT_scr[h] = sT
        if n_live < nchunk:
            inter.append(jnp.zeros(((nchunk - n_live) * C, GLA_DV), F32))
        o_h = o_h + jnp.concatenate(inter, axis=0)
        outs.append(_rms_rows(o_h, gn_ref[:, vs]))
    o = jnp.concatenate(outs, axis=-1)
    gr = padded(gr_ref)
    o = o * (gr * jax.nn.sigmoid(gr))
    o_ref[...] = o[:rows_in, :]

    @pl.when(step == pl.num_programs(1) - 1)
    def _():
        sT_out_ref[...] = sT_scr[...]


def _gla_call(z, wa_pad_bf, ba, gnorm, s0, n_batch, rows_per_batch, rows_in, t_valid):
    steps = rows_per_batch // rows_in
    body = functools.partial(_gla_body, rows_in=rows_in, t_valid=t_valid)

    def zspec(col, width):
        return pl.BlockSpec((rows_in, width), lambda b, s, c=col // width: (b * steps + s, c))

    return pl.pallas_call(
        body,
        grid=(n_batch, steps),
        in_specs=[zspec(C_GQ, 256), zspec(C_GK, 256), zspec(C_GV, 512), zspec(C_GR, 512),
                  zspec(C_MISC, 128),
                  pl.BlockSpec((128, GLA_HEADS * GLA_DK), lambda b, s: (0, 0)),
                  pl.BlockSpec((1, GLA_HEADS * GLA_DK), lambda b, s: (0, 0)),
                  pl.BlockSpec((1, GLA_WIDTH), lambda b, s: (0, 0)),
                  pl.BlockSpec((None, GLA_HEADS, GLA_DK, GLA_DV), lambda b, s: (b, 0, 0, 0))],
        out_specs=[pl.BlockSpec((rows_in, GLA_WIDTH), lambda b, s: (b * steps + s, 0)),
                   pl.BlockSpec((None, GLA_HEADS, GLA_DV, GLA_DK), lambda b, s: (b, 0, 0, 0))],
        out_shape=[jax.ShapeDtypeStruct((n_batch * rows_per_batch, GLA_WIDTH), F32),
                   jax.ShapeDtypeStruct((n_batch, GLA_HEADS, GLA_DV, GLA_DK), F32)],
        scratch_shapes=[pltpu.VMEM((GLA_HEADS, GLA_DV, GLA_DK), F32)],
        compiler_params=_cparams("parallel", "arbitrary"),
        name="gla",
    )(z, z, z, z, z, wa_pad_bf, ba, gnorm, s0)


def _nsa_prep_body(nq_ref, kv_ref, win_ref, qg_ref, ksg_ref, kwg_ref,
                   q_out, kv_out, win_out, kvs_bf, win_bf):
    q = _seg_rms64(nq_ref[...], qg_ref[...]) * (NSA_DH ** -0.5)
    q_out[...] = q.astype(BF16)
    kv = kv_ref[...]
    ksn = _seg_rms64(kv[:, 2 * KV_WIDTH:3 * KV_WIDTH], ksg_ref[...])
    vs = kv[:, 3 * KV_WIDTH:]
    kv_out[:, :2 * KV_WIDTH] = kv[:, :2 * KV_WIDTH]
    kv_out[:, 2 * KV_WIDTH:3 * KV_WIDTH] = ksn
    kv_out[:, 3 * KV_WIDTH:] = vs
    kvs_bf[:, :KV_WIDTH] = ksn.astype(BF16)
    kvs_bf[:, KV_WIDTH:] = vs.astype(BF16)
    w = win_ref[...]
    kwn = _seg_rms64(w[:, :KV_WIDTH], kwg_ref[...])
    win_out[:, :KV_WIDTH] = kwn
    win_out[:, KV_WIDTH:] = w[:, KV_WIDTH:]
    win_bf[:, :KV_WIDTH] = kwn.astype(BF16)
    win_bf[:, KV_WIDTH:] = w[:, KV_WIDTH:].astype(BF16)


def _nsa_prep_call(z, qg, ksg, kwg, tm):
    m = z.shape[0]

    def zspec(col, width):
        return pl.BlockSpec((tm, width), lambda i, c=col // width: (i, c))

    def ospec(width):
        return pl.BlockSpec((tm, width), lambda i: (i, 0))

    return pl.pallas_call(
        _nsa_prep_body,
        grid=(m // tm,),
        in_specs=[zspec(C_NQ, 512), zspec(C_KV, 512), zspec(C_WIN, 256),
                  pl.BlockSpec((1, 512), lambda i: (0, 0)),
                  pl.BlockSpec((1, 128), lambda i: (0, 0)),
                  pl.BlockSpec((1, 128), lambda i: (0, 0))],
        out_specs=[ospec(512), ospec(512), ospec(256), ospec(256), ospec(256)],
        out_shape=[jax.ShapeDtypeStruct((m, 512), BF16),
                   jax.ShapeDtypeStruct((m, 512), F32),
                   jax.ShapeDtypeStruct((m, 256), F32),
                   jax.ShapeDtypeStruct((m, 256), BF16),
                   jax.ShapeDtypeStruct((m, 256), BF16)],
        compiler_params=_cparams("parallel"),
        name="nsa_prep",
    )(z, z, z, qg, ksg, kwg)


def _with_pos_lanes(keys, blk, lo):
    lane = lax.broadcasted_iota(jnp.int32, keys.shape, 1) - NSA_DH
    pos = jnp.where(lane == 0, lo, jnp.where((lane > 0) & (lane == blk), 1, 0)).astype(F32)
    return jnp.where(lane < 0, keys, pos).astype(BF16)


def _group_keys(k2, g):
    return k2 if g == 0 else pltpu.roll(k2, NSA_DH, 1)


def _nsa_prep_t_body(nq_ref, kv_ref, win_ref, misc_ref, qg_ref, ksg_ref, kwg_ref,
                     qt_out, gt_out, kv_out, win_out, ksel_out, vselt_out, kwin_out, vwint_out,
                     *, tm, length):
    q = _seg_rms64(nq_ref[...], qg_ref[...]) * (NSA_DH ** -0.5)
    qt_out[...] = q.T.astype(BF16)
    gt_out[...] = jax.nn.sigmoid(misc_ref[...]).T[MISC_NG:MISC_NG + 32, :]
    kv = kv_ref[...]
    ksn = _seg_rms64(kv[:, 2 * KV_WIDTH:3 * KV_WIDTH], ksg_ref[...])
    vs = kv[:, 3 * KV_WIDTH:]
    kv_out[:, :2 * KV_WIDTH] = kv[:, :2 * KV_WIDTH]
    kv_out[:, 2 * KV_WIDTH:3 * KV_WIDTH] = ksn
    kv_out[:, 3 * KV_WIDTH:] = vs
    w = win_ref[...]
    kwn = _seg_rms64(w[:, :KV_WIDTH], kwg_ref[...])
    vw = w[:, KV_WIDTH:]
    win_out[:, :KV_WIDTH] = kwn
    win_out[:, KV_WIDTH:] = vw

    pos = (pl.program_id(0) * tm + lax.broadcasted_iota(jnp.int32, (tm, 1), 0)) % length
    lane = lax.broadcasted_iota(jnp.int32, (tm, KV_WIDTH), 1)
    for g in range(NSA_G):
        gs = slice(g * KV_WIDTH, (g + 1) * KV_WIDTH)
        ksel_out[:, gs] = _with_pos_lanes(_group_keys(ksn, g), pos // SEL_BLOCK, pos % SEL_BLOCK)
        kwin_out[:, gs] = jnp.where(lane < NSA_DH, _group_keys(kwn, g), 0.0).astype(BF16)
    ones = jnp.where(lax.broadcasted_iota(jnp.int32, (NSA_DH, tm), 0) == 0, 1.0, 0.0)

    def values_t(v):
        vt = v.T
        return jnp.concatenate([vt[:NSA_DH], ones, vt[NSA_DH:], ones], axis=0).astype(BF16)

    vselt_out[...] = values_t(vs)
    vwint_out[...] = values_t(vw)


def _nsa_prep_t_call(z, qg, ksg, kwg, tm, length):
    m = z.shape[0]

    def zspec(col, width):
        return pl.BlockSpec((tm, width), lambda i, c=col // width: (i, c))

    def rows(width):
        return pl.BlockSpec((tm, width), lambda i: (i, 0))

    def cols(height):
        return pl.BlockSpec((height, tm), lambda i: (0, i))

    return pl.pallas_call(
        functools.partial(_nsa_prep_t_body, tm=tm, length=length),
        grid=(m // tm,),
        in_specs=[zspec(C_NQ, 512), zspec(C_KV, 512), zspec(C_WIN, 256), zspec(C_MISC, 128),
                  pl.BlockSpec((1, 512), lambda i: (0, 0)),
                  pl.BlockSpec((1, 128), lambda i: (0, 0)),
                  pl.BlockSpec((1, 128), lambda i: (0, 0))],
        out_specs=[cols(NSA_WIDTH), cols(32), rows(512), rows(256), rows(256), cols(256), rows(256), cols(256)],
        out_shape=[jax.ShapeDtypeStruct((NSA_WIDTH, m), BF16),
                   jax.ShapeDtypeStruct((32, m), F32),
                   jax.ShapeDtypeStruct((m, 512), F32),
                   jax.ShapeDtypeStruct((m, 256), F32),
                   jax.ShapeDtypeStruct((m, 256), BF16),
                   jax.ShapeDtypeStruct((256, m), BF16),
                   jax.ShapeDtypeStruct((m, 256), BF16),
                   jax.ShapeDtypeStruct((256, m), BF16)],
        compiler_params=_cparams("parallel"),
        name="nsa_prep_t",
    )(z, z, z, z, qg, ksg, kwg)


def _compress_rows(load_rows, w_ref, kind, nblk):
    acc = jnp.zeros((nblk, 2 * KV_WIDTH), F32)
    for j in range(CMP_STRIDE):
        acc = acc + _dot(load_rows(j).astype(BF16), w_ref[kind, j])
    row = lax.broadcasted_iota(jnp.int32, (nblk, 1), 0)
    bot_next = pltpu.roll(acc[:, KV_WIDTH:], nblk - 1, 0)
    return jnp.where(row < nblk - 1, acc[:, :KV_WIDTH] + bot_next, 0.0)


def _compress_body(krows_ref, vrows_ref, w_ref, g_ref, kc_out, vct_out, *, nblk):
    def loader(ref):
        return lambda j: ref[pl.ds(j, nblk, stride=CMP_STRIDE), :]

    kc = _compress_rows(loader(krows_ref), w_ref, 0, nblk)
    vc = _compress_rows(loader(vrows_ref), w_ref, 1, nblk)
    kcn = _seg_rms64(kc, g_ref[...])
    ec = CMP_STRIDE * lax.broadcasted_iota(jnp.int32, (nblk, 1), 0) + (CMP_LEN - 1)
    for g in range(NSA_G):
        kc_out[:, g * KV_WIDTH:(g + 1) * KV_WIDTH] = _with_pos_lanes(
            _group_keys(kcn, g), ec // SEL_BLOCK, ec % SEL_BLOCK)
    vct_out[...] = vc.T.astype(BF16)


def _compress_call(kv_new3, w_cmp, kcg):
    nb, length, _ = kv_new3.shape
    nblk = length // CMP_STRIDE
    return pl.pallas_call(
        functools.partial(_compress_body, nblk=nblk),
        grid=(nb,),
        in_specs=[pl.BlockSpec((None, length, KV_WIDTH), lambda b: (b, 0, 0)),
                  pl.BlockSpec((None, length, KV_WIDTH), lambda b: (b, 0, 1)),
                  pl.BlockSpec((2, CMP_STRIDE, KV_WIDTH, 2 * KV_WIDTH), lambda b: (0, 0, 0, 0)),
                  pl.BlockSpec((1, KV_WIDTH), lambda b: (0, 0))],
        out_specs=[pl.BlockSpec((None, nblk, 2 * KV_WIDTH), lambda b: (b, 0, 0)),
                   pl.BlockSpec((None, KV_WIDTH, nblk), lambda b: (b, 0, 0))],
        out_shape=[jax.ShapeDtypeStruct((nb, nblk, 2 * KV_WIDTH), BF16),
                   jax.ShapeDtypeStruct((nb, KV_WIDTH, nblk), BF16)],
        compiler_params=_cparams("parallel"),
        name="nsa_compress",
    )(kv_new3, kv_new3, w_cmp, kcg)


def _round_up(a, b):
    return -(-a // b) * b


def _tile_rows(a, n):
    return jnp.concatenate([a] * n, axis=0)


def _softmax_rows(s, valid):
    s = jnp.where(valid, s, -BIG)
    m = jnp.max(s, axis=-1, keepdims=True)
    e = jnp.where(valid, jnp.exp(s - m), 0.0)
    l = jnp.sum(e, axis=-1, keepdims=True)
    return e / jnp.where(l > 0.0, l, 1.0)


def _sel_rows(ns):
    return _round_up(ns, 8)


def _imp_scratch_rows(ns):
    return 8 + SEL_PER * _sel_rows(ns) + 8


def _select_blocks_t(imp_t, imp_scr, sc_scr, t_row, ns, n_sel, n_rank=None):
    nb, tq = imp_t.shape
    nsr = _sel_rows(ns)
    rows = _imp_scratch_rows(ns)
    imp_scr[0:8, :] = jnp.zeros((8, tq), F32)
    imp_scr[8 + nb:rows, :] = jnp.zeros((rows - 8 - nb, tq), F32)
    imp_scr[8:8 + nb, :] = imp_t

    def ld(off):
        return imp_scr[pl.ds(8 + off, nsr, stride=SEL_PER), :]

    p_slc = ld(-1) + 2.0 * (ld(0) + ld(1) + ld(2)) + ld(3)
    j = lax.broadcasted_iota(jnp.int32, (nsr, 1), 0)
    cur = t_row // SEL_BLOCK
    forced = (j == 0) | (j == cur) | (j == cur - 1)
    score = jnp.where(forced, BIG, jnp.where(j <= cur, p_slc, -BIG))
    sc_scr[...] = score

    def rank_body(k, cnt):
        sk = sc_scr[pl.ds(k, 1), :]
        beats = (sk > score) | ((sk == score) & (k < j))
        return cnt + jnp.where(beats, 1.0, 0.0)

    cnt = lax.fori_loop(0, ns if n_rank is None else n_rank, rank_body, jnp.zeros((nsr, tq), F32))
    return jnp.where((cnt < float(n_sel)) & (j < ns), 1.0, 0.0)


def _select_blocks(imp, imp_scr, sc_scr, t_row, ns, n_sel):
    sel_t = _select_blocks_t(imp.T, imp_scr, sc_scr, t_row, ns, n_sel)
    nsp = _round_up(ns, 128)
    if nsp > sel_t.shape[0]:
        sel_t = jnp.concatenate([sel_t, jnp.zeros((nsp - sel_t.shape[0], sel_t.shape[1]), F32)], axis=0)
    return sel_t.T.astype(BF16)


def _online_update(carry, s, valid, pv):
    m, l, acc = carry
    s = jnp.where(valid, s, -BIG)
    m_new = jnp.maximum(m, jnp.max(s, axis=-1, keepdims=True))
    alpha = jnp.exp(m - m_new)
    e = jnp.where(valid, jnp.exp(s - m_new), 0.0)
    l = alpha * l + jnp.sum(e, axis=-1, keepdims=True)
    acc = alpha * acc + pv(e.astype(BF16))
    return m_new, l, acc


def _online_init(rows, width):
    return (jnp.full((rows, 1), -BIG, F32), jnp.zeros((rows, 1), F32), jnp.zeros((rows, width), F32))


def _online_finish(carry):
    _, l, acc = carry
    return acc / jnp.where(l > 0.0, l, 1.0)


NEG_MASK = -(2.0 ** 100)
CMP_TQ = 128
ATT_TQ = 256
ATT_TK = 512


def _slope_lanes(g, tq):
    r = lax.broadcasted_iota(jnp.int32, (1, NSA_REP * tq), 1) // tq
    out = jnp.full((1, NSA_REP * tq), SLOPES[g * NSA_REP], F32)
    for k in range(1, NSA_REP):
        out = jnp.where(r == k, SLOPES[g * NSA_REP + k], out)
    return out


def _q_aug_t(qt_ref, g, tq, qsel_t):
    lanes = NSA_REP * tq
    qrows = jnp.concatenate(
        [qt_ref[(g * NSA_REP + r) * NSA_DH:(g * NSA_REP + r + 1) * NSA_DH, :] for r in range(NSA_REP)], axis=1)
    j = lax.broadcasted_iota(jnp.int32, (NSA_DH, lanes), 0)
    slope = _slope_lanes(g, tq)
    aug = jnp.where(j == 0, slope, slope * (float(SEL_BLOCK) * j.astype(F32)))
    if qsel_t is not None:
        aug = jnp.where(j == 0, aug, aug + _tile_lanes(qsel_t.astype(F32), NSA_REP))
    return jnp.concatenate([qrows, aug.astype(BF16)], axis=0)


def _tile_lanes(a, n):
    return jnp.concatenate([a] * n, axis=1)


def _cmp_select_body(qt_ref, kc_ref, vct_ref, oct_ref, qselt_ref, cnt_ref, imp_scr, sc_scr, *, length, tq):
    nb = length // CMP_STRIDE
    ns = length // SEL_BLOCK
    nsr = _sel_rows(ns)
    n_sel = min(SEL_TOP_N, ns)
    t0 = pl.program_id(1) * tq
    t_row = t0 + lax.broadcasted_iota(jnp.int32, (1, tq), 1)
    ec = CMP_STRIDE * lax.broadcasted_iota(jnp.int32, (nb, 1), 0) + (CMP_LEN - 1)
    hidden = _tile_lanes(jnp.where(t_row >= ec, 0.0, NEG_MASK), NSA_REP)
    counts = []
    for g in range(NSA_G):
        s_t = _dot(kc_ref[:, g * KV_WIDTH:(g + 1) * KV_WIDTH], _q_aug_t(qt_ref, g, tq, None)) + hidden
        m = jnp.max(s_t, axis=0, keepdims=True)
        e = jnp.where(hidden < 0.0, 0.0, jnp.exp(s_t - m))
        l = jnp.sum(e, axis=0, keepdims=True)
        p_t = e * jnp.where(l > 0.0, 1.0 / l, 0.0)
        o_ct = _dot(vct_ref[g * NSA_DH:(g + 1) * NSA_DH, :], p_t.astype(BF16))
        imp_t = p_t[:, 0:tq]
        for r in range(1, NSA_REP):
            imp_t = imp_t + p_t[:, r * tq:(r + 1) * tq]
        n_rank = jnp.minimum((t0 + tq - 1) // SEL_BLOCK + 1, ns)
        sel_t = _select_blocks_t(imp_t, imp_scr, sc_scr, t_row, ns, n_sel, n_rank)
        qsel = jnp.where(sel_t > 0.5, 0.0, NEG_MASK)
        if nsr < SEL_BLOCK:
            qsel = jnp.concatenate([qsel, jnp.full((SEL_BLOCK - nsr, tq), NEG_MASK, F32)], axis=0)
        qselt_ref[g * SEL_BLOCK:(g + 1) * SEL_BLOCK, :] = qsel.astype(BF16)
        cnt = _dot_nt(jnp.ones((8, tq), BF16), sel_t.astype(BF16))
        if nsr < SEL_BLOCK:
            cnt = jnp.concatenate([cnt, jnp.zeros((8, SEL_BLOCK - nsr), F32)], axis=1)
        counts.append(cnt)
        for r in range(NSA_REP):
            h = g * NSA_REP + r
            oct_ref[h * NSA_DH:(h + 1) * NSA_DH, :] = o_ct[:, r * tq:(r + 1) * tq]
    cnt_ref[...] = jnp.concatenate(counts, axis=1)


def _cmp_select_call(q_t, kc_aug, vc_t, n_batch, length):
    tq = CMP_TQ
    nq = length // tq
    nb = length // CMP_STRIDE
    ns = length // SEL_BLOCK
    assert ns <= SEL_BLOCK
    m = n_batch * length
    return pl.pallas_call(
        functools.partial(_cmp_select_body, length=length, tq=tq),
        grid=(n_batch, nq),
        in_specs=[pl.BlockSpec((NSA_WIDTH, tq), lambda b, i: (0, b * nq + i)),
                  pl.BlockSpec((None, nb, 2 * KV_WIDTH), lambda b, i: (b, 0, 0)),
                  pl.BlockSpec((None, KV_WIDTH, nb), lambda b, i: (b, 0, 0))],
        out_specs=[pl.BlockSpec((NSA_WIDTH, tq), lambda b, i: (0, b * nq + i)),
                   pl.BlockSpec((NSA_G * SEL_BLOCK, tq), lambda b, i: (0, b * nq + i)),
                   pl.BlockSpec((8, NSA_G * SEL_BLOCK), lambda b, i: (b * nq + i, 0))],
        out_shape=[jax.ShapeDtypeStruct((NSA_WIDTH, m), F32),
                   jax.ShapeDtypeStruct((NSA_G * SEL_BLOCK, m), BF16),
                   jax.ShapeDtypeStruct((n_batch * nq * 8, NSA_G * SEL_BLOCK), F32)],
        scratch_shapes=[pltpu.VMEM((_imp_scratch_rows(ns), tq), F32), pltpu.VMEM((_sel_rows(ns), tq), F32)],
        compiler_params=_cparams("parallel", "parallel"),
        name="nsa_cmp_select",
    )(q_t, kc_aug, vc_t)


def _nsa_attn_body(flags_ref, qt_ref, qselt_ref, gt_ref, oct_ref, ksel_ref, vselt_ref, kwin_ref, vwint_ref,
                   nn_ref, o_ref, m_scr, acc_scr, on_scr, *, length, tq, tk):
    b, i = pl.program_id(0), pl.program_id(1)
    nq = pl.num_programs(1)
    n_kt_all = length // tk
    t0 = i * tq
    t_row = t0 + lax.broadcasted_iota(jnp.int32, (1, tq), 1)
    span = WINDOW + tq
    w0 = pl.multiple_of(jnp.maximum(t0 - WINDOW, 0), tq)
    w_rel = lax.broadcasted_iota(jnp.int32, (span, 1), 0)
    in_window = (w0 + w_rel <= t_row) & (w0 + w_rel >= t_row - WINDOW)
    win_mask = _tile_lanes(jnp.where(in_window, 0.0, NEG_MASK), NSA_REP)
    for g in range(NSA_G):
        qa = _q_aug_t(qt_ref, g, tq, qselt_ref[g * SEL_BLOCK:(g + 1) * SEL_BLOCK, :])
        gs = slice(g * KV_WIDTH, (g + 1) * KV_WIDTH)
        vrows = slice(g * 2 * NSA_DH, (g + 1) * 2 * NSA_DH)

        def reset():
            m_scr[...] = jnp.full(m_scr.shape, -BIG, F32)
            acc_scr[...] = jnp.zeros(acc_scr.shape, F32)

        def step(k_ref, vt_ref, k0, rows, causal):
            s_t = _dot(k_ref[pl.ds(k0, rows), gs], qa)
            if causal:
                kpos = k0 + lax.broadcasted_iota(jnp.int32, (rows, 1), 0)
                s_t = s_t + _tile_lanes(jnp.where(kpos <= t_row, 0.0, NEG_MASK), NSA_REP)
            m_old = m_scr[...]
            m_new = jnp.maximum(m_old, jnp.max(s_t, axis=0, keepdims=True))
            e = jnp.exp(s_t - m_new).astype(BF16)
            acc_scr[...] = jnp.exp(m_old - m_new) * acc_scr[...] + _dot(vt_ref[vrows, pl.ds(k0, rows)], e)
            m_scr[...] = m_new

        def finish():
            acc = acc_scr[...]
            l = acc[NSA_DH:NSA_DH + 1, :]
            return acc[0:NSA_DH, :] / jnp.where(l > 0.0, l, 1.0)

        reset()
        last = (t0 + tq + tk - 1) // tk - 1
        fbase = ((b * nq + i) * NSA_G + g) * n_kt_all

        def sel_tile(kt, carry):
            @pl.when(flags_ref[fbase + kt] != 0)
            def _():
                step(ksel_ref, vselt_ref, pl.multiple_of(kt * tk, tk), tk, False)
            return carry

        lax.fori_loop(0, last, sel_tile, 0)
        step(ksel_ref, vselt_ref, pl.multiple_of(last * tk, tk), tk, True)
        o_s = finish()

        k_w = _with_pos_lanes(kwin_ref[pl.ds(w0, span), gs].astype(F32), w_rel // SEL_BLOCK, w_rel % SEL_BLOCK)
        s_w = _dot(k_w, _q_aug_t(qt_ref, g, tq, None)) + win_mask
        e = jnp.exp(s_w - jnp.max(s_w, axis=0, keepdims=True)).astype(BF16)
        acc = _dot(vwint_ref[vrows, pl.ds(w0, span)], e)
        o_w = acc[0:NSA_DH, :] / acc[NSA_DH:NSA_DH + 1, :]

        for r in range(NSA_REP):
            h = g * NSA_REP + r
            hs = slice(h * NSA_DH, (h + 1) * NSA_DH)
            ls = slice(r * tq, (r + 1) * tq)
            on_scr[hs, :] = (gt_ref[h:h + 1, :] * oct_ref[hs, :]
                             + gt_ref[NSA_HEADS + h:NSA_HEADS + h + 1, :] * o_s[:, ls]
                             + gt_ref[2 * NSA_HEADS + h:2 * NSA_HEADS + h + 1, :] * o_w[:, ls])
    on = on_scr[...]
    ms = jnp.sum(on * on, axis=0, keepdims=True) * (1.0 / NSA_WIDTH)
    o_ref[...] = (on * lax.rsqrt(ms + RMS_EPS) * nn_ref[...]).T


def _nsa_attn_call(flags, q_t, qsel_t, g_t, oc_t, ksel, vsel_t, kwin, vwin_t, nn_col, n_batch, length):
    tq, tk = ATT_TQ, ATT_TK
    nq = length // tq
    assert WINDOW % tq == 0 and length % tk == 0 and tk % tq == 0 and length >= WINDOW + tq
    body = functools.partial(_nsa_attn_body, length=length, tq=tq, tk=tk)

    def cols(rows):
        return pl.BlockSpec((rows, tq), lambda b, i, fl: (0, b * nq + i))

    def keys(width=2 * KV_WIDTH):
        return pl.BlockSpec((length, width), lambda b, i, fl: (b, 0))

    def values():
        return pl.BlockSpec((2 * KV_WIDTH, length), lambda b, i, fl: (0, b))

    grid_spec = pltpu.PrefetchScalarGridSpec(
        num_scalar_prefetch=1,
        grid=(n_batch, nq),
        in_specs=[cols(NSA_WIDTH), cols(NSA_G * SEL_BLOCK), cols(32), cols(NSA_WIDTH),
                  keys(), values(), keys(), values(),
                  pl.BlockSpec((NSA_WIDTH, 1), lambda b, i, fl: (0, 0))],
        out_specs=pl.BlockSpec((tq, NSA_WIDTH), lambda b, i, fl: (b * nq + i, 0)),
        scratch_shapes=[pltpu.VMEM((1, NSA_REP * tq), F32),
                        pltpu.VMEM((2 * NSA_DH, NSA_REP * tq), F32),
                        pltpu.VMEM((NSA_WIDTH, tq), F32)])
    return pl.pallas_call(
        body,
        grid_spec=grid_spec,
        out_shape=jax.ShapeDtypeStruct((n_batch * length, NSA_WIDTH), F32),
        compiler_params=_cparams("parallel", "parallel"),
        name="nsa_attn",
    )(flags, q_t, qsel_t, g_t, oc_t, ksel, vsel_t, kwin, vwin_t, nn_col)


def _prep_weights(p):
    depth = p['w_in'].shape[0]
    offs = np.concatenate([[0], np.cumsum(IN_SIZES)])
    piece = lambda i: p['w_in'][:, :, int(offs[i]):int(offs[i + 1])]
    order = [0, 1, 2, 4, 5, 6, 7, 8, 9, 10, 11, 3, 12]
    cols = [piece(i) for i in order]
    used = sum(IN_SIZES)
    cols.append(jnp.zeros((depth, D_MODEL, N_Z - used), F32))
    w_in = jnp.concatenate(cols, axis=-1).astype(BF16)
    wa = jnp.zeros((depth, 128, GLA_HEADS * GLA_DK), F32).at[:, :GLA_GATE_RANK, :].set(p['w_gla_a']).astype(BF16)

    def cmp_weights(w):
        eye = jnp.eye(NSA_G, dtype=F32)
        bd = jnp.einsum('gh,ljde->ljgdhe', eye, w).reshape(depth, CMP_LEN, KV_WIDTH, KV_WIDTH)
        return jnp.concatenate([bd[:, :CMP_STRIDE], bd[:, CMP_STRIDE:]], axis=-1)

    w_cmp = jnp.stack([cmp_weights(p['w_ck']), cmp_weights(p['w_cv'])], axis=1).astype(BF16)
    tile = lambda a, n: jnp.tile(a, (1, n))[:, None, :]
    return {
        'w_in': w_in, 'wa': wa, 'ba': p['b_gla_a'][:, None, :], 'w_cmp': w_cmp,
        'norm1': p['norm1'][:, None, :], 'norm2': p['norm2'][:, None, :],
        'gla_norm': p['gla_norm'][:, None, :], 'nsa_norm': p['nsa_norm'][:, None, :],
        'q_norm': tile(p['q_norm'], NSA_HEADS), 'kc_norm': tile(p['kc_norm'], NSA_G),
        'ks_norm': tile(p['ks_norm'], NSA_G), 'kw_norm': tile(p['kw_norm'], NSA_G),
        'w_out': p['w_out'].astype(BF16), 'w_up': p['w_up'].astype(BF16),
        'w_gate': p['w_gate'].astype(BF16), 'w_down': p['w_down'].astype(BF16),
        'conv_w': p['conv_w'], 'conv_b': p['conv_b'][:, None, :],
    }


def _split_mod(mod, shape):
    return [mod[:, k * D_MODEL:(k + 1) * D_MODEL].reshape(shape) for k in range(6)]


def _layer_prompt(x, mod, pw, l):
    nb, length, _ = x.shape
    m = nb * length
    sh1, sc1, g1, sh2, sc2, g2 = _split_mod(mod, (nb, 1, D_MODEL))
    x2 = x.reshape(m, D_MODEL)
    tm = min(512, length)
    z = _inproj_call(x2, pw['norm1'][l], sc1, sh1, pw['w_in'][l], min(1024, length), length)
    s0 = jnp.zeros((nb, GLA_HEADS, GLA_DK, GLA_DV), F32)
    o_g, s_t = _gla_call(z, pw['wa'][l], pw['ba'][l], pw['gla_norm'][l], s0, nb, length, GLA_ROWS, length)
    q_t, g_t, kv_new, win_new, ksel, vsel_t, kwin, vwin_t = _nsa_prep_t_call(
        z, pw['q_norm'][l], pw['ks_norm'][l], pw['kw_norm'][l], tm, length)
    kc_aug, vc_t = _compress_call(kv_new.reshape(nb, length, 4 * KV_WIDTH), pw['w_cmp'][l], pw['kc_norm'][l])
    oc_t, qsel_t, cnt = _cmp_select_call(q_t, kc_aug, vc_t, nb, length)
    per_tile = ATT_TK // SEL_BLOCK
    flags = cnt.reshape(-1, ATT_TQ // CMP_TQ, 8, NSA_G, SEL_BLOCK // per_tile, per_tile)
    flags = jnp.sum(flags[:, :, 0, :, :length // ATT_TK], axis=(1, -1))
    flags = (flags > 0.0).astype(jnp.int32).reshape(-1)
    o_n = _nsa_attn_call(flags, q_t, qsel_t, g_t, oc_t, ksel, vsel_t, kwin, vwin_t,
                         pw['nsa_norm'][l].reshape(NSA_WIDTH, 1), nb, length)
    x_mid, h2 = _outproj_call(o_g, o_n, x2, g1, pw['w_out'][l], pw['norm2'][l], sc2, sh2, tm, length)
    tm_f = min(1024, length)
    x_out, u_keep = _ffn_call(h2, x_mid, g2, pw['w_up'][l], pw['w_gate'][l], pw['w_down'][l],
                              pw['conv_w'][l], pw['conv_b'][l], tm_f, length)
    conv = u_keep.reshape(nb, length // tm_f, 8, D_FF)[:, -1, 8 - (CONV_W - 1):, :]
    wl = min(WINDOW, length)
    return (x_out.reshape(nb, length, D_MODEL),
            jnp.swapaxes(s_t, -1, -2),
            kv_new.reshape(nb, length, 4, NSA_G, NSA_DH),
            win_new.reshape(nb, length, 2 * KV_WIDTH)[:, length - wl:].reshape(nb, wl, 2, NSA_G, NSA_DH),
            conv)


SAMPLE_ROWS = 16


def _page_dma(pt_ref, ckv_hbm, buf, sem, layer, batch, slot, row0, n_pages, start):
    rows = buf.shape[1]

    def body(p, carry):
        cp = pltpu.make_async_copy(
            ckv_hbm.at[layer, pt_ref[batch, p], pl.ds(row0, rows), :],
            buf.at[slot, :, pl.ds(pl.multiple_of(p * PAGE_SIZE, PAGE_SIZE), PAGE_SIZE)],
            sem.at[slot])
        if start:
            cp.start()
        else:
            cp.wait()
        return carry

    lax.fori_loop(0, n_pages, body, 0)


def _gather_pages(pt_ref, ckv_hbm, buf, sem, layer, row0, n_pages):
    b = pl.program_id(0)
    dma = functools.partial(_page_dma, pt_ref, ckv_hbm, buf, sem, layer)

    @pl.when(b == 0)
    def _():
        dma(0, 0, row0, n_pages, True)

    @pl.when(b + 1 < pl.num_programs(0))
    def _():
        dma(b + 1, (b + 1) % 2, row0, n_pages, True)

    slot = b % 2
    dma(b, slot, row0, n_pages, False)
    return slot


def _q_stack(q_ref, g):
    return jnp.concatenate(
        [q_ref[:, (g * NSA_REP + r) * NSA_DH:(g * NSA_REP + r + 1) * NSA_DH] for r in range(NSA_REP)], axis=0)


def _slope_rows(g, rows):
    return jnp.concatenate([jnp.full((rows, 1), SLOPES[g * NSA_REP + r], F32) for r in range(NSA_REP)], axis=0)


def _s1_body(pt_ref, ckv_hbm, q_ref, w_ref, g_ref, oc_ref, sel_ref, buf, sem, rows_scr, imp_scr, sc_scr,
             *, layer, n_pages, past_len):
    slot = _gather_pages(pt_ref, ckv_hbm, buf, sem, layer, 0, n_pages)
    ts = SAMPLE_ROWS
    nblk = past_len // CMP_STRIDE
    ns = past_len // SEL_BLOCK + 1
    nsp = _round_up(ns, 128)
    n_sel = min(SEL_TOP_N, ns)
    chunk = min(512, past_len)
    for kind in range(2):
        for c in range(past_len // chunk):
            rows_scr[kind, c * chunk:(c + 1) * chunk, :] = (
                buf[slot, kind * KV_WIDTH:(kind + 1) * KV_WIDTH, c * chunk:(c + 1) * chunk].T)

    def loader(kind):
        return lambda j: rows_scr[kind, pl.ds(j, nblk, stride=CMP_STRIDE), :]

    kc = _seg_rms64(_compress_rows(loader(0), w_ref, 0, nblk), g_ref[...]).astype(BF16)
    vc = _compress_rows(loader(1), w_ref, 1, nblk).astype(BF16)

    t_col = past_len + lax.broadcasted_iota(jnp.int32, (ts, 1), 0)
    ec = CMP_STRIDE * lax.broadcasted_iota(jnp.int32, (1, nblk), 1) + (CMP_LEN - 1)
    dist = t_col - ec
    valid4 = _tile_rows(dist >= 0, NSA_REP)
    distf4 = _tile_rows(dist.astype(F32), NSA_REP)
    heads, imps = [], []
    for g in range(NSA_G):
        ks = slice(g * NSA_DH, (g + 1) * NSA_DH)
        s = _dot_nt(_q_stack(q_ref, g), kc[:, ks]) - _slope_rows(g, ts) * distf4
        p = _softmax_rows(s, valid4)
        o_c = _dot(p.astype(BF16), vc[:, ks])
        imp = p[0:ts]
        for r in range(1, NSA_REP):
            imp = imp + p[r * ts:(r + 1) * ts]
        imps.append(imp)
        heads += [o_c[r * ts:(r + 1) * ts] for r in range(NSA_REP)]
    oc_ref[...] = jnp.concatenate(heads, axis=-1)
    imp = jnp.concatenate(imps + [jnp.zeros((128 - NSA_G * ts, nblk), F32)], axis=0)
    t_row = past_len + lax.broadcasted_iota(jnp.int32, (1, 128), 1) % ts
    sel = _select_blocks(imp, imp_scr, sc_scr, t_row, ns, n_sel)
    for g in range(NSA_G):
        sel_ref[:, g * nsp:(g + 1) * nsp] = sel[g * ts:(g + 1) * ts, :].astype(F32)


def _s1_call(page_table, ckv, q_bf, w_cmp, kcg, layer, past_len):
    nbatch, n_pages = page_table.shape
    ts = SAMPLE_ROWS
    ns = past_len // SEL_BLOCK + 1
    nsp = _round_up(ns, 128)
    body = functools.partial(_s1_body, layer=layer, n_pages=n_pages, past_len=past_len)
    grid_spec = pltpu.PrefetchScalarGridSpec(
        num_scalar_prefetch=1,
        grid=(nbatch,),
        in_specs=[pl.BlockSpec(memory_space=pl.ANY),
                  pl.BlockSpec((ts, NSA_WIDTH), lambda b, pt: (b, 0)),
                  pl.BlockSpec((2, CMP_STRIDE, KV_WIDTH, 2 * KV_WIDTH), lambda b, pt: (0, 0, 0, 0)),
                  pl.BlockSpec((1, KV_WIDTH), lambda b, pt: (0, 0))],
        out_specs=[pl.BlockSpec((ts, NSA_WIDTH), lambda b, pt: (b, 0)),
                   pl.BlockSpec((ts, NSA_G * nsp), lambda b, pt: (b, 0))],
        scratch_shapes=[pltpu.VMEM((2, 2 * KV_WIDTH, past_len), F32),
                        pltpu.SemaphoreType.DMA((2,)),
                        pltpu.VMEM((2, past_len, KV_WIDTH), F32),
                        pltpu.VMEM((_imp_scratch_rows(ns), 128), F32),
                        pltpu.VMEM((_sel_rows(ns), 128), F32)])
    return pl.pallas_call(
        body,
        grid_spec=grid_spec,
        out_shape=[jax.ShapeDtypeStruct((nbatch * ts, NSA_WIDTH), F32),
                   jax.ShapeDtypeStruct((nbatch * ts, NSA_G * nsp), F32)],
        compiler_params=_cparams("arbitrary"),
        name="nsa_sample_select",
    )(page_table, ckv, q_bf, w_cmp, kcg)


def _s2_body(pt_ref, ckv_hbm, q_ref, misc_ref, oc_ref, sel_ref, kvn_ref, cwin_ref, wn_ref, e_ref, nn_ref,
             o_ref, buf, sem, *, layer, n_pages, past_len, wb, tk):
    slot = _gather_pages(pt_ref, ckv_hbm, buf, sem, layer, 2 * KV_WIDTH, n_pages)
    ts = SAMPLE_ROWS
    rows = NSA_HEADS * ts
    ns = past_len // SEL_BLOCK + 1
    nsp = sel_ref.shape[1] // NSA_G
    gates = jax.nn.sigmoid(misc_ref[:, MISC_NG:MISC_NG + 3 * NSA_HEADS])
    half = lax.broadcasted_iota(jnp.int32, (ts, KV_WIDTH), 1) // NSA_DH
    q_rows, slopes, sel_rows = [], [], []
    for h in range(NSA_HEADS):
        g = h // NSA_REP
        pair = q_ref[:, (h // 2) * KV_WIDTH:(h // 2 + 1) * KV_WIDTH].astype(F32)
        if h % 2 != g:
            pair = pltpu.roll(pair, NSA_DH, 1)
        q_rows.append(jnp.where(half == g, pair, 0.0))
        slopes.append(jnp.full((ts, 1), SLOPES[h], F32))
        sel_rows.append(sel_ref[:, g * nsp:(g + 1) * nsp])
    q_all = jnp.concatenate(q_rows, axis=0).astype(BF16)
    slope_rows = jnp.concatenate(slopes, axis=0)
    sel_all = jnp.concatenate(sel_rows, axis=0)
    sel_bf = sel_all.astype(BF16)
    t_col = past_len + lax.broadcasted_iota(jnp.int32, (rows, 1), 0) % ts
    zpad = jnp.zeros((128 - ts, KV_WIDTH), BF16)

    def rel(k0, n):
        return k0 - past_len + lax.broadcasted_iota(jnp.int32, (1, n), 1)

    def scores(keys_t, r):
        return _dot(q_all, keys_t) + slope_rows * r.astype(F32)

    def scores_new(ref):
        keys = jnp.concatenate([ref[:, :KV_WIDTH], zpad], axis=0)
        return _dot_nt(q_all, keys) + slope_rows * rel(past_len, 128).astype(F32)

    def pv_new(ref):
        vals = jnp.concatenate([ref[:, KV_WIDTH:], zpad], axis=0)
        return lambda e: _dot(e, vals)

    d_new = t_col - (past_len + lax.broadcasted_iota(jnp.int32, (1, 128), 1))

    def kt_body(kt, carry):
        k0 = pl.multiple_of(kt * tk, tk)
        valid = _dot(sel_bf, e_ref[:, pl.ds(k0, tk)]) > 0.5
        v_t = buf[slot, KV_WIDTH:, pl.ds(k0, tk)].astype(BF16)
        s = scores(buf[slot, :KV_WIDTH, pl.ds(k0, tk)].astype(BF16), rel(k0, tk))
        return _online_update(carry, s, valid, lambda e: _dot_nt(e, v_t))

    carry = lax.fori_loop(0, past_len // tk, kt_body, _online_init(rows, KV_WIDTH))
    valid = (sel_all[:, ns - 1:ns] > 0.5) & (d_new >= 0)
    o_s = _online_finish(_online_update(carry, scores_new(kvn_ref), valid, pv_new(kvn_ref)))

    d_win = t_col - (past_len - wb + lax.broadcasted_iota(jnp.int32, (1, wb), 1))
    vw_t = cwin_ref[KV_WIDTH:, :].astype(BF16)
    carry = _online_update(_online_init(rows, KV_WIDTH),
                           scores(cwin_ref[:KV_WIDTH, :].astype(BF16), rel(past_len - wb, wb)),
                           (d_win >= 0) & (d_win <= WINDOW), lambda e: _dot_nt(e, vw_t))
    valid = (d_new >= 0) & (d_new <= WINDOW)
    o_w = _online_finish(_online_update(carry, scores_new(wn_ref), valid, pv_new(wn_ref)))

    heads = []
    for h in range(NSA_HEADS):
        g = h // NSA_REP
        rs = slice(h * ts, (h + 1) * ts)
        ls = slice(g * NSA_DH, (g + 1) * NSA_DH)
        heads.append(gates[:, h:h + 1] * oc_ref[:, h * NSA_DH:(h + 1) * NSA_DH]
                     + gates[:, NSA_HEADS + h:NSA_HEADS + h + 1] * o_s[rs, ls]
                     + gates[:, 2 * NSA_HEADS + h:2 * NSA_HEADS + h + 1] * o_w[rs, ls])
    o_ref[...] = _rms_rows(jnp.concatenate(heads, axis=-1), nn_ref[...])


def _s2_call(page_table, ckv_t, q_bf, z, o_c, sel, kvs_bf, cwin_t, win_bf, nn, layer, past_len):
    nbatch, n_pages = page_table.shape
    ts = SAMPLE_ROWS
    wb = cwin_t.shape[3]
    tk = min(2048, past_len)
    nsp = sel.shape[1] // NSA_G
    body = functools.partial(_s2_body, layer=layer, n_pages=n_pages, past_len=past_len, wb=wb, tk=tk)
    blk = lax.broadcasted_iota(jnp.int32, (nsp, past_len), 1) // SEL_BLOCK
    expand = (lax.broadcasted_iota(jnp.int32, (nsp, past_len), 0) == blk).astype(BF16)

    def rows(width, col_block=0):
        return pl.BlockSpec((ts, width), lambda b, pt, c=col_block: (b, c))

    grid_spec = pltpu.PrefetchScalarGridSpec(
        num_scalar_prefetch=1,
        grid=(nbatch,),
        in_specs=[pl.BlockSpec(memory_space=pl.ANY),
                  rows(NSA_WIDTH), rows(128, C_MISC // 128), rows(NSA_WIDTH), rows(sel.shape[1]),
                  rows(2 * KV_WIDTH),
                  pl.BlockSpec((None, None, 2 * KV_WIDTH, wb), lambda b, pt: (layer, b, 0, 0)),
                  rows(2 * KV_WIDTH),
                  pl.BlockSpec((nsp, past_len), lambda b, pt: (0, 0)),
                  pl.BlockSpec((1, NSA_WIDTH), lambda b, pt: (0, 0))],
        out_specs=rows(NSA_WIDTH),
        scratch_shapes=[pltpu.VMEM((2, 2 * KV_WIDTH, past_len), F32),
                        pltpu.SemaphoreType.DMA((2,))])
    return pl.pallas_call(
        body,
        grid_spec=grid_spec,
        out_shape=jax.ShapeDtypeStruct((nbatch * ts, NSA_WIDTH), F32),
        compiler_params=_cparams("arbitrary"),
        name="nsa_sample_attn",
    )(page_table, ckv_t, q_bf, z, o_c, sel, kvs_bf, cwin_t, win_bf, expand, nn)


def _layer_sample(x, mod, pw, l, t_real, ckv, cwin, page_table, s0, conv_prev):
    nb, ts, _ = x.shape
    m = nb * ts
    past_len = page_table.shape[1] * PAGE_SIZE
    sh1, sc1, g1, sh2, sc2, g2 = [jnp.repeat(a, ts, axis=0) for a in _split_mod(mod, (nb, D_MODEL))]
    x2 = x.reshape(m, D_MODEL)
    z = _inproj_call(x2, pw['norm1'][l], sc1, sh1, pw['w_in'][l], m, ts)
    o_g, s_t = _gla_call(z, pw['wa'][l], pw['ba'][l], pw['gla_norm'][l], s0, nb, ts, ts, t_real)
    q_bf, kv_new, win_new, kvs_bf, win_bf = _nsa_prep_call(
        z, pw['q_norm'][l], pw['ks_norm'][l], pw['kw_norm'][l], m)
    o_c, sel = _s1_call(page_table, ckv, q_bf, pw['w_cmp'][l], pw['kc_norm'][l], l, past_len)
    o_n = _s2_call(page_table, ckv, q_bf, z, o_c, sel, kvs_bf, cwin, win_bf, pw['nsa_norm'][l], l, past_len)
    x_mid, h2 = _outproj_call(o_g, o_n, x2, g1, pw['w_out'][l], pw['norm2'][l], sc2, sh2, m, ts)
    zrow = jnp.zeros((nb, ts, D_FF), F32)
    p1 = zrow.at[:, 0].set(conv_prev[:, 1]).reshape(m, D_FF)
    p2 = zrow.at[:, 0].set(conv_prev[:, 0]).at[:, 1].set(conv_prev[:, 1]).reshape(m, D_FF)
    x_out, u = _ffn_call(h2, x_mid, g2, pw['w_up'][l], pw['w_gate'][l], pw['w_down'][l],
                         pw['conv_w'][l], pw['conv_b'][l], m, ts, prev=(p1, p2))
    conv = u.reshape(nb, ts, D_FF)[:, t_real - (CONV_W - 1):t_real]
    return (x_out.reshape(nb, ts, D_MODEL),
            jnp.swapaxes(s_t, -1, -2),
            kv_new.reshape(nb, ts, 4, NSA_G, NSA_DH)[:, :t_real],
            win_new.reshape(nb, ts, 2, NSA_G, NSA_DH)[:, :t_real],
            conv)


def kernel(x_prompt, x_sample, cache_kv, cache_win, state_gla, state_conv, page_table, c_prompt, c_sample,
           norm1, norm2, w_ada, b_ada, w_in, w_gla_a, b_gla_a, gla_norm, q_norm, kc_norm, ks_norm, kw_norm,
           w_ck, w_cv, nsa_norm, w_out, w_up, w_gate, conv_w, conv_b, w_down):
    depth = w_in.shape[0]
    bp, bs, t_s = x_prompt.shape[0], x_sample.shape[0], x_sample.shape[1]
    assert CONV_W - 1 <= t_s <= SAMPLE_ROWS
    pw = _prep_weights({'w_in': w_in, 'w_gla_a': w_gla_a, 'b_gla_a': b_gla_a, 'w_ck': w_ck, 'w_cv': w_cv,
                        'norm1': norm1, 'norm2': norm2, 'gla_norm': gla_norm, 'nsa_norm': nsa_norm,
                        'q_norm': q_norm, 'kc_norm': kc_norm, 'ks_norm': ks_norm, 'kw_norm': kw_norm,
                        'w_out': w_out, 'w_up': w_up, 'w_gate': w_gate, 'w_down': w_down,
                        'conv_w': conv_w, 'conv_b': conv_b})
    bc = _round_up(bp + bs, 8)
    c_all = jnp.concatenate([c_prompt, c_sample, jnp.zeros((bc - bp - bs, D_MODEL), F32)], axis=0)
    mod_all = _mod_call(c_all, w_ada, b_ada)
    n_pool = cache_kv.shape[1]
    wb = cache_win.shape[2]
    ckv = jnp.transpose(cache_kv, (0, 1, 3, 4, 5, 2)).reshape(depth, n_pool, 4 * KV_WIDTH, PAGE_SIZE)
    cwin = jnp.transpose(cache_win, (0, 1, 3, 4, 5, 2)).reshape(depth, bs, 2 * KV_WIDTH, wb)
    xp = x_prompt
    xs = jnp.pad(x_sample, ((0, 0), (0, SAMPLE_ROWS - t_s), (0, 0)))
    outs_p, outs_s = [], []
    for l in range(depth):
        res = _layer_prompt(xp, mod_all[l, :bp], pw, l)
        xp = res[0]
        outs_p.append(res[1:])
        res = _layer_sample(xs, mod_all[l, bp:bp + bs], pw, l, t_s, ckv, cwin, page_table,
                            state_gla[l], state_conv[l])
        xs = res[0]
        outs_s.append(res[1:])
    stack = lambda outs, k: jnp.stack([o[k] for o in outs])
    win_s = jnp.concatenate([cache_win[:, :, t_s:], stack(outs_s, 2)], axis=2)
    return (xp, xs[:, :t_s], stack(outs_p, 1), stack(outs_s, 1), stack(outs_p, 2), win_s,
            stack(outs_p, 0), stack(outs_s, 0), stack(outs_p, 3), stack(outs_s, 3))
```
